```python
import math
import jax, jax.numpy as jnp
from jax import lax
import numpy as np

D_MODEL = 1024
BATCH = 8
SEQ = 8192
DEPTH = 2

N_A_LAYERS = DEPTH // 2
N_B_LAYERS = DEPTH - N_A_LAYERS
EPS = 1e-6
N_ADA = 6

MLSTM_HEADS = 4
MLSTM_QK_DIM = D_MODEL // (2 * MLSTM_HEADS)
MLSTM_V_DIM = D_MODEL // MLSTM_HEADS
MLSTM_CHUNK = 64
GATE_SOFTCAP = 15.0
MLSTM_QK_W = MLSTM_HEADS * MLSTM_QK_DIM
MLSTM_V_W = MLSTM_HEADS * MLSTM_V_DIM
MLSTM_PROJ = 2 * MLSTM_QK_W + MLSTM_V_W + D_MODEL + 2 * MLSTM_HEADS

FOX_HEADS = 8
FOX_HEAD_DIM = D_MODEL // FOX_HEADS
Q_BLOCK = 128

PEER_HEADS = 8
PEER_KEYS = 128
PEER_EXPERTS = PEER_KEYS * PEER_KEYS
PEER_QUERY_DIM = 256
PEER_HALF = PEER_QUERY_DIM // 2
PEER_TOPK = 16
PEER_TOKEN_CHUNK = 128

kernel_name = "yoco_mlstm_fox_peer_adaln"

F32 = jnp.float32


def _ada_rmsnorm(x, gain, shift, scale):
    xf = x.astype(F32)
    xf = xf * lax.rsqrt(jnp.mean(xf * xf, axis=-1, keepdims=True) + EPS) * gain.astype(F32)
    return (xf * (1.0 + scale[:, None, :]) + shift[:, None, :]).astype(x.dtype)


def _head_rms(t, gain):
    tf = t.astype(F32)
    tf = tf * lax.rsqrt(jnp.mean(tf * tf, axis=-1, keepdims=True) + EPS) * gain.astype(F32)
    return tf.astype(t.dtype)


def _softcap(z):
    return GATE_SOFTCAP * jnp.tanh(z / GATE_SOFTCAP)


def _mlstm_chunk_step(carry, inp):
    C, n, m = carry
    q, k, v, ig, lf = inp
    L = q.shape[2]
    causal = jnp.tril(jnp.ones((L, L), dtype=bool))
    b = jnp.cumsum(lf, axis=-1)
    d_log = jnp.where(causal, b[..., :, None] - b[..., None, :] + ig[..., None, :], -jnp.inf)
    inter = b + m[..., None]
    m_t = jnp.maximum(inter, jnp.max(d_log, axis=-1))
    w = jnp.exp(d_log - m_t[..., None])
    a_inter = jnp.exp(inter - m_t)
    s = jnp.einsum('bhtd,bhsd->bhts', q, k) * w
    num = a_inter[..., None] * jnp.einsum('bhtd,bhde->bhte', q, C) + jnp.einsum('bhts,bhse->bhte', s, v)
    den = a_inter * jnp.einsum('bhtd,bhd->bht', q, n) + jnp.sum(s, axis=-1)
    h = num / jnp.maximum(jnp.abs(den), jnp.exp(-m_t))[..., None]
    b_last = b[..., -1]
    g = b_last[..., None] - b + ig
    m_new = jnp.maximum(b_last + m, jnp.max(g, axis=-1))
    w_s = jnp.exp(g - m_new[..., None])
    decay = jnp.exp(b_last + m - m_new)
    C_new = decay[..., None, None] * C + jnp.einsum('bhs,bhsd,bhse->bhde', w_s, k, v)
    n_new = decay[..., None] * n + jnp.einsum('bhs,bhsd->bhd', w_s, k)
    return (C_new, n_new, m_new), h


def _mlstm(xn, w_in, b_i, b_f, h_gain, w_out):
    B, S, _ = xn.shape
    H, DQK, DV, L = MLSTM_HEADS, MLSTM_QK_DIM, MLSTM_V_DIM, MLSTM_CHUNK
    NC = S // L
    p = xn @ w_in
    o1 = MLSTM_QK_W
    o2 = o1 + MLSTM_QK_W
    o3 = o2 + MLSTM_V_W
    o4 = o3 + D_MODEL
    o5 = o4 + H
    q = p[..., :o1].reshape(B, S, H, DQK).astype(F32) * (DQK ** -0.5)
    k = p[..., o1:o2].reshape(B, S, H, DQK).astype(F32)
    v = p[..., o2:o3].reshape(B, S, H, DV).astype(F32)
    o_gate = jax.nn.sigmoid(p[..., o3:o4])
    ig = _softcap(p[..., o4:o5].astype(F32) + b_i.astype(F32))
    lf = jax.nn.log_sigmoid(_softcap(p[..., o5:].astype(F32) + b_f.astype(F32)))

    def to_chunks(t):
        return t.reshape(B, NC, L, H, t.shape[-1]).transpose(1, 0, 3, 2, 4)

    def gate_chunks(t):
        return t.reshape(B, NC, L, H).transpose(1, 0, 3, 2)

    init = (jnp.zeros((B, H, DQK, DV), F32), jnp.zeros((B, H, DQK), F32), jnp.zeros((B, H), F32))
    _, h = lax.scan(_mlstm_chunk_step, init,
                    (to_chunks(q), to_chunks(k), to_chunks(v), gate_chunks(ig), gate_chunks(lf)))
    h = h.transpose(1, 0, 3, 2, 4).reshape(B, S, H, DV)
    h = _head_rms(h, h_gain.reshape(H, DV)).reshape(B, S, D_MODEL).astype(xn.dtype)
    return (o_gate * h) @ w_out


def _shared_kv(x, cs, ada_w, ada_b, norm_g, w_kv, b_f, k_gain):
    B, S, _ = x.shape
    mod = cs @ ada_w.astype(F32) + ada_b.astype(F32)
    sh, sc = jnp.split(mod, 2, axis=-1)
    xn = _ada_rmsnorm(x, norm_g, sh, sc)
    p = xn @ w_kv
    k = _head_rms(p[..., :D_MODEL].reshape(B, S, FOX_HEADS, FOX_HEAD_DIM), k_gain)
    v = p[..., D_MODEL:2 * D_MODEL].reshape(B, S, FOX_HEADS, FOX_HEAD_DIM)
    log_f = jax.nn.log_sigmoid(p[..., 2 * D_MODEL:].astype(F32) + b_f.astype(F32))
    f_cum = jnp.cumsum(log_f, axis=1).transpose(0, 2, 1)
    return k.transpose(0, 2, 1, 3), v.transpose(0, 2, 1, 3), f_cum


def _fox(xn, w_qo, q_gain, w_out, k, v, f_cum):
    B, S, _ = xn.shape
    p = xn @ w_qo
    q = _head_rms(p[..., :D_MODEL].reshape(B, S, FOX_HEADS, FOX_HEAD_DIM), q_gain).transpose(0, 2, 1, 3)
    o_gate = jax.nn.sigmoid(p[..., D_MODEL:])
    scale = FOX_HEAD_DIM ** -0.5
    outs = []
    for blk in range(S // Q_BLOCK):
        lo, hi = blk * Q_BLOCK, (blk + 1) * Q_BLOCK
        logits = jnp.einsum('bhqd,bhkd->bhqk', q[:, :, lo:hi], k[:, :, :hi]).astype(F32) * scale
        logits = logits + f_cum[:, :, lo:hi, None] - f_cum[:, :, None, :hi]
        mask = jnp.arange(hi)[None, :] <= jnp.arange(lo, hi)[:, None]
        probs = jax.nn.softmax(jnp.where(mask, logits, -jnp.inf), axis=-1)
        outs.append(jnp.einsum('bhqk,bhkd->bhqd', probs.astype(v.dtype), v[:, :, :hi]))
    att = jnp.concatenate(outs, axis=2).transpose(0, 2, 1, 3).reshape(B, S, D_MODEL)
    return (o_gate * att) @ w_out


def _peer(xn, wq, subkeys, u, v):
    B, S, D = xn.shape
    tokens = xn.reshape(B * S // PEER_TOKEN_CHUNK, PEER_TOKEN_CHUNK, D)
    K = PEER_TOPK

    def chunk(xc):
        T = xc.shape[0]
        q = (xc @ wq).reshape(T, PEER_HEADS, 2, PEER_HALF)
        s = jnp.einsum('thpd,pkd->thpk', q, subkeys).astype(F32)
        s_top, i_top = lax.top_k(s, K)
        cand_s = (s_top[:, :, 0, :, None] + s_top[:, :, 1, None, :]).reshape(T, PEER_HEADS, K * K)
        cand_i = (i_top[:, :, 0, :, None] * PEER_KEYS + i_top[:, :, 1, None, :]).reshape(T, PEER_HEADS, K * K)
        best_s, pos = lax.top_k(cand_s, K)
        idx = jnp.take_along_axis(cand_i, pos, axis=-1)
        gate = jax.nn.softmax(best_s, axis=-1)
        u_sel = jnp.take(u, idx, axis=0)
        act = jax.nn.gelu(jnp.einsum('thkd,td->thk', u_sel, xc).astype(F32), approximate=False)
        v_sel = jnp.take(v, idx, axis=0)
        return jnp.einsum('thk,thkd->td', (gate * act).astype(v.dtype), v_sel)

    return lax.map(chunk, tokens).reshape(B, S, D)


def setup_inputs(seed: int = 0) -> dict:
    key = jax.random.key(seed)
    ks = jax.random.split(key, 26)
    D = D_MODEL
    s = D ** -0.5

    def nrm(k, shape, scale):
        return jax.random.normal(k, shape, F32) * scale

    def gain(k, shape):
        return 1.0 + 0.02 * jax.random.normal(k, shape, F32)

    return {
        "x": nrm(ks[0], (BATCH, SEQ, D), 1.0),
        "c": nrm(ks[1], (BATCH, D), 1.0),
        "ada_w": nrm(ks[2], (DEPTH, D, N_ADA * D), s),
        "ada_b": nrm(ks[3], (DEPTH, N_ADA * D), 0.02),
        "mix_norm": gain(ks[4], (DEPTH, D)),
        "ffn_norm": gain(ks[5], (DEPTH, D)),
        "a_w_in": nrm(ks[6], (N_A_LAYERS, D, MLSTM_PROJ), s),
        "a_b_i": nrm(ks[7], (N_A_LAYERS, MLSTM_HEADS), 0.1),
        "a_b_f": 3.0 + 3.0 * jax.random.uniform(ks[8], (N_A_LAYERS, MLSTM_HEADS), F32),
        "a_h_norm": gain(ks[9], (N_A_LAYERS, D)),
        "a_w_out": nrm(ks[10], (N_A_LAYERS, D, D), s),
        "kv_ada_w": nrm(ks[11], (D, 2 * D), s),
        "kv_ada_b": nrm(ks[12], (2 * D,), 0.02),
        "kv_norm": gain(ks[13], (D,)),
        "kv_w": nrm(ks[14], (D, 2 * D + FOX_HEADS), s),
        "kv_b_f": 2.0 + 0.5 * jax.random.normal(ks[15], (FOX_HEADS,), F32),
        "kv_k_norm": gain(ks[16], (FOX_HEAD_DIM,)),
        "b_w_qo": nrm(ks[17], (N_B_LAYERS, D, 2 * D), s),
        "b_q_norm": gain(ks[18], (N_B_LAYERS, FOX_HEAD_DIM)),
        "b_w_out": nrm(ks[19], (N_B_LAYERS, D, D), s),
        "peer_wq": nrm(ks[20], (DEPTH, D, PEER_HEADS * PEER_QUERY_DIM), s),
        "peer_subkeys": nrm(ks[21], (DEPTH, 2, PEER_KEYS, PEER_HALF), PEER_HALF ** -0.5),
        "peer_u": nrm(ks[22], (DEPTH, PEER_EXPERTS, D), s),
        "peer_v": nrm(ks[23], (DEPTH, PEER_EXPERTS, D), PEER_HEADS ** -0.5),
    }


def reference(x, c, ada_w, ada_b, mix_norm, ffn_norm, a_w_in, a_b_i, a_b_f, a_h_norm, a_w_out,
              kv_ada_w, kv_ada_b, kv_norm, kv_w, kv_b_f, kv_k_norm, b_w_qo, b_q_norm, b_w_out,
              peer_wq, peer_subkeys, peer_u, peer_v):
    cs = jax.nn.silu(c.astype(F32))
    k_sh = v_sh = f_cum = None
    for l in range(DEPTH):
        mod = cs @ ada_w[l].astype(F32) + ada_b[l].astype(F32)
        sh1, sc1, g1, sh2, sc2, g2 = jnp.split(mod, N_ADA, axis=-1)
        xn = _ada_rmsnorm(x, mix_norm[l], sh1, sc1)
        if l < N_A_LAYERS:
            y = _mlstm(xn, a_w_in[l], a_b_i[l], a_b_f[l], a_h_norm[l], a_w_out[l])
        else:
            j = l - N_A_LAYERS
            y = _fox(xn, b_w_qo[j], b_q_norm[j], b_w_out[j], k_sh, v_sh, f_cum)
        x = x + (g1[:, None, :] * y.astype(F32)).astype(x.dtype)
        xn = _ada_rmsnorm(x, ffn_norm[l], sh2, sc2)
        y = _peer(xn, peer_wq[l], peer_subkeys[l], peer_u[l], peer_v[l])
        x = x + (g2[:, None, :] * y.astype(F32)).astype(x.dtype)
        if l == N_A_LAYERS - 1:
            k_sh, v_sh, f_cum = _shared_kv(x, cs, kv_ada_w, kv_ada_b, kv_norm, kv_w, kv_b_f, kv_k_norm)
    return x
```

```python
import functools
import math

import jax
import jax.numpy as jnp
from jax import lax
from jax.experimental import pallas as pl
from jax.experimental.pallas import tpu as pltpu

F32 = jnp.float32
BF16 = jnp.bfloat16

D_MODEL = 1024
DEPTH = 2
N_A_LAYERS = DEPTH // 2
EPS = 1e-6
N_ADA = 6

MLSTM_HEADS = 4
MLSTM_QK_DIM = D_MODEL // (2 * MLSTM_HEADS)
MLSTM_V_DIM = D_MODEL // MLSTM_HEADS
MLSTM_CHUNK = 64
GATE_SOFTCAP = 15.0
MLSTM_QK_W = MLSTM_HEADS * MLSTM_QK_DIM
MLSTM_V_W = MLSTM_HEADS * MLSTM_V_DIM

FOX_HEADS = 8
FOX_HEAD_DIM = D_MODEL // FOX_HEADS
Q_BLOCK = 128

PEER_HEADS = 8
PEER_KEYS = 128
PEER_EXPERTS = PEER_KEYS * PEER_KEYS
PEER_QUERY_DIM = 256
PEER_HALF = PEER_QUERY_DIM // 2
PEER_TOPK = 16

SUBLANES = 8
LANES = 128
ROW_WORDS = SUBLANES * LANES
assert ROW_WORDS == D_MODEL
PEER_SLOTS = PEER_HEADS * PEER_TOPK
HALF_EXPERTS = PEER_EXPERTS // 2
SUBROWS = 2 * SUBLANES
GATHER_ROWS = PEER_SLOTS * SUBROWS
TOKEN_GROUP = SUBLANES
PEER_TOKEN_TILE = 64
PEER_VMEM_LIMIT = 48 * 1024 * 1024


def _pack_expert_table(t):
    lo = lax.bitcast_convert_type(t[:HALF_EXPERTS].astype(BF16), jnp.uint16).astype(jnp.uint32)
    hi = lax.bitcast_convert_type(t[HALF_EXPERTS:].astype(BF16), jnp.uint16).astype(jnp.uint32)
    w = lo | (hi << 16)
    return lax.bitcast_convert_type(w, jnp.int32).reshape(HALF_EXPERTS, SUBLANES, LANES)


def _slot_expand_matrix():
    slot = lax.broadcasted_iota(jnp.int32, (PEER_SLOTS, GATHER_ROWS), 0)
    sub = lax.broadcasted_iota(jnp.int32, (PEER_SLOTS, GATHER_ROWS), 1)
    return (sub // SUBROWS == slot).astype(BF16)


def _diag_mask():
    r = lax.broadcasted_iota(jnp.int32, (SUBLANES, GATHER_ROWS), 0)
    sub = lax.broadcasted_iota(jnp.int32, (SUBLANES, GATHER_ROWS), 1)
    return (sub % SUBROWS) // 2 == r


def _half_mask(hs, e):
    hsx = jnp.dot(hs.astype(BF16), e, preferred_element_type=F32)
    par = (lax.broadcasted_iota(jnp.int32, hsx.shape, 1) % 2).astype(F32)
    return hsx == par


def _gather_token(idx_ref, tab_ref, gb_ref, t):
    for s in range(PEER_SLOTS):
        gb_ref[pl.ds(s * SUBLANES, SUBLANES), :] = tab_ref[idx_ref[t, s]]
    return pltpu.bitcast(gb_ref[...], BF16)


def _split_hi_lo(a):
    hi = a.astype(BF16)
    lo = (a - hi.astype(F32)).astype(BF16)
    return hi, lo


def _peer_down_kernel(idx_ref, xs_ref, hs_ref, gate_ref, tab_ref, e_ref, sel_ref, w_ref, gb_ref, z_ref):
    tt = w_ref.shape[0]
    diag = _diag_mask()

    def group(g, carry):
        rows = []
        for i in range(TOKEN_GROUP):
            t = g * TOKEN_GROUP + i
            gath = _gather_token(idx_ref, tab_ref, gb_ref, t)
            y = lax.dot_general(xs_ref[t], gath, (((1,), (1,)), ((), ())),
                                preferred_element_type=F32)
            y8 = y[:SUBLANES] + y[SUBLANES:]
            rows.append(jnp.sum(jnp.where(diag, y8, 0.0), axis=0, keepdims=True))
        z_ref[pl.ds(pl.multiple_of(g * TOKEN_GROUP, TOKEN_GROUP), TOKEN_GROUP), :] = jnp.concatenate(rows, axis=0)
        return carry

    lax.fori_loop(0, tt // TOKEN_GROUP, group, 0)
    zm = jnp.where(_half_mask(hs_ref[...], e_ref[...]), z_ref[...], 0.0)
    act = jnp.dot(zm, sel_ref[...], precision=lax.Precision.HIGHEST, preferred_element_type=F32)
    gelu = 0.5 * act * (1.0 + lax.erf(act * (1.0 / math.sqrt(2.0))))
    w_ref[...] = gate_ref[...] * gelu


def _peer_up_kernel(idx_ref, w_ref, hs_ref, tab_ref, e_ref, y_ref, gb_ref, ahi_ref, alo_ref):
    tt = w_ref.shape[0]
    diag = _diag_mask()
    e = e_ref[...]
    hm = _half_mask(hs_ref[...], e)
    w_hi, w_lo = _split_hi_lo(w_ref[...])
    ahi_ref[...] = jnp.where(hm, jnp.dot(w_hi, e, preferred_element_type=F32), 0.0)
    alo_ref[...] = jnp.where(hm, jnp.dot(w_lo, e, preferred_element_type=F32), 0.0)

    def group(g, carry):
        base = pl.multiple_of(g * TOKEN_GROUP, TOKEN_GROUP)
        a_hi = ahi_ref[pl.ds(base, TOKEN_GROUP), :]
        a_lo = alo_ref[pl.ds(base, TOKEN_GROUP), :]
        for i in range(TOKEN_GROUP):
            t = g * TOKEN_GROUP + i
            gath = _gather_token(idx_ref, tab_ref, gb_ref, t)
            lhs = jnp.concatenate(
                [jnp.where(diag, a_hi[i:i + 1, :], 0.0), jnp.where(diag, a_lo[i:i + 1, :], 0.0)],
                axis=0).astype(BF16)
            out = jnp.dot(lhs, gath, preferred_element_type=F32)
            y_ref[t] = out[:SUBLANES] + out[SUBLANES:]
        return carry

    lax.fori_loop(0, tt // TOKEN_GROUP, group, 0)


def _peer_experts(xn2, idx, gate, tab_u, tab_v, *, token_tile=PEER_TOKEN_TILE, interpret=False):
    n = xn2.shape[0]
    tt = token_tile
    assert n % tt == 0 and tt % TOKEN_GROUP == 0
    rows = idx % HALF_EXPERTS
    hs = (idx // HALF_EXPERTS).astype(F32)
    x_hi, x_lo = _split_hi_lo(xn2)
    xs = jnp.concatenate([x_hi.reshape(n, SUBLANES, LANES), x_lo.reshape(n, SUBLANES, LANES)], axis=1)
    e = _slot_expand_matrix()
    sel = e.T.astype(F32)

    tok = lambda i: (i, 0)
    const2 = lambda i: (0, 0)
    smem_idx = pl.BlockSpec((tt, PEER_SLOTS), tok, memory_space=pltpu.SMEM)
    slot_spec = pl.BlockSpec((tt, PEER_SLOTS), tok)
    table_spec = pl.BlockSpec(memory_space=pltpu.VMEM)
    params = pltpu.CompilerParams(dimension_semantics=("arbitrary",), vmem_limit_bytes=PEER_VMEM_LIMIT)

    w = pl.pallas_call(
        _peer_down_kernel,
        grid=(n // tt,),
        in_specs=[smem_idx,
                  pl.BlockSpec((tt, SUBROWS, LANES), lambda i: (i, 0, 0)),
                  slot_spec, slot_spec, table_spec,
                  pl.BlockSpec((PEER_SLOTS, GATHER_ROWS), const2),
                  pl.BlockSpec((GATHER_ROWS, PEER_SLOTS), const2)],
        out_specs=slot_spec,
        out_shape=jax.ShapeDtypeStruct((n, PEER_SLOTS), F32),
        scratch_shapes=[pltpu.VMEM((PEER_SLOTS * SUBLANES, LANES), jnp.int32),
                        pltpu.VMEM((tt, GATHER_ROWS), F32)],
        compiler_params=params,
        interpret=interpret,
        name="peer_down",
    )(rows, xs, hs, gate, tab_u, e, sel)

    y = pl.pallas_call(
        _peer_up_kernel,
        grid=(n // tt,),
        in_specs=[smem_idx, slot_spec, slot_spec, table_spec,
                  pl.BlockSpec((PEER_SLOTS, GATHER_ROWS), const2)],
        out_specs=pl.BlockSpec((tt, SUBLANES, LANES), lambda i: (i, 0, 0)),
        out_shape=jax.ShapeDtypeStruct((n, SUBLANES, LANES), F32),
        scratch_shapes=[pltpu.VMEM((PEER_SLOTS * SUBLANES, LANES), jnp.int32),
                        pltpu.VMEM((tt, GATHER_ROWS), F32),
                        pltpu.VMEM((tt, GATHER_ROWS), F32)],
        compiler_params=params,
        interpret=interpret,
        name="peer_up",
    )(rows, w, hs, tab_v, e)
    return y.reshape(n, D_MODEL)


def _ada_rmsnorm(x, gain, shift, scale):
    xf = x * lax.rsqrt(jnp.mean(x * x, axis=-1, keepdims=True) + EPS) * gain
    return xf * (1.0 + scale[:, None, :]) + shift[:, None, :]


def _head_rms(t, gain):
    return t * lax.rsqrt(jnp.mean(t * t, axis=-1, keepdims=True) + EPS) * gain


def _softcap(z):
    return GATE_SOFTCAP * jnp.tanh(z / GATE_SOFTCAP)


def _peer_route(xc, wq, subkeys):
    T = xc.shape[0]
    K = PEER_TOPK
    q = (xc @ wq).reshape(T, PEER_HEADS, 2, PEER_HALF)
    s = jnp.einsum('thpd,pkd->thpk', q, subkeys)
    s_top, i_top = lax.top_k(s, K)
    cand_s = (s_top[:, :, 0, :, None] + s_top[:, :, 1, None, :]).reshape(T, PEER_HEADS, K * K)
    cand_i = (i_top[:, :, 0, :, None] * PEER_KEYS + i_top[:, :, 1, None, :]).reshape(T, PEER_HEADS, K * K)
    best_s, pos = lax.top_k(cand_s, K)
    idx = jnp.take_along_axis(cand_i, pos, axis=-1)
    gate = jax.nn.softmax(best_s, axis=-1)
    return idx.reshape(T, PEER_SLOTS), gate.reshape(T, PEER_SLOTS)


def _peer(xn, wq, subkeys, tab_u, tab_v):
    B, S, D = xn.shape
    idx, gate = lax.map(lambda xc: _peer_route(xc, wq, subkeys), xn)
    y = _peer_experts(xn.reshape(B * S, D), idx.reshape(B * S, PEER_SLOTS),
                      gate.reshape(B * S, PEER_SLOTS), tab_u, tab_v)
    return y.reshape(B, S, D)


def _mlstm_chunk_step(carry, inp):
    C, n, m = carry
    q, k, v, ig, lf = inp
    L = q.shape[2]
    causal = jnp.tril(jnp.ones((L, L), dtype=bool))
    b = jnp.cumsum(lf, axis=-1)
    d_log = jnp.where(causal, b[..., :, None] - b[..., None, :] + ig[..., None, :], -jnp.inf)
    inter = b + m[..., None]
    m_t = jnp.maximum(inter, jnp.max(d_log, axis=-1))
    w = jnp.exp(d_log - m_t[..., None])
    a_inter = jnp.exp(inter - m_t)
    s = jnp.einsum('bhtd,bhsd->bhts', q, k) * w
    num = a_inter[..., None] * jnp.einsum('bhtd,bhde->bhte', q, C) + jnp.einsum('bhts,bhse->bhte', s, v)
    den = a_inter * jnp.einsum('bhtd,bhd->bht', q, n) + jnp.sum(s, axis=-1)
    h = num / jnp.maximum(jnp.abs(den), jnp.exp(-m_t))[..., None]
    b_last = b[..., -1]
    g = b_last[..., None] - b + ig
    m_new = jnp.maximum(b_last + m, jnp.max(g, axis=-1))
    w_s = jnp.exp(g - m_new[..., None])
    decay = jnp.exp(b_last + m - m_new)
    C_new = decay[..., None, None] * C + jnp.einsum('bhs,bhsd,bhse->bhde', w_s, k, v)
    n_new = decay[..., None] * n + jnp.einsum('bhs,bhsd->bhd', w_s, k)
    return (C_new, n_new, m_new), h


def _mlstm(xn, w_in, b_i, b_f, h_gain, w_out):
    B, S, _ = xn.shape
    H, DQK, DV, L = MLSTM_HEADS, MLSTM_QK_DIM, MLSTM_V_DIM, MLSTM_CHUNK
    NC = S // L
    p = xn @ w_in
    o1 = MLSTM_QK_W
    o2 = o1 + MLSTM_QK_W
    o3 = o2 + MLSTM_V_W
    o4 = o3 + D_MODEL
    o5 = o4 + H
    q = p[..., :o1].reshape(B, S, H, DQK) * (DQK ** -0.5)
    k = p[..., o1:o2].reshape(B, S, H, DQK)
    v = p[..., o2:o3].reshape(B, S, H, DV)
    o_gate = jax.nn.sigmoid(p[..., o3:o4])
    ig = _softcap(p[..., o4:o5] + b_i)
    lf = jax.nn.log_sigmoid(_softcap(p[..., o5:] + b_f))

    def to_chunks(t):
        return t.reshape(B, NC, L, H, t.shape[-1]).transpose(1, 0, 3, 2, 4)

    def gate_chunks(t):
        return t.reshape(B, NC, L, H).transpose(1, 0, 3, 2)

    init = (jnp.zeros((B, H, DQK, DV), F32), jnp.zeros((B, H, DQK), F32), jnp.zeros((B, H), F32))
    _, h = lax.scan(_mlstm_chunk_step, init,
                    (to_chunks(q), to_chunks(k), to_chunks(v), gate_chunks(ig), gate_chunks(lf)))
    h = h.transpose(1, 0, 3, 2, 4).reshape(B, S, H, DV)
    h = _head_rms(h, h_gain.reshape(H, DV)).reshape(B, S, D_MODEL)
    return (o_gate * h) @ w_out


def _shared_kv(x, cs, ada_w, ada_b, norm_g, w_kv, b_f, k_gain):
    B, S, _ = x.shape
    mod = cs @ ada_w + ada_b
    sh, sc = jnp.split(mod, 2, axis=-1)
    xn = _ada_rmsnorm(x, norm_g, sh, sc)
    p = xn @ w_kv
    k = _head_rms(p[..., :D_MODEL].reshape(B, S, FOX_HEADS, FOX_HEAD_DIM), k_gain)
    v = p[..., D_MODEL:2 * D_MODEL].reshape(B, S, FOX_HEADS, FOX_HEAD_DIM)
    log_f = jax.nn.log_sigmoid(p[..., 2 * D_MODEL:] + b_f)
    f_cum = jnp.cumsum(log_f, axis=1).transpose(0, 2, 1)
    return k.transpose(0, 2, 1, 3), v.transpose(0, 2, 1, 3), f_cum


def _fox(xn, w_qo, q_gain, w_out, k, v, f_cum):
    B, S, _ = xn.shape
    p = xn @ w_qo
    q = _head_rms(p[..., :D_MODEL].reshape(B, S, FOX_HEADS, FOX_HEAD_DIM), q_gain).transpose(0, 2, 1, 3)
    o_gate = jax.nn.sigmoid(p[..., D_MODEL:])
    scale = FOX_HEAD_DIM ** -0.5
    outs = []
    for blk in range(S // Q_BLOCK):
        lo, hi = blk * Q_BLOCK, (blk + 1) * Q_BLOCK
        logits = jnp.einsum('bhqd,bhkd->bhqk', q[:, :, lo:hi], k[:, :, :hi]) * scale
        logits = logits + f_cum[:, :, lo:hi, None] - f_cum[:, :, None, :hi]
        mask = jnp.arange(hi)[None, :] <= jnp.arange(lo, hi)[:, None]
        probs = jax.nn.softmax(jnp.where(mask, logits, -jnp.inf), axis=-1)
        outs.append(jnp.einsum('bhqk,bhkd->bhqd', probs, v[:, :, :hi]))
    att = jnp.concatenate(outs, axis=2).transpose(0, 2, 1, 3).reshape(B, S, D_MODEL)
    return (o_gate * att) @ w_out


def kernel(x, c, ada_w, ada_b, mix_norm, ffn_norm, a_w_in, a_b_i, a_b_f, a_h_norm, a_w_out,
           kv_ada_w, kv_ada_b, kv_norm, kv_w, kv_b_f, kv_k_norm, b_w_qo, b_q_norm, b_w_out,
           peer_wq, peer_subkeys, peer_u, peer_v):
    cs = jax.nn.silu(c)
    k_sh = v_sh = f_cum = None
    for l in range(DEPTH):
        mod = cs @ ada_w[l] + ada_b[l]
        sh1, sc1, g1, sh2, sc2, g2 = jnp.split(mod, N_ADA, axis=-1)
        xn = _ada_rmsnorm(x, mix_norm[l], sh1, sc1)
        if l < N_A_LAYERS:
            y = _mlstm(xn, a_w_in[l], a_b_i[l], a_b_f[l], a_h_norm[l], a_w_out[l])
        else:
            j = l - N_A_LAYERS
            y = _fox(xn, b_w_qo[j], b_q_norm[j], b_w_out[j], k_sh, v_sh, f_cum)
        x = x + g1[:, None, :] * y
        xn = _ada_rmsnorm(x, ffn_norm[l], sh2, sc2)
        y = _peer(xn, peer_wq[l], peer_subkeys[l],
                  _pack_expert_table(peer_u[l]), _pack_expert_table(peer_v[l]))
        x = x + g2[:, None, :] * y
        if l == N_A_LAYERS - 1:
            k_sh, v_sh, f_cum = _shared_kv(x, cs, kv_ada_w, kv_ada_b, kv_norm, kv_w, kv_b_f, kv_k_norm)
    return x
```

```python
import functools
import math

import jax
import jax.numpy as jnp
from jax import lax
from jax.experimental import pallas as pl
from jax.experimental.pallas import tpu as pltpu

F32 = jnp.float32
BF16 = jnp.bfloat16

D_MODEL = 1024
DEPTH = 2
N_A_LAYERS = DEPTH // 2
EPS = 1e-6
N_ADA = 6

MLSTM_HEADS = 4
MLSTM_QK_DIM = D_MODEL // (2 * MLSTM_HEADS)
MLSTM_V_DIM = D_MODEL // MLSTM_HEADS
MLSTM_CHUNK = 64
GATE_SOFTCAP = 15.0
MLSTM_QK_W = MLSTM_HEADS * MLSTM_QK_DIM
MLSTM_V_W = MLSTM_HEADS * MLSTM_V_DIM

FOX_HEADS = 8
FOX_HEAD_DIM = D_MODEL // FOX_HEADS
Q_BLOCK = 128

PEER_HEADS = 8
PEER_KEYS = 128
PEER_EXPERTS = PEER_KEYS * PEER_KEYS
PEER_QUERY_DIM = 256
PEER_HALF = PEER_QUERY_DIM // 2
PEER_TOPK = 16

SUBLANES = 8
LANES = 128
ROW_WORDS = SUBLANES * LANES
assert ROW_WORDS == D_MODEL
PEER_SLOTS = PEER_HEADS * PEER_TOPK
HALF_EXPERTS = PEER_EXPERTS // 2
SUBROWS = 2 * SUBLANES
GATHER_ROWS = PEER_SLOTS * SUBROWS
TOKEN_GROUP = SUBLANES
PEER_TOKEN_TILE = 64
PEER_VMEM_LIMIT = 48 * 1024 * 1024


def _pack_expert_table(t):
    lo = lax.bitcast_convert_type(t[:HALF_EXPERTS].astype(BF16), jnp.uint16).astype(jnp.uint32)
    hi = lax.bitcast_convert_type(t[HALF_EXPERTS:].astype(BF16), jnp.uint16).astype(jnp.uint32)
    w = lo | (hi << 16)
    return lax.bitcast_convert_type(w, jnp.int32).reshape(HALF_EXPERTS, SUBLANES, LANES)


def _slot_expand_matrix():
    slot = lax.broadcasted_iota(jnp.int32, (PEER_SLOTS, GATHER_ROWS), 0)
    sub = lax.broadcasted_iota(jnp.int32, (PEER_SLOTS, GATHER_ROWS), 1)
    return (sub // SUBROWS == slot).astype(BF16)


def _diag_mask():
    r = lax.broadcasted_iota(jnp.int32, (SUBLANES, GATHER_ROWS), 0)
    sub = lax.broadcasted_iota(jnp.int32, (SUBLANES, GATHER_ROWS), 1)
    return (sub % SUBROWS) // 2 == r


def _half_mask(hs, e):
    hsx = jnp.dot(hs.astype(BF16), e, preferred_element_type=F32)
    par = (lax.broadcasted_iota(jnp.int32, hsx.shape, 1) % 2).astype(F32)
    return hsx == par


def _gather_token(idx_ref, tab_ref, gb_ref, t):
    for s in range(PEER_SLOTS):
        gb_ref[pl.ds(s * SUBLANES, SUBLANES), :] = tab_ref[idx_ref[t, s]]
    return pltpu.bitcast(gb_ref[...], BF16)


def _split_hi_lo(a):
    hi = a.astype(BF16)
    lo = (a - hi.astype(F32)).astype(BF16)
    return hi, lo


def _peer_down_kernel(idx_ref, xs_ref, hs_ref, gate_ref, tab_ref, e_ref, sel_ref, w_ref, gb_ref, z_ref):
    tt = w_ref.shape[0]
    diag = _diag_mask()

    def group(g, carry):
        rows = []
        for i in range(TOKEN_GROUP):
            t = g * TOKEN_GROUP + i
            gath = _gather_token(idx_ref, tab_ref, gb_ref, t)
            y = lax.dot_general(xs_ref[t], gath, (((1,), (1,)), ((), ())),
                                preferred_element_type=F32)
            y8 = y[:SUBLANES] + y[SUBLANES:]
            rows.append(jnp.sum(jnp.where(diag, y8, 0.0), axis=0, keepdims=True))
        z_ref[pl.ds(pl.multiple_of(g * TOKEN_GROUP, TOKEN_GROUP), TOKEN_GROUP), :] = jnp.concatenate(rows, axis=0)
        return carry

    lax.fori_loop(0, tt // TOKEN_GROUP, group, 0)
    zm = jnp.where(_half_mask(hs_ref[...], e_ref[...]), z_ref[...], 0.0)
    act = jnp.dot(zm, sel_ref[...], precision=lax.Precision.HIGHEST, preferred_element_type=F32)
    gelu = 0.5 * act * (1.0 + lax.erf(act * (1.0 / math.sqrt(2.0))))
    w_ref[...] = gate_ref[...] * gelu


def _peer_up_kernel(idx_ref, w_ref, hs_ref, tab_ref, e_ref, y_ref, gb_ref, ahi_ref, alo_ref):
    tt = w_ref.shape[0]
    diag = _diag_mask()
    e = e_ref[...]
    hm = _half_mask(hs_ref[...], e)
    w_hi, w_lo = _split_hi_lo(w_ref[...])
    ahi_ref[...] = jnp.where(hm, jnp.dot(w_hi, e, preferred_element_type=F32), 0.0)
    alo_ref[...] = jnp.where(hm, jnp.dot(w_lo, e, preferred_element_type=F32), 0.0)

    def group(g, carry):
        base = pl.multiple_of(g * TOKEN_GROUP, TOKEN_GROUP)
        a_hi = ahi_ref[pl.ds(base, TOKEN_GROUP), :]
        a_lo = alo_ref[pl.ds(base, TOKEN_GROUP), :]
        for i in range(TOKEN_GROUP):
            t = g * TOKEN_GROUP + i
            gath = _gather_token(idx_ref, tab_ref, gb_ref, t)
            lhs = jnp.concatenate(
                [jnp.where(diag, a_hi[i:i + 1, :], 0.0), jnp.where(diag, a_lo[i:i + 1, :], 0.0)],
                axis=0).astype(BF16)
            out = jnp.dot(lhs, gath, preferred_element_type=F32)
            y_ref[t] = out[:SUBLANES] + out[SUBLANES:]
        return carry

    lax.fori_loop(0, tt // TOKEN_GROUP, group, 0)


def _peer_experts(xs, rows, hs, gate, tab_u, tab_v, *, token_tile=PEER_TOKEN_TILE, interpret=False):
    n = xs.shape[0]
    tt = token_tile
    assert n % tt == 0 and tt % TOKEN_GROUP == 0
    e = _slot_expand_matrix()
    sel = e.T.astype(F32)

    tok = lambda i: (i, 0)
    const2 = lambda i: (0, 0)
    smem_idx = pl.BlockSpec((tt, PEER_SLOTS), tok, memory_space=pltpu.SMEM)
    slot_spec = pl.BlockSpec((tt, PEER_SLOTS), tok)
    table_spec = pl.BlockSpec(memory_space=pltpu.VMEM)
    params = pltpu.CompilerParams(dimension_semantics=("arbitrary",), vmem_limit_bytes=PEER_VMEM_LIMIT)

    w = pl.pallas_call(
        _peer_down_kernel,
        grid=(n // tt,),
        in_specs=[smem_idx,
                  pl.BlockSpec((tt, SUBROWS, LANES), lambda i: (i, 0, 0)),
                  slot_spec, slot_spec, table_spec,
                  pl.BlockSpec((PEER_SLOTS, GATHER_ROWS), const2),
                  pl.BlockSpec((GATHER_ROWS, PEER_SLOTS), const2)],
        out_specs=slot_spec,
        out_shape=jax.ShapeDtypeStruct((n, PEER_SLOTS), F32),
        scratch_shapes=[pltpu.VMEM((PEER_SLOTS * SUBLANES, LANES), jnp.int32),
                        pltpu.VMEM((tt, GATHER_ROWS), F32)],
        compiler_params=params,
        interpret=interpret,
        name="peer_down",
    )(rows, xs, hs, gate, tab_u, e, sel)

    y = pl.pallas_call(
        _peer_up_kernel,
        grid=(n // tt,),
        in_specs=[smem_idx, slot_spec, slot_spec, table_spec,
                  pl.BlockSpec((PEER_SLOTS, GATHER_ROWS), const2)],
        out_specs=pl.BlockSpec((tt, SUBLANES, LANES), lambda i: (i, 0, 0)),
        out_shape=jax.ShapeDtypeStruct((n, SUBLANES, LANES), F32),
        scratch_shapes=[pltpu.VMEM((PEER_SLOTS * SUBLANES, LANES), jnp.int32),
                        pltpu.VMEM((tt, GATHER_ROWS), F32),
                        pltpu.VMEM((tt, GATHER_ROWS), F32)],
        compiler_params=params,
        interpret=interpret,
        name="peer_up",
    )(rows, w, hs, tab_v, e)
    return y.reshape(n, D_MODEL)


def _ada_rmsnorm(x, gain, shift, scale):
    xf = x * lax.rsqrt(jnp.mean(x * x, axis=-1, keepdims=True) + EPS) * gain
    return xf * (1.0 + scale[:, None, :]) + shift[:, None, :]


def _head_rms(t, gain):
    return t * lax.rsqrt(jnp.mean(t * t, axis=-1, keepdims=True) + EPS) * gain


def _softcap(z):
    return GATE_SOFTCAP * jnp.tanh(z / GATE_SOFTCAP)


def _dot3(a_hi, a_lo, b_hi, b_lo, dims):
    dot = functools.partial(lax.dot_general, dimension_numbers=(dims, ((), ())), preferred_element_type=F32)
    return dot(a_hi, b_hi) + (dot(a_hi, b_lo) + dot(a_lo, b_hi))


def _top_rows(s, k):
    nrows = s.shape[0]
    row = lax.broadcasted_iota(jnp.int32, s.shape, 0)
    vals, rows = [], []
    for _ in range(k):
        m = jnp.max(s, axis=0, keepdims=True)
        r = jnp.min(jnp.where(s == m, row, nrows), axis=0, keepdims=True)
        vals.append(m)
        rows.append(r)
        s = jnp.where(row == r, -jnp.inf, s)
    return jnp.concatenate(vals, axis=0), jnp.concatenate(rows, axis=0)


def _tree(op, xs):
    while len(xs) > 1:
        xs = [op(xs[i], xs[i + 1]) if i + 1 < len(xs) else xs[i] for i in range(0, len(xs), 2)]
    return xs[0]


def _product_candidates(s0, s1, i0, i1):
    k = PEER_TOPK
    t = s0.shape[1]
    sub = lax.broadcasted_iota(jnp.int32, (SUBLANES, t), 0)
    blocks = []
    for half in range(k // SUBLANES):
        b = sub + half * SUBLANES
        lo = half * SUBLANES
        blocks.append((s0[0:1] + s1[lo:lo + SUBLANES], b, i0[0:1] * PEER_KEYS + i1[lo:lo + SUBLANES]))
    for a in range(1, SUBLANES):
        valid = (a + 1) * (sub + 1) <= k
        blocks.append((jnp.where(valid, s0[a:a + 1] + s1[0:SUBLANES], -jnp.inf), a * k + sub,
                       i0[a:a + 1] * PEER_KEYS + i1[0:SUBLANES]))
    a = sub + SUBLANES
    blocks.append((s0[SUBLANES:k] + s1[0:1], a * k, i0[SUBLANES:k] * PEER_KEYS + i1[0:1]))
    return blocks


def _peer_route_kernel(x_ref, shift_ref, scale_ref, gain_ref, wqh_ref, wql_ref, skh_ref, skl_ref,
                       rows_ref, hs_ref, gate_ref, xs_ref, q_ref, tv_ref, ti_ref, bs_ref, be_ref):
    tt = x_ref.shape[0]
    k = PEER_TOPK
    x = x_ref[...]
    xn = x * lax.rsqrt(jnp.mean(x * x, axis=-1, keepdims=True) + EPS) * gain_ref[...]
    xn = xn * (1.0 + scale_ref[0]) + shift_ref[0]
    x_hi, x_lo = _split_hi_lo(xn)
    xs_ref[:, :D_MODEL] = x_hi
    xs_ref[:, D_MODEL:] = x_lo
    q = _dot3(x_hi, x_lo, wqh_ref[...], wql_ref[...], ((1,), (0,)))
    for hp in range(2 * PEER_HEADS):
        q_ref[hp] = q[:, hp * PEER_HALF:(hp + 1) * PEER_HALF]

    def sub_topk(hp, carry):
        q_hi, q_lo = _split_hi_lo(q_ref[hp])
        p = hp % 2
        s = _dot3(skh_ref[p], skl_ref[p], q_hi, q_lo, ((1,), (1,)))
        tv_ref[hp], ti_ref[hp] = _top_rows(s, k)
        return carry

    lax.fori_loop(0, 2 * PEER_HEADS, sub_topk, 0)

    def head_topk(h, carry):
        blocks = _product_candidates(tv_ref[2 * h], tv_ref[2 * h + 1], ti_ref[2 * h], ti_ref[2 * h + 1])
        sums = [b[0] for b in blocks]
        best_s, best_e = [], []
        for _ in range(k):
            m = jnp.max(_tree(jnp.maximum, sums), axis=0, keepdims=True)
            pos = jnp.min(_tree(jnp.minimum, [jnp.where(c == m, b[1], k * k) for c, b in zip(sums, blocks)]),
                          axis=0, keepdims=True)
            hit = [b[1] == pos for b in blocks]
            e = jnp.max(_tree(jnp.maximum, [jnp.where(hh, b[2], -1) for hh, b in zip(hit, blocks)]),
                        axis=0, keepdims=True)
            sums = [jnp.where(hh, -jnp.inf, c) for hh, c in zip(hit, sums)]
            best_s.append(m)
            best_e.append(e)
        bs = jnp.concatenate(best_s, axis=0)
        ex = jnp.exp(bs - bs[0:1])
        off = pl.multiple_of(h * k, k)
        bs_ref[pl.ds(off, k), :] = ex / jnp.sum(ex, axis=0, keepdims=True)
        be_ref[pl.ds(off, k), :] = jnp.concatenate(best_e, axis=0).astype(F32)
        return carry

    lax.fori_loop(0, PEER_HEADS, head_topk, 0)
    gate_ref[...] = bs_ref[...].T
    ids = be_ref[...].T.astype(jnp.int32)
    rows_ref[...] = ids % HALF_EXPERTS
    hs_ref[...] = (ids // HALF_EXPERTS).astype(F32)


ROUTE_TOKEN_TILE = 256
ROUTE_VMEM_LIMIT = 40 * 1024 * 1024


def _peer_route(x2, shift, scale, gain, wq, subkeys, seq_len, *, token_tile=ROUTE_TOKEN_TILE, interpret=False):
    n = x2.shape[0]
    tt = token_tile
    assert seq_len % tt == 0 and n % seq_len == 0
    tiles_per_seq = seq_len // tt
    batch = n // seq_len
    wq_hi, wq_lo = _split_hi_lo(wq)
    sk_hi, sk_lo = _split_hi_lo(subkeys)
    tok = lambda i: (i, 0)
    per_seq = pl.BlockSpec((1, 1, D_MODEL), lambda i: (i // tiles_per_seq, 0, 0))
    whole = pl.BlockSpec(memory_space=pltpu.VMEM)
    slot_spec = pl.BlockSpec((tt, PEER_SLOTS), tok)
    slot_shape = lambda dt: jax.ShapeDtypeStruct((n, PEER_SLOTS), dt)
    return pl.pallas_call(
        _peer_route_kernel,
        grid=(n // tt,),
        in_specs=[pl.BlockSpec((tt, D_MODEL), tok), per_seq, per_seq,
                  pl.BlockSpec((1, D_MODEL), lambda i: (0, 0)), whole, whole, whole, whole],
        out_specs=[slot_spec, slot_spec, slot_spec, pl.BlockSpec((tt, 2 * D_MODEL), tok)],
        out_shape=[slot_shape(jnp.int32), slot_shape(F32), slot_shape(F32),
                   jax.ShapeDtypeStruct((n, 2 * D_MODEL), BF16)],
        scratch_shapes=[pltpu.VMEM((2 * PEER_HEADS, tt, PEER_HALF), F32),
                        pltpu.VMEM((2 * PEER_HEADS, PEER_TOPK, tt), F32),
                        pltpu.VMEM((2 * PEER_HEADS, PEER_TOPK, tt), jnp.int32),
                        pltpu.VMEM((PEER_SLOTS, tt), F32),
                        pltpu.VMEM((PEER_SLOTS, tt), F32)],
        compiler_params=pltpu.CompilerParams(dimension_semantics=("arbitrary",),
                                             vmem_limit_bytes=ROUTE_VMEM_LIMIT),
        interpret=interpret,
        name="peer_route",
    )(x2, shift.reshape(batch, 1, D_MODEL), scale.reshape(batch, 1, D_MODEL), gain.reshape(1, D_MODEL),
      wq_hi, wq_lo, sk_hi, sk_lo)


def _peer(x, gain, shift, scale, wq, subkeys, tab_u, tab_v):
    B, S, D = x.shape
    rows, hs, gate, xs = _peer_route(x.reshape(B * S, D), shift, scale, gain, wq, subkeys, S)
    y = _peer_experts(xs.reshape(B * S, SUBROWS, LANES), rows, hs, gate, tab_u, tab_v)
    return y.reshape(B, S, D)


def _mlstm_chunk_step(carry, inp):
    C, n, m = carry
    q, k, v, ig, lf = inp
    L = q.shape[2]
    causal = jnp.tril(jnp.ones((L, L), dtype=bool))
    b = jnp.cumsum(lf, axis=-1)
    d_log = jnp.where(causal, b[..., :, None] - b[..., None, :] + ig[..., None, :], -jnp.inf)
    inter = b + m[..., None]
    m_t = jnp.maximum(inter, jnp.max(d_log, axis=-1))
    w = jnp.exp(d_log - m_t[..., None])
    a_inter = jnp.exp(inter - m_t)
    s = jnp.einsum('bhtd,bhsd->bhts', q, k) * w
    num = a_inter[..., None] * jnp.einsum('bhtd,bhde->bhte', q, C) + jnp.einsum('bhts,bhse->bhte', s, v)
    den = a_inter * jnp.einsum('bhtd,bhd->bht', q, n) + jnp.sum(s, axis=-1)
    h = num / jnp.maximum(jnp.abs(den), jnp.exp(-m_t))[..., None]
    b_last = b[..., -1]
    g = b_last[..., None] - b + ig
    m_new = jnp.maximum(b_last + m, jnp.max(g, axis=-1))
    w_s = jnp.exp(g - m_new[..., None])
    decay = jnp.exp(b_last + m - m_new)
    C_new = decay[..., None, None] * C + jnp.einsum('bhs,bhsd,bhse->bhde', w_s, k, v)
    n_new = decay[..., None] * n + jnp.einsum('bhs,bhsd->bhd', w_s, k)
    return (C_new, n_new, m_new), h


def _mlstm(xn, w_in, b_i, b_f, h_gain, w_out):
    B, S, _ = xn.shape
    H, DQK, DV, L = MLSTM_HEADS, MLSTM_QK_DIM, MLSTM_V_DIM, MLSTM_CHUNK
    NC = S // L
    p = xn @ w_in
    o1 = MLSTM_QK_W
    o2 = o1 + MLSTM_QK_W
    o3 = o2 + MLSTM_V_W
    o4 = o3 + D_MODEL
    o5 = o4 + H
    q = p[..., :o1].reshape(B, S, H, DQK) * (DQK ** -0.5)
    k = p[..., o1:o2].reshape(B, S, H, DQK)
    v = p[..., o2:o3].reshape(B, S, H, DV)
    o_gate = jax.nn.sigmoid(p[..., o3:o4])
    ig = _softcap(p[..., o4:o5] + b_i)
    lf = jax.nn.log_sigmoid(_softcap(p[..., o5:] + b_f))

    def to_chunks(t):
        return t.reshape(B, NC, L, H, t.shape[-1]).transpose(1, 0, 3, 2, 4)

    def gate_chunks(t):
        return t.reshape(B, NC, L, H).transpose(1, 0, 3, 2)

    init = (jnp.zeros((B, H, DQK, DV), F32), jnp.zeros((B, H, DQK), F32), jnp.zeros((B, H), F32))
    _, h = lax.scan(_mlstm_chunk_step, init,
                    (to_chunks(q), to_chunks(k), to_chunks(v), gate_chunks(ig), gate_chunks(lf)))
    h = h.transpose(1, 0, 3, 2, 4).reshape(B, S, H, DV)
    h = _head_rms(h, h_gain.reshape(H, DV)).reshape(B, S, D_MODEL)
    return (o_gate * h) @ w_out


def _shared_kv(x, cs, ada_w, ada_b, norm_g, w_kv, b_f, k_gain):
    B, S, _ = x.shape
    mod = cs @ ada_w + ada_b
    sh, sc = jnp.split(mod, 2, axis=-1)
    xn = _ada_rmsnorm(x, norm_g, sh, sc)
    p = xn @ w_kv
    k = _head_rms(p[..., :D_MODEL].reshape(B, S, FOX_HEADS, FOX_HEAD_DIM), k_gain)
    v = p[..., D_MODEL:2 * D_MODEL].reshape(B, S, FOX_HEADS, FOX_HEAD_DIM)
    log_f = jax.nn.log_sigmoid(p[..., 2 * D_MODEL:] + b_f)
    f_cum = jnp.cumsum(log_f, axis=1).transpose(0, 2, 1)
    return k.transpose(0, 2, 1, 3), v.transpose(0, 2, 1, 3), f_cum


def _fox(xn, w_qo, q_gain, w_out, k, v, f_cum):
    B, S, _ = xn.shape
    p = xn @ w_qo
    q = _head_rms(p[..., :D_MODEL].reshape(B, S, FOX_HEADS, FOX_HEAD_DIM), q_gain).transpose(0, 2, 1, 3)
    o_gate = jax.nn.sigmoid(p[..., D_MODEL:])
    scale = FOX_HEAD_DIM ** -0.5
    outs = []
    for blk in range(S // Q_BLOCK):
        lo, hi = blk * Q_BLOCK, (blk + 1) * Q_BLOCK
        logits = jnp.einsum('bhqd,bhkd->bhqk', q[:, :, lo:hi], k[:, :, :hi]) * scale
        logits = logits + f_cum[:, :, lo:hi, None] - f_cum[:, :, None, :hi]
        mask = jnp.arange(hi)[None, :] <= jnp.arange(lo, hi)[:, None]
        probs = jax.nn.softmax(jnp.where(mask, logits, -jnp.inf), axis=-1)
        outs.append(jnp.einsum('bhqk,bhkd->bhqd', probs, v[:, :, :hi]))
    att = jnp.concatenate(outs, axis=2).transpose(0, 2, 1, 3).reshape(B, S, D_MODEL)
    return (o_gate * att) @ w_out


def kernel(x, c, ada_w, ada_b, mix_norm, ffn_norm, a_w_in, a_b_i, a_b_f, a_h_norm, a_w_out,
           kv_ada_w, kv_ada_b, kv_norm, kv_w, kv_b_f, kv_k_norm, b_w_qo, b_q_norm, b_w_out,
           peer_wq, peer_subkeys, peer_u, peer_v):
    cs = jax.nn.silu(c)
    k_sh = v_sh = f_cum = None
    for l in range(DEPTH):
        mod = cs @ ada_w[l] + ada_b[l]
        sh1, sc1, g1, sh2, sc2, g2 = jnp.split(mod, N_ADA, axis=-1)
        xn = _ada_rmsnorm(x, mix_norm[l], sh1, sc1)
        if l < N_A_LAYERS:
            y = _mlstm(xn, a_w_in[l], a_b_i[l], a_b_f[l], a_h_norm[l], a_w_out[l])
        else:
            j = l - N_A_LAYERS
            y = _fox(xn, b_w_qo[j], b_q_norm[j], b_w_out[j], k_sh, v_sh, f_cum)
        x = x + g1[:, None, :] * y
        y = _peer(x, ffn_norm[l], sh2, sc2, peer_wq[l], peer_subkeys[l],
                  _pack_expert_table(peer_u[l]), _pack_expert_table(peer_v[l]))
        x = x + g2[:, None, :] * y
        if l == N_A_LAYERS - 1:
            k_sh, v_sh, f_cum = _shared_kv(x, cs, kv_ada_w, kv_ada_b, kv_norm, kv_w, kv_b_f, kv_k_norm)
    return x
```

```python
import functools
import math

import jax
import jax.numpy as jnp
from jax import lax
from jax.experimental import pallas as pl
from jax.experimental.pallas import tpu as pltpu

F32 = jnp.float32
BF16 = jnp.bfloat16

D_MODEL = 1024
DEPTH = 2
N_A_LAYERS = DEPTH // 2
EPS = 1e-6
N_ADA = 6

MLSTM_HEADS = 4
MLSTM_QK_DIM = D_MODEL // (2 * MLSTM_HEADS)
MLSTM_V_DIM = D_MODEL // MLSTM_HEADS
MLSTM_CHUNK = 64
GATE_SOFTCAP = 15.0
MLSTM_QK_W = MLSTM_HEADS * MLSTM_QK_DIM
MLSTM_V_W = MLSTM_HEADS * MLSTM_V_DIM

FOX_HEADS = 8
FOX_HEAD_DIM = D_MODEL // FOX_HEADS
Q_BLOCK = 128

PEER_HEADS = 8
PEER_KEYS = 128
PEER_EXPERTS = PEER_KEYS * PEER_KEYS
PEER_QUERY_DIM = 256
PEER_HALF = PEER_QUERY_DIM // 2
PEER_TOPK = 16

SUBLANES = 8
LANES = 128
ROW_WORDS = SUBLANES * LANES
assert ROW_WORDS == D_MODEL
PEER_SLOTS = PEER_HEADS * PEER_TOPK
HALF_EXPERTS = PEER_EXPERTS // 2
SUBROWS = 2 * SUBLANES
GATHER_ROWS = PEER_SLOTS * SUBROWS
TOKEN_GROUP = SUBLANES
PEER_TOKEN_TILE = 64
PEER_VMEM_LIMIT = 48 * 1024 * 1024


def _pack_expert_table(t):
    lo = lax.bitcast_convert_type(t[:HALF_EXPERTS].astype(BF16), jnp.uint16).astype(jnp.uint32)
    hi = lax.bitcast_convert_type(t[HALF_EXPERTS:].astype(BF16), jnp.uint16).astype(jnp.uint32)
    w = lo | (hi << 16)
    return lax.bitcast_convert_type(w, jnp.int32).reshape(HALF_EXPERTS, SUBLANES, LANES)


def _slot_expand_matrix():
    slot = lax.broadcasted_iota(jnp.int32, (PEER_SLOTS, GATHER_ROWS), 0)
    sub = lax.broadcasted_iota(jnp.int32, (PEER_SLOTS, GATHER_ROWS), 1)
    return (sub // SUBROWS == slot).astype(BF16)


def _diag_mask():
    r = lax.broadcasted_iota(jnp.int32, (SUBLANES, GATHER_ROWS), 0)
    sub = lax.broadcasted_iota(jnp.int32, (SUBLANES, GATHER_ROWS), 1)
    return (sub % SUBROWS) // 2 == r


def _half_mask(hs, e):
    hsx = jnp.dot(hs.astype(BF16), e, preferred_element_type=F32)
    par = (lax.broadcasted_iota(jnp.int32, hsx.shape, 1) % 2).astype(F32)
    return hsx == par


def _gather_token(idx_ref, tab_ref, gb_ref, t):
    for s in range(PEER_SLOTS):
        gb_ref[pl.ds(s * SUBLANES, SUBLANES), :] = tab_ref[idx_ref[t, s]]
    return pltpu.bitcast(gb_ref[...], BF16)


def _split_hi_lo(a):
    hi = a.astype(BF16)
    lo = (a - hi.astype(F32)).astype(BF16)
    return hi, lo


def _peer_down_kernel(idx_ref, xs_ref, hs_ref, gate_ref, tab_ref, e_ref, sel_ref, w_ref, gb_ref, z_ref):
    tt = w_ref.shape[0]
    diag = _diag_mask()

    def group(g, carry):
        rows = []
        for i in range(TOKEN_GROUP):
            t = g * TOKEN_GROUP + i
            gath = _gather_token(idx_ref, tab_ref, gb_ref, t)
            y = lax.dot_general(xs_ref[t], gath, (((1,), (1,)), ((), ())),
                                preferred_element_type=F32)
            y8 = y[:SUBLANES] + y[SUBLANES:]
            rows.append(jnp.sum(jnp.where(diag, y8, 0.0), axis=0, keepdims=True))
        z_ref[pl.ds(pl.multiple_of(g * TOKEN_GROUP, TOKEN_GROUP), TOKEN_GROUP), :] = jnp.concatenate(rows, axis=0)
        return carry

    lax.fori_loop(0, tt // TOKEN_GROUP, group, 0)
    zm = jnp.where(_half_mask(hs_ref[...], e_ref[...]), z_ref[...], 0.0)
    act = jnp.dot(zm, sel_ref[...], precision=lax.Precision.HIGHEST, preferred_element_type=F32)
    gelu = 0.5 * act * (1.0 + lax.erf(act * (1.0 / math.sqrt(2.0))))
    w_ref[...] = gate_ref[...] * gelu


def _peer_up_kernel(idx_ref, w_ref, hs_ref, tab_ref, e_ref, y_ref, gb_ref, ahi_ref, alo_ref):
    tt = w_ref.shape[0]
    diag = _diag_mask()
    e = e_ref[...]
    hm = _half_mask(hs_ref[...], e)
    w_hi, w_lo = _split_hi_lo(w_ref[...])
    ahi_ref[...] = jnp.where(hm, jnp.dot(w_hi, e, preferred_element_type=F32), 0.0)
    alo_ref[...] = jnp.where(hm, jnp.dot(w_lo, e, preferred_element_type=F32), 0.0)

    def group(g, carry):
        base = pl.multiple_of(g * TOKEN_GROUP, TOKEN_GROUP)
        a_hi = ahi_ref[pl.ds(base, TOKEN_GROUP), :]
        a_lo = alo_ref[pl.ds(base, TOKEN_GROUP), :]
        for i in range(TOKEN_GROUP):
            t = g * TOKEN_GROUP + i
            gath = _gather_token(idx_ref, tab_ref, gb_ref, t)
            lhs = jnp.concatenate(
                [jnp.where(diag, a_hi[i:i + 1, :], 0.0), jnp.where(diag, a_lo[i:i + 1, :], 0.0)],
                axis=0).astype(BF16)
            out = jnp.dot(lhs, gath, preferred_element_type=F32)
            y_ref[t] = out[:SUBLANES] + out[SUBLANES:]
        return carry

    lax.fori_loop(0, tt // TOKEN_GROUP, group, 0)


def _peer_experts(xs, rows, hs, gate, tab_u, tab_v, *, token_tile=PEER_TOKEN_TILE, interpret=False):
    n = xs.shape[0]
    tt = token_tile
    assert n % tt == 0 and tt % TOKEN_GROUP == 0
    e = _slot_expand_matrix()
    sel = e.T.astype(F32)

    tok = lambda i: (i, 0)
    const2 = lambda i: (0, 0)
    smem_idx = pl.BlockSpec((tt, PEER_SLOTS), tok, memory_space=pltpu.SMEM)
    slot_spec = pl.BlockSpec((tt, PEER_SLOTS), tok)
    table_spec = pl.BlockSpec(memory_space=pltpu.VMEM)
    params = pltpu.CompilerParams(dimension_semantics=("arbitrary",), vmem_limit_bytes=PEER_VMEM_LIMIT)

    w = pl.pallas_call(
        _peer_down_kernel,
        grid=(n // tt,),
        in_specs=[smem_idx,
                  pl.BlockSpec((tt, SUBROWS, LANES), lambda i: (i, 0, 0)),
                  slot_spec, slot_spec, table_spec,
                  pl.BlockSpec((PEER_SLOTS, GATHER_ROWS), const2),
                  pl.BlockSpec((GATHER_ROWS, PEER_SLOTS), const2)],
        out_specs=slot_spec,
        out_shape=jax.ShapeDtypeStruct((n, PEER_SLOTS), F32),
        scratch_shapes=[pltpu.VMEM((PEER_SLOTS * SUBLANES, LANES), jnp.int32),
                        pltpu.VMEM((tt, GATHER_ROWS), F32)],
        compiler_params=params,
        interpret=interpret,
        name="peer_down",
    )(rows, xs, hs, gate, tab_u, e, sel)

    y = pl.pallas_call(
        _peer_up_kernel,
        grid=(n // tt,),
        in_specs=[smem_idx, slot_spec, slot_spec, table_spec,
                  pl.BlockSpec((PEER_SLOTS, GATHER_ROWS), const2)],
        out_specs=pl.BlockSpec((tt, SUBLANES, LANES), lambda i: (i, 0, 0)),
        out_shape=jax.ShapeDtypeStruct((n, SUBLANES, LANES), F32),
        scratch_shapes=[pltpu.VMEM((PEER_SLOTS * SUBLANES, LANES), jnp.int32),
                        pltpu.VMEM((tt, GATHER_ROWS), F32),
                        pltpu.VMEM((tt, GATHER_ROWS), F32)],
        compiler_params=params,
        interpret=interpret,
        name="peer_up",
    )(rows, w, hs, tab_v, e)
    return y.reshape(n, D_MODEL)


def _ada_rmsnorm(x, gain, shift, scale):
    xf = x * lax.rsqrt(jnp.mean(x * x, axis=-1, keepdims=True) + EPS) * gain
    return xf * (1.0 + scale[:, None, :]) + shift[:, None, :]


def _head_rms(t, gain):
    return t * lax.rsqrt(jnp.mean(t * t, axis=-1, keepdims=True) + EPS) * gain


def _softcap(z):
    return GATE_SOFTCAP * jnp.tanh(z / GATE_SOFTCAP)


def _dot3(a_hi, a_lo, b_hi, b_lo, dims):
    dot = functools.partial(lax.dot_general, dimension_numbers=(dims, ((), ())), preferred_element_type=F32)
    return dot(a_hi, b_hi) + (dot(a_hi, b_lo) + dot(a_lo, b_hi))


def _top_rows(s, k):
    nrows = s.shape[0]
    row = lax.broadcasted_iota(jnp.int32, s.shape, 0)
    vals, rows = [], []
    for _ in range(k):
        m = jnp.max(s, axis=0, keepdims=True)
        r = jnp.min(jnp.where(s == m, row, nrows), axis=0, keepdims=True)
        vals.append(m)
        rows.append(r)
        s = jnp.where(row == r, -jnp.inf, s)
    return jnp.concatenate(vals, axis=0), jnp.concatenate(rows, axis=0)


def _tree(op, xs):
    while len(xs) > 1:
        xs = [op(xs[i], xs[i + 1]) if i + 1 < len(xs) else xs[i] for i in range(0, len(xs), 2)]
    return xs[0]


def _product_candidates(s0, s1, i0, i1):
    k = PEER_TOPK
    t = s0.shape[1]
    sub = lax.broadcasted_iota(jnp.int32, (SUBLANES, t), 0)
    blocks = []
    for half in range(k // SUBLANES):
        b = sub + half * SUBLANES
        lo = half * SUBLANES
        blocks.append((s0[0:1] + s1[lo:lo + SUBLANES], b, i0[0:1] * PEER_KEYS + i1[lo:lo + SUBLANES]))
    for a in range(1, SUBLANES):
        valid = (a + 1) * (sub + 1) <= k
        blocks.append((jnp.where(valid, s0[a:a + 1] + s1[0:SUBLANES], -jnp.inf), a * k + sub,
                       i0[a:a + 1] * PEER_KEYS + i1[0:SUBLANES]))
    a = sub + SUBLANES
    blocks.append((s0[SUBLANES:k] + s1[0:1], a * k, i0[SUBLANES:k] * PEER_KEYS + i1[0:1]))
    return blocks


def _peer_route_kernel(x_ref, shift_ref, scale_ref, gain_ref, wqh_ref, wql_ref, skh_ref, skl_ref,
                       rows_ref, hs_ref, gate_ref, xs_ref, q_ref, tv_ref, ti_ref, bs_ref, be_ref):
    tt = x_ref.shape[0]
    k = PEER_TOPK
    x = x_ref[...]
    xn = x * lax.rsqrt(jnp.mean(x * x, axis=-1, keepdims=True) + EPS) * gain_ref[...]
    xn = xn * (1.0 + scale_ref[0]) + shift_ref[0]
    x_hi, x_lo = _split_hi_lo(xn)
    xs_ref[:, :D_MODEL] = x_hi
    xs_ref[:, D_MODEL:] = x_lo
    q = _dot3(x_hi, x_lo, wqh_ref[...], wql_ref[...], ((1,), (0,)))
    for hp in range(2 * PEER_HEADS):
        q_ref[hp] = q[:, hp * PEER_HALF:(hp + 1) * PEER_HALF]

    def sub_topk(hp, carry):
        q_hi, q_lo = _split_hi_lo(q_ref[hp])
        p = hp % 2
        s = _dot3(skh_ref[p], skl_ref[p], q_hi, q_lo, ((1,), (1,)))
        tv_ref[hp], ti_ref[hp] = _top_rows(s, k)
        return carry

    lax.fori_loop(0, 2 * PEER_HEADS, sub_topk, 0)

    def head_topk(h, carry):
        blocks = _product_candidates(tv_ref[2 * h], tv_ref[2 * h + 1], ti_ref[2 * h], ti_ref[2 * h + 1])
        sums = [b[0] for b in blocks]
        best_s, best_e = [], []
        for _ in range(k):
            m = jnp.max(_tree(jnp.maximum, sums), axis=0, keepdims=True)
            pos = jnp.min(_tree(jnp.minimum, [jnp.where(c == m, b[1], k * k) for c, b in zip(sums, blocks)]),
                          axis=0, keepdims=True)
            hit = [b[1] == pos for b in blocks]
            e = jnp.max(_tree(jnp.maximum, [jnp.where(hh, b[2], -1) for hh, b in zip(hit, blocks)]),
                        axis=0, keepdims=True)
            sums = [jnp.where(hh, -jnp.inf, c) for hh, c in zip(hit, sums)]
            best_s.append(m)
            best_e.append(e)
        bs = jnp.concatenate(best_s, axis=0)
        ex = jnp.exp(bs - bs[0:1])
        off = pl.multiple_of(h * k, k)
        bs_ref[pl.ds(off, k), :] = ex / jnp.sum(ex, axis=0, keepdims=True)
        be_ref[pl.ds(off, k), :] = jnp.concatenate(best_e, axis=0).astype(F32)
        return carry

    lax.fori_loop(0, PEER_HEADS, head_topk, 0)
    gate_ref[...] = bs_ref[...].T
    ids = be_ref[...].T.astype(jnp.int32)
    rows_ref[...] = ids % HALF_EXPERTS
    hs_ref[...] = (ids // HALF_EXPERTS).astype(F32)


ROUTE_TOKEN_TILE = 256
ROUTE_VMEM_LIMIT = 40 * 1024 * 1024


def _peer_route(x2, shift, scale, gain, wq, subkeys, seq_len, *, token_tile=ROUTE_TOKEN_TILE, interpret=False):
    n = x2.shape[0]
    tt = token_tile
    assert seq_len % tt == 0 and n % seq_len == 0
    tiles_per_seq = seq_len // tt
    batch = n // seq_len
    wq_hi, wq_lo = _split_hi_lo(wq)
    sk_hi, sk_lo = _split_hi_lo(subkeys)
    tok = lambda i: (i, 0)
    per_seq = pl.BlockSpec((1, 1, D_MODEL), lambda i: (i // tiles_per_seq, 0, 0))
    whole = pl.BlockSpec(memory_space=pltpu.VMEM)
    slot_spec = pl.BlockSpec((tt, PEER_SLOTS), tok)
    slot_shape = lambda dt: jax.ShapeDtypeStruct((n, PEER_SLOTS), dt)
    return pl.pallas_call(
        _peer_route_kernel,
        grid=(n // tt,),
        in_specs=[pl.BlockSpec((tt, D_MODEL), tok), per_seq, per_seq,
                  pl.BlockSpec((1, D_MODEL), lambda i: (0, 0)), whole, whole, whole, whole],
        out_specs=[slot_spec, slot_spec, slot_spec, pl.BlockSpec((tt, 2 * D_MODEL), tok)],
        out_shape=[slot_shape(jnp.int32), slot_shape(F32), slot_shape(F32),
                   jax.ShapeDtypeStruct((n, 2 * D_MODEL), BF16)],
        scratch_shapes=[pltpu.VMEM((2 * PEER_HEADS, tt, PEER_HALF), F32),
                        pltpu.VMEM((2 * PEER_HEADS, PEER_TOPK, tt), F32),
                        pltpu.VMEM((2 * PEER_HEADS, PEER_TOPK, tt), jnp.int32),
                        pltpu.VMEM((PEER_SLOTS, tt), F32),
                        pltpu.VMEM((PEER_SLOTS, tt), F32)],
        compiler_params=pltpu.CompilerParams(dimension_semantics=("arbitrary",),
                                             vmem_limit_bytes=ROUTE_VMEM_LIMIT),
        interpret=interpret,
        name="peer_route",
    )(x2, shift.reshape(batch, 1, D_MODEL), scale.reshape(batch, 1, D_MODEL), gain.reshape(1, D_MODEL),
      wq_hi, wq_lo, sk_hi, sk_lo)


def _peer(x, gain, shift, scale, wq, subkeys, tab_u, tab_v):
    B, S, D = x.shape
    rows, hs, gate, xs = _peer_route(x.reshape(B * S, D), shift, scale, gain, wq, subkeys, S)
    y = _peer_experts(xs.reshape(B * S, SUBROWS, LANES), rows, hs, gate, tab_u, tab_v)
    return y.reshape(B, S, D)


PROJ_TOKEN_TILE = 512
PROJ_VMEM_LIMIT = 48 * 1024 * 1024


def _ada_norm_tile(x, gain, shift, scale):
    xn = x * lax.rsqrt(jnp.mean(x * x, axis=-1, keepdims=True) + EPS) * gain
    return xn * (1.0 + scale) + shift


def _norm_proj_kernel(x_ref, shift_ref, scale_ref, gain_ref, w_ref, *rest, with_gate):
    x_hi, x_lo = _split_hi_lo(_ada_norm_tile(x_ref[...], gain_ref[...], shift_ref[0], scale_ref[0]))
    if with_gate:
        wgh_ref, wgl_ref, main_ref, gate_ref = rest
        gate_ref[...] = _dot3(x_hi, x_lo, wgh_ref[...], wgl_ref[...], ((1,), (0,)))
    else:
        (main_ref,) = rest
    main_ref[...] = jnp.dot(x_hi, w_ref[...], preferred_element_type=F32)


def _norm_proj(x2, shift, scale, gain, w_main, w_gate, seq_len, *, token_tile=PROJ_TOKEN_TILE, interpret=False):
    n = x2.shape[0]
    tt = token_tile
    assert seq_len % tt == 0 and n % seq_len == 0
    tiles_per_seq = seq_len // tt
    batch = n // seq_len
    m = w_main.shape[1]
    tok = lambda i: (i, 0)
    per_seq = pl.BlockSpec((1, 1, D_MODEL), lambda i: (i // tiles_per_seq, 0, 0))
    whole = pl.BlockSpec(memory_space=pltpu.VMEM)
    in_specs = [pl.BlockSpec((tt, D_MODEL), tok), per_seq, per_seq,
                pl.BlockSpec((1, D_MODEL), lambda i: (0, 0)), whole]
    args = [x2, shift.reshape(batch, 1, D_MODEL), scale.reshape(batch, 1, D_MODEL), gain.reshape(1, D_MODEL),
            w_main.astype(BF16)]
    out_specs = [pl.BlockSpec((tt, m), tok)]
    out_shape = [jax.ShapeDtypeStruct((n, m), F32)]
    if w_gate is not None:
        wg_hi, wg_lo = _split_hi_lo(jnp.pad(w_gate, ((0, 0), (0, LANES - w_gate.shape[1]))))
        in_specs += [whole, whole]
        args += [wg_hi, wg_lo]
        out_specs.append(pl.BlockSpec((tt, LANES), tok))
        out_shape.append(jax.ShapeDtypeStruct((n, LANES), F32))
    outs = pl.pallas_call(
        functools.partial(_norm_proj_kernel, with_gate=w_gate is not None),
        grid=(n // tt,), in_specs=in_specs, out_specs=out_specs, out_shape=out_shape,
        compiler_params=pltpu.CompilerParams(dimension_semantics=("arbitrary",),
                                             vmem_limit_bytes=PROJ_VMEM_LIMIT),
        interpret=interpret, name="norm_proj",
    )(*args)
    return (outs[0], outs[1]) if w_gate is not None else (outs[0], None)


def _out_proj_kernel(h_ref, og_ref, x_ref, g_ref, hg_ref, w_ref, o_ref, *, head_dim):
    h = h_ref[...]
    if head_dim is not None:
        parts = []
        for j in range(D_MODEL // head_dim):
            hb = h[:, j * head_dim:(j + 1) * head_dim]
            parts.append(hb * lax.rsqrt(jnp.mean(hb * hb, axis=-1, keepdims=True) + EPS))
        h = jnp.concatenate(parts, axis=1) * hg_ref[...]
    a = jax.nn.sigmoid(og_ref[...]) * h
    y = jnp.dot(a.astype(BF16), w_ref[...], preferred_element_type=F32)
    o_ref[...] = x_ref[...] + g_ref[0] * y


def _out_proj(h2, p, og_block, x2, g, h_gain, w_out, seq_len, head_dim, *, token_tile=PROJ_TOKEN_TILE,
              interpret=False):
    n = x2.shape[0]
    tt = token_tile
    tiles_per_seq = seq_len // tt
    batch = n // seq_len
    tok = lambda i: (i, 0)
    row = pl.BlockSpec((tt, D_MODEL), tok)
    gain = jnp.ones((1, D_MODEL), F32) if h_gain is None else h_gain.reshape(1, D_MODEL)
    return pl.pallas_call(
        functools.partial(_out_proj_kernel, head_dim=head_dim),
        grid=(n // tt,),
        in_specs=[row, pl.BlockSpec((tt, D_MODEL), lambda i: (i, og_block)), row,
                  pl.BlockSpec((1, 1, D_MODEL), lambda i: (i // tiles_per_seq, 0, 0)),
                  pl.BlockSpec((1, D_MODEL), lambda i: (0, 0)),
                  pl.BlockSpec(memory_space=pltpu.VMEM)],
        out_specs=row,
        out_shape=jax.ShapeDtypeStruct((n, D_MODEL), F32),
        compiler_params=pltpu.CompilerParams(dimension_semantics=("arbitrary",),
                                             vmem_limit_bytes=PROJ_VMEM_LIMIT),
        interpret=interpret, name="out_proj",
    )(h2, p, x2, g.reshape(batch, 1, D_MODEL), gain, w_out.astype(BF16))


MLSTM_STATE_W = MLSTM_V_DIM + LANES


def _mlstm_kernel(q_ref, k_ref, v_ref, gc_ref, gr_ref, h_ref, cn_ref, m_ref):
    H, DQK, DV, L = MLSTM_HEADS, MLSTM_QK_DIM, MLSTM_V_DIM, MLSTM_CHUNK

    @pl.when(pl.program_id(1) == 0)
    def _():
        cn_ref[...] = jnp.zeros_like(cn_ref)
        m_ref[...] = jnp.zeros_like(m_ref)

    tril = lax.broadcasted_iota(jnp.int32, (L, L), 1) <= lax.broadcasted_iota(jnp.int32, (L, L), 0)
    gc = gc_ref[...]
    gr = gr_ref[0]
    ones = jnp.ones((L, LANES), F32)
    for h in range(H):
        ig_col, b_col = gc[:, h:h + 1], gc[:, H + h:H + h + 1]
        ig_row, b_row = gr[h:h + 1, :], gr[H + h:H + h + 1, :]
        m_prev = m_ref[h:h + 1, 0:1]
        d_log = jnp.where(tril, b_col - b_row + ig_row, -jnp.inf)
        inter = b_col + m_prev
        m_t = jnp.maximum(inter, jnp.max(d_log, axis=1, keepdims=True))
        w = jnp.exp(d_log - m_t)
        a_inter = jnp.exp(inter - m_t)
        qh = (q_ref[:, h * DQK:(h + 1) * DQK] * (DQK ** -0.5)).astype(BF16)
        kf = k_ref[:, h * DQK:(h + 1) * DQK]
        vh = v_ref[:, h * DV:(h + 1) * DV]
        s = lax.dot_general(qh, kf.astype(BF16), (((1,), (1,)), ((), ())), preferred_element_type=F32) * w
        cn = cn_ref[h]
        qc = jnp.dot(qh, cn.astype(BF16), preferred_element_type=F32)
        num = a_inter * qc[:, :DV] + jnp.dot(s.astype(BF16), vh.astype(BF16), preferred_element_type=F32)
        den = a_inter * qc[:, DV:DV + 1] + jnp.sum(s, axis=1, keepdims=True)
        h_ref[:, h * DV:(h + 1) * DV] = num / jnp.maximum(jnp.abs(den), jnp.exp(-m_t))
        b_last = b_col[L - 1:L, :]
        g_col = b_last - b_col + ig_col
        m_new = jnp.maximum(b_last + m_prev, jnp.max(g_col, axis=0, keepdims=True))
        w_s = jnp.exp(g_col - m_new)
        decay = jnp.exp(b_last + m_prev - m_new)
        kw = (kf * w_s).astype(BF16)
        vaug = jnp.concatenate([vh, ones], axis=1).astype(BF16)
        cn_ref[h] = decay * cn + lax.dot_general(kw, vaug, (((0,), (0,)), ((), ())), preferred_element_type=F32)
        m_ref[h:h + 1, :] = jnp.broadcast_to(m_new, (1, LANES))


def _mlstm_scan(p, gates, b_i, b_f, batch, seq_len, *, interpret=False):
    H, L = MLSTM_HEADS, MLSTM_CHUNK
    n = p.shape[0]
    nc = seq_len // L
    ig = _softcap(gates[:, :H] + b_i)
    lf = jax.nn.log_sigmoid(_softcap(gates[:, H:2 * H] + b_f))
    b = jnp.cumsum(lf.reshape(n // L, L, H), axis=1).reshape(n, H)
    gc = jnp.concatenate([ig, b], axis=1)
    gr = gc.reshape(n // L, L, 2 * H).transpose(0, 2, 1)
    chunk = lambda bi, c: bi * nc + c
    return pl.pallas_call(
        _mlstm_kernel,
        grid=(batch, nc),
        in_specs=[pl.BlockSpec((L, MLSTM_QK_W), lambda bi, c: (chunk(bi, c), 0)),
                  pl.BlockSpec((L, MLSTM_QK_W), lambda bi, c: (chunk(bi, c), 1)),
                  pl.BlockSpec((L, MLSTM_V_W), lambda bi, c: (chunk(bi, c), 1)),
                  pl.BlockSpec((L, 2 * H), lambda bi, c: (chunk(bi, c), 0)),
                  pl.BlockSpec((1, 2 * H, L), lambda bi, c: (chunk(bi, c), 0, 0))],
        out_specs=pl.BlockSpec((L, D_MODEL), lambda bi, c: (chunk(bi, c), 0)),
        out_shape=jax.ShapeDtypeStruct((n, D_MODEL), F32),
        scratch_shapes=[pltpu.VMEM((H, MLSTM_QK_DIM, MLSTM_STATE_W), F32),
                        pltpu.VMEM((SUBLANES, LANES), F32)],
        compiler_params=pltpu.CompilerParams(dimension_semantics=("arbitrary", "arbitrary")),
        interpret=interpret, name="mlstm_scan",
    )(p, p, p, gc, gr)


FOX_BLOCK = 512
FOX_VMEM_LIMIT = 48 * 1024 * 1024


def _bias_columns(f, query_side):
    hi = f.astype(BF16).astype(F32)
    r1 = f - hi
    lo = r1.astype(BF16).astype(F32)
    lo2 = r1 - lo
    lane = lax.broadcasted_iota(jnp.int32, (f.shape[0], LANES), 1)
    if query_side:
        vals = jnp.where(lane == 0, hi, jnp.where(lane == 1, lo, jnp.where(lane == 2, lo2,
                         jnp.where(lane < 6, 1.0, 0.0))))
    else:
        vals = jnp.where(lane < 3, 1.0, jnp.where(lane == 3, -hi, jnp.where(lane == 4, -lo,
                         jnp.where(lane == 5, -lo2, 0.0))))
    return vals.astype(BF16)


def _head_column(fc, h):
    lane = lax.broadcasted_iota(jnp.int32, fc.shape, 1)
    return jnp.sum(jnp.where(lane == h, fc, 0.0), axis=1, keepdims=True)


def _rms_rows(t, gain):
    return t * lax.rsqrt(jnp.mean(t * t, axis=-1, keepdims=True) + EPS) * gain


def _fox_attn_kernel(q_ref, k_ref, v_ref, fq_ref, fk_ref, qg_ref, kg_ref, o_ref,
                     ka_ref, vb_ref, m_ref, l_ref, acc_ref):
    i = pl.program_id(1)
    h = pl.program_id(0) % FOX_HEADS
    tq = q_ref.shape[0]
    nk = k_ref.shape[0] // tq
    hd = FOX_HEAD_DIM

    @pl.when(i == 0)
    def _():
        def prep(c, carry):
            r = pl.ds(pl.multiple_of(c * tq, tq), tq)
            ka_ref[r, :hd] = _rms_rows(k_ref[r, :], kg_ref[...]).astype(BF16)
            ka_ref[r, hd:] = _bias_columns(_head_column(fk_ref[r, :], h), False)
            vb_ref[r, :] = v_ref[r, :].astype(BF16)
            return carry
        lax.fori_loop(0, nk, prep, 0)

    qn = _rms_rows(q_ref[...], qg_ref[...]) * (hd ** -0.5)
    qa = jnp.concatenate([qn.astype(BF16), _bias_columns(_head_column(fq_ref[...], h), True)], axis=1)
    m_ref[...] = jnp.full_like(m_ref, -jnp.inf)
    l_ref[...] = jnp.zeros_like(l_ref)
    acc_ref[...] = jnp.zeros_like(acc_ref)

    def step(j, masked):
        r = pl.ds(pl.multiple_of(j * tq, tq), tq)
        s = lax.dot_general(qa, ka_ref[r, :], (((1,), (1,)), ((), ())), preferred_element_type=F32)
        if masked:
            causal = lax.broadcasted_iota(jnp.int32, s.shape, 1) <= lax.broadcasted_iota(jnp.int32, s.shape, 0)
            s = jnp.where(causal, s, -jnp.inf)
        m_prev = m_ref[...]
        m_new = jnp.maximum(m_prev, jnp.max(s, axis=1, keepdims=True))
        alpha = jnp.exp(m_prev - m_new)
        p = jnp.exp(s - m_new)
        l_ref[...] = alpha * l_ref[...] + jnp.sum(p, axis=1, keepdims=True)
        acc_ref[...] = alpha * acc_ref[...] + jnp.dot(p.astype(BF16), vb_ref[r, :], preferred_element_type=F32)
        m_ref[...] = m_new

    def body(j, carry):
        step(j, False)
        return carry

    lax.fori_loop(0, i, body, 0)
    step(i, True)
    o_ref[...] = acc_ref[...] / l_ref[...]


def _fox_attention(pq, kv, f_cum, q_gain, k_gain, batch, seq_len, *, block=FOX_BLOCK, interpret=False):
    n = pq.shape[0]
    hd, nh = FOX_HEAD_DIM, FOX_HEADS
    nq = seq_len // block
    return pl.pallas_call(
        _fox_attn_kernel,
        grid=(batch * nh, nq),
        in_specs=[pl.BlockSpec((block, hd), lambda bh, i: ((bh // nh) * nq + i, bh % nh)),
                  pl.BlockSpec((seq_len, hd), lambda bh, i: (bh // nh, bh % nh)),
                  pl.BlockSpec((seq_len, hd), lambda bh, i: (bh // nh, nh + bh % nh)),
                  pl.BlockSpec((block, nh), lambda bh, i: ((bh // nh) * nq + i, 0)),
                  pl.BlockSpec((seq_len, nh), lambda bh, i: (bh // nh, 0)),
                  pl.BlockSpec((1, hd), lambda bh, i: (0, 0)),
                  pl.BlockSpec((1, hd), lambda bh, i: (0, 0))],
        out_specs=pl.BlockSpec((block, hd), lambda bh, i: ((bh // nh) * nq + i, bh % nh)),
        out_shape=jax.ShapeDtypeStruct((n, D_MODEL), F32),
        scratch_shapes=[pltpu.VMEM((seq_len, 2 * hd), BF16),
                        pltpu.VMEM((seq_len, hd), BF16),
                        pltpu.VMEM((block, 1), F32),
                        pltpu.VMEM((block, 1), F32),
                        pltpu.VMEM((block, hd), F32)],
        compiler_params=pltpu.CompilerParams(dimension_semantics=("arbitrary", "arbitrary"),
                                             vmem_limit_bytes=FOX_VMEM_LIMIT),
        interpret=interpret, name="fox_attention",
    )(pq, kv, kv, f_cum, f_cum, q_gain.reshape(1, hd), k_gain.reshape(1, hd))


def kernel(x, c, ada_w, ada_b, mix_norm, ffn_norm, a_w_in, a_b_i, a_b_f, a_h_norm, a_w_out,
           kv_ada_w, kv_ada_b, kv_norm, kv_w, kv_b_f, kv_k_norm, b_w_qo, b_q_norm, b_w_out,
           peer_wq, peer_subkeys, peer_u, peer_v):
    B, S, D = x.shape
    n = B * S
    cs = jax.nn.silu(c)
    x2 = x.reshape(n, D)
    kv = f_cum = None
    for l in range(DEPTH):
        mod = cs @ ada_w[l] + ada_b[l]
        sh1, sc1, g1, sh2, sc2, g2 = jnp.split(mod, N_ADA, axis=-1)
        if l < N_A_LAYERS:
            split = 2 * MLSTM_QK_W + MLSTM_V_W + D_MODEL
            p, gates = _norm_proj(x2, sh1, sc1, mix_norm[l], a_w_in[l][:, :split], a_w_in[l][:, split:], S)
            h = _mlstm_scan(p, gates, a_b_i[l], a_b_f[l], B, S)
            x2 = _out_proj(h, p, 2, x2, g1, a_h_norm[l], a_w_out[l], S, MLSTM_V_DIM)
        else:
            j = l - N_A_LAYERS
            pq, _ = _norm_proj(x2, sh1, sc1, mix_norm[l], b_w_qo[j], None, S)
            att = _fox_attention(pq, kv, f_cum, b_q_norm[j], kv_k_norm, B, S)
            x2 = _out_proj(att, pq, 1, x2, g1, None, b_w_out[j], S, None)
        y = _peer(x2.reshape(B, S, D), ffn_norm[l], sh2, sc2, peer_wq[l], peer_subkeys[l],
                  _pack_expert_table(peer_u[l]), _pack_expert_table(peer_v[l]))
        x2 = x2 + (g2[:, None, :] * y).reshape(n, D)
        if l == N_A_LAYERS - 1:
            sh, sc = jnp.split(cs @ kv_ada_w + kv_ada_b, 2, axis=-1)
            kv, fg = _norm_proj(x2, sh, sc, kv_norm, kv_w[:, :2 * D], kv_w[:, 2 * D:], S)
            log_f = jax.nn.log_sigmoid(fg[:, :FOX_HEADS] + kv_b_f)
            f_cum = jnp.cumsum(log_f.reshape(B, S, FOX_HEADS), axis=1).reshape(n, FOX_HEADS)
    return x2.reshape(B, S, D)
```

```python
import functools
import math

import jax
import jax.numpy as jnp
from jax import lax
from jax.experimental import pallas as pl
from jax.experimental.pallas import tpu as pltpu

F32 = jnp.float32
BF16 = jnp.bfloat16

D_MODEL = 1024
DEPTH = 2
N_A_LAYERS = DEPTH // 2
EPS = 1e-6
N_ADA = 6

MLSTM_HEADS = 4
MLSTM_QK_DIM = D_MODEL // (2 * MLSTM_HEADS)
MLSTM_V_DIM = D_MODEL // MLSTM_HEADS
MLSTM_CHUNK = 64
GATE_SOFTCAP = 15.0
MLSTM_QK_W = MLSTM_HEADS * MLSTM_QK_DIM
MLSTM_V_W = MLSTM_HEADS * MLSTM_V_DIM

FOX_HEADS = 8
FOX_HEAD_DIM = D_MODEL // FOX_HEADS
Q_BLOCK = 128

PEER_HEADS = 8
PEER_KEYS = 128
PEER_EXPERTS = PEER_KEYS * PEER_KEYS
PEER_QUERY_DIM = 256
PEER_HALF = PEER_QUERY_DIM // 2
PEER_TOPK = 16

SUBLANES = 8
LANES = 128
ROW_WORDS = SUBLANES * LANES
assert ROW_WORDS == D_MODEL
PEER_SLOTS = PEER_HEADS * PEER_TOPK
HALF_EXPERTS = PEER_EXPERTS // 2
SUBROWS = 2 * SUBLANES
GATHER_ROWS = PEER_SLOTS * SUBROWS
TOKEN_GROUP = SUBLANES
PEER_TOKEN_TILE = 64
PEER_VMEM_LIMIT = 48 * 1024 * 1024


def _pack_expert_table(t):
    lo = lax.bitcast_convert_type(t[:HALF_EXPERTS].astype(BF16), jnp.uint16).astype(jnp.uint32)
    hi = lax.bitcast_convert_type(t[HALF_EXPERTS:].astype(BF16), jnp.uint16).astype(jnp.uint32)
    w = lo | (hi << 16)
    return lax.bitcast_convert_type(w, jnp.int32).reshape(HALF_EXPERTS, SUBLANES, LANES)


def _slot_expand_matrix():
    slot = lax.broadcasted_iota(jnp.int32, (PEER_SLOTS, GATHER_ROWS), 0)
    sub = lax.broadcasted_iota(jnp.int32, (PEER_SLOTS, GATHER_ROWS), 1)
    return (sub // SUBROWS == slot).astype(BF16)


def _diag_mask():
    r = lax.broadcasted_iota(jnp.int32, (SUBLANES, GATHER_ROWS), 0)
    sub = lax.broadcasted_iota(jnp.int32, (SUBLANES, GATHER_ROWS), 1)
    return (sub % SUBROWS) // 2 == r


def _half_mask(hs, e):
    hsx = jnp.dot(hs.astype(BF16), e, preferred_element_type=F32)
    par = (lax.broadcasted_iota(jnp.int32, hsx.shape, 1) % 2).astype(F32)
    return hsx == par


def _gather_token(idx_ref, tab_ref, gb_ref, t):
    for s in range(PEER_SLOTS):
        gb_ref[pl.ds(s * SUBLANES, SUBLANES), :] = tab_ref[idx_ref[t, s]]
    return pltpu.bitcast(gb_ref[...], BF16)


def _split_hi_lo(a):
    hi = a.astype(BF16)
    lo = (a - hi.astype(F32)).astype(BF16)
    return hi, lo


def _peer_down_kernel(idx_ref, xs_ref, hs_ref, gate_ref, tab_ref, e_ref, sel_ref, w_ref, gb_ref, z_ref):
    tt = w_ref.shape[0]
    diag = _diag_mask()

    def group(g, carry):
        rows = []
        for i in range(TOKEN_GROUP):
            t = g * TOKEN_GROUP + i
            gath = _gather_token(idx_ref, tab_ref, gb_ref, t)
            y = lax.dot_general(xs_ref[t], gath, (((1,), (1,)), ((), ())),
                                preferred_element_type=F32)
            y8 = y[:SUBLANES] + y[SUBLANES:]
            rows.append(jnp.sum(jnp.where(diag, y8, 0.0), axis=0, keepdims=True))
        z_ref[pl.ds(pl.multiple_of(g * TOKEN_GROUP, TOKEN_GROUP), TOKEN_GROUP), :] = jnp.concatenate(rows, axis=0)
        return carry

    lax.fori_loop(0, tt // TOKEN_GROUP, group, 0)
    zm = jnp.where(_half_mask(hs_ref[...], e_ref[...]), z_ref[...], 0.0)
    act = jnp.dot(zm, sel_ref[...], precision=lax.Precision.HIGHEST, preferred_element_type=F32)
    gelu = 0.5 * act * (1.0 + lax.erf(act * (1.0 / math.sqrt(2.0))))
    w_ref[...] = gate_ref[...] * gelu


def _peer_up_kernel(idx_ref, w_ref, hs_ref, tab_ref, e_ref, y_ref, gb_ref, ahi_ref, alo_ref):
    tt = w_ref.shape[0]
    diag = _diag_mask()
    e = e_ref[...]
    hm = _half_mask(hs_ref[...], e)
    w_hi, w_lo = _split_hi_lo(w_ref[...])
    ahi_ref[...] = jnp.where(hm, jnp.dot(w_hi, e, preferred_element_type=F32), 0.0)
    alo_ref[...] = jnp.where(hm, jnp.dot(w_lo, e, preferred_element_type=F32), 0.0)

    def group(g, carry):
        base = pl.multiple_of(g * TOKEN_GROUP, TOKEN_GROUP)
        a_hi = ahi_ref[pl.ds(base, TOKEN_GROUP), :]
        a_lo = alo_ref[pl.ds(base, TOKEN_GROUP), :]
        for i in range(TOKEN_GROUP):
            t = g * TOKEN_GROUP + i
            gath = _gather_token(idx_ref, tab_ref, gb_ref, t)
            lhs = jnp.concatenate(
                [jnp.where(diag, a_hi[i:i + 1, :], 0.0), jnp.where(diag, a_lo[i:i + 1, :], 0.0)],
                axis=0).astype(BF16)
            out = jnp.dot(lhs, gath, preferred_element_type=F32)
            y_ref[t] = out[:SUBLANES] + out[SUBLANES:]
        return carry

    lax.fori_loop(0, tt // TOKEN_GROUP, group, 0)


def _peer_experts(xs, rows, hs, gate, tab_u, tab_v, *, token_tile=PEER_TOKEN_TILE, interpret=False):
    n = xs.shape[0]
    tt = token_tile
    assert n % tt == 0 and tt % TOKEN_GROUP == 0
    e = _slot_expand_matrix()
    sel = e.T.astype(F32)

    tok = lambda i: (i, 0)
    const2 = lambda i: (0, 0)
    smem_idx = pl.BlockSpec((tt, PEER_SLOTS), tok, memory_space=pltpu.SMEM)
    slot_spec = pl.BlockSpec((tt, PEER_SLOTS), tok)
    table_spec = pl.BlockSpec(memory_space=pltpu.VMEM)
    params = pltpu.CompilerParams(dimension_semantics=("arbitrary",), vmem_limit_bytes=PEER_VMEM_LIMIT)

    w = pl.pallas_call(
        _peer_down_kernel,
        grid=(n // tt,),
        in_specs=[smem_idx,
                  pl.BlockSpec((tt, SUBROWS, LANES), lambda i: (i, 0, 0)),
                  slot_spec, slot_spec, table_spec,
                  pl.BlockSpec((PEER_SLOTS, GATHER_ROWS), const2),
                  pl.BlockSpec((GATHER_ROWS, PEER_SLOTS), const2)],
        out_specs=slot_spec,
        out_shape=jax.ShapeDtypeStruct((n, PEER_SLOTS), F32),
        scratch_shapes=[pltpu.VMEM((PEER_SLOTS * SUBLANES, LANES), jnp.int32),
                        pltpu.VMEM((tt, GATHER_ROWS), F32)],
        compiler_params=params,
        interpret=interpret,
        name="peer_down",
    )(rows, xs, hs, gate, tab_u, e, sel)

    y = pl.pallas_call(
        _peer_up_kernel,
        grid=(n // tt,),
        in_specs=[smem_idx, slot_spec, slot_spec, table_spec,
                  pl.BlockSpec((PEER_SLOTS, GATHER_ROWS), const2)],
        out_specs=pl.BlockSpec((tt, SUBLANES, LANES), lambda i: (i, 0, 0)),
        out_shape=jax.ShapeDtypeStruct((n, SUBLANES, LANES), F32),
        scratch_shapes=[pltpu.VMEM((PEER_SLOTS * SUBLANES, LANES), jnp.int32),
                        pltpu.VMEM((tt, GATHER_ROWS), F32),
                        pltpu.VMEM((tt, GATHER_ROWS), F32)],
        compiler_params=params,
        interpret=interpret,
        name="peer_up",
    )(rows, w, hs, tab_v, e)
    return y.reshape(n, D_MODEL)


def _ada_rmsnorm(x, gain, shift, scale):
    xf = x * lax.rsqrt(jnp.mean(x * x, axis=-1, keepdims=True) + EPS) * gain
    return xf * (1.0 + scale[:, None, :]) + shift[:, None, :]


def _head_rms(t, gain):
    return t * lax.rsqrt(jnp.mean(t * t, axis=-1, keepdims=True) + EPS) * gain


def _softcap(z):
    return GATE_SOFTCAP * jnp.tanh(z / GATE_SOFTCAP)


def _dot3(a_hi, a_lo, b_hi, b_lo, dims):
    dot = functools.partial(lax.dot_general, dimension_numbers=(dims, ((), ())), preferred_element_type=F32)
    return dot(a_hi, b_hi) + (dot(a_hi, b_lo) + dot(a_lo, b_hi))


def _top_rows(s, k):
    nrows = s.shape[0]
    row = lax.broadcasted_iota(jnp.int32, s.shape, 0)
    vals, rows = [], []
    for _ in range(k):
        m = jnp.max(s, axis=0, keepdims=True)
        r = jnp.min(jnp.where(s == m, row, nrows), axis=0, keepdims=True)
        vals.append(m)
        rows.append(r)
        s = jnp.where(row == r, -jnp.inf, s)
    return jnp.concatenate(vals, axis=0), jnp.concatenate(rows, axis=0)


def _tree(op, xs):
    while len(xs) > 1:
        xs = [op(xs[i], xs[i + 1]) if i + 1 < len(xs) else xs[i] for i in range(0, len(xs), 2)]
    return xs[0]


def _product_candidates(s0, s1, i0, i1):
    k = PEER_TOPK
    t = s0.shape[1]
    sub = lax.broadcasted_iota(jnp.int32, (SUBLANES, t), 0)
    blocks = []
    for half in range(k // SUBLANES):
        b = sub + half * SUBLANES
        lo = half * SUBLANES
        blocks.append((s0[0:1] + s1[lo:lo + SUBLANES], b, i0[0:1] * PEER_KEYS + i1[lo:lo + SUBLANES]))
    for a in range(1, SUBLANES):
        valid = (a + 1) * (sub + 1) <= k
        blocks.append((jnp.where(valid, s0[a:a + 1] + s1[0:SUBLANES], -jnp.inf), a * k + sub,
                       i0[a:a + 1] * PEER_KEYS + i1[0:SUBLANES]))
    a = sub + SUBLANES
    blocks.append((s0[SUBLANES:k] + s1[0:1], a * k, i0[SUBLANES:k] * PEER_KEYS + i1[0:1]))
    return blocks


def _peer_route_kernel(x_ref, shift_ref, scale_ref, gain_ref, wqh_ref, wql_ref, skh_ref, skl_ref,
                       rows_ref, hs_ref, gate_ref, xs_ref, q_ref, tv_ref, ti_ref, bs_ref, be_ref):
    tt = x_ref.shape[0]
    k = PEER_TOPK
    x = x_ref[...]
    xn = x * lax.rsqrt(jnp.mean(x * x, axis=-1, keepdims=True) + EPS) * gain_ref[...]
    xn = xn * (1.0 + scale_ref[0]) + shift_ref[0]
    x_hi, x_lo = _split_hi_lo(xn)
    xs_ref[:, :D_MODEL] = x_hi
    xs_ref[:, D_MODEL:] = x_lo
    q = _dot3(x_hi, x_lo, wqh_ref[...], wql_ref[...], ((1,), (0,)))
    for hp in range(2 * PEER_HEADS):
        q_ref[hp] = q[:, hp * PEER_HALF:(hp + 1) * PEER_HALF]

    def sub_topk(hp, carry):
        q_hi, q_lo = _split_hi_lo(q_ref[hp])
        p = hp % 2
        s = _dot3(skh_ref[p], skl_ref[p], q_hi, q_lo, ((1,), (1,)))
        tv_ref[hp], ti_ref[hp] = _top_rows(s, k)
        return carry

    lax.fori_loop(0, 2 * PEER_HEADS, sub_topk, 0)

    def head_topk(h, carry):
        blocks = _product_candidates(tv_ref[2 * h], tv_ref[2 * h + 1], ti_ref[2 * h], ti_ref[2 * h + 1])
        sums = [b[0] for b in blocks]
        best_s, best_e = [], []
        for _ in range(k):
            m = jnp.max(_tree(jnp.maximum, sums), axis=0, keepdims=True)
            pos = jnp.min(_tree(jnp.minimum, [jnp.where(c == m, b[1], k * k) for c, b in zip(sums, blocks)]),
                          axis=0, keepdims=True)
            hit = [b[1] == pos for b in blocks]
            e = jnp.max(_tree(jnp.maximum, [jnp.where(hh, b[2], -1) for hh, b in zip(hit, blocks)]),
                        axis=0, keepdims=True)
            sums = [jnp.where(hh, -jnp.inf, c) for hh, c in zip(hit, sums)]
            best_s.append(m)
            best_e.append(e)
        bs = jnp.concatenate(best_s, axis=0)
        ex = jnp.exp(bs - bs[0:1])
        off = pl.multiple_of(h * k, k)
        bs_ref[pl.ds(off, k), :] = ex / jnp.sum(ex, axis=0, keepdims=True)
        be_ref[pl.ds(off, k), :] = jnp.concatenate(best_e, axis=0).astype(F32)
        return carry

    lax.fori_loop(0, PEER_HEADS, head_topk, 0)
    gate_ref[...] = bs_ref[...].T
    ids = be_ref[...].T.astype(jnp.int32)
    rows_ref[...] = ids % HALF_EXPERTS
    hs_ref[...] = (ids // HALF_EXPERTS).astype(F32)


ROUTE_TOKEN_TILE = 256
ROUTE_VMEM_LIMIT = 40 * 1024 * 1024


def _peer_route(x2, shift, scale, gain, wq, subkeys, seq_len, *, token_tile=ROUTE_TOKEN_TILE, interpret=False):
    n = x2.shape[0]
    tt = token_tile
    assert seq_len % tt == 0 and n % seq_len == 0
    tiles_per_seq = seq_len // tt
    batch = n // seq_len
    wq_hi, wq_lo = _split_hi_lo(wq)
    sk_hi, sk_lo = _split_hi_lo(subkeys)
    tok = lambda i: (i, 0)
    per_seq = pl.BlockSpec((1, 1, D_MODEL), lambda i: (i // tiles_per_seq, 0, 0))
    whole = pl.BlockSpec(memory_space=pltpu.VMEM)
    slot_spec = pl.BlockSpec((tt, PEER_SLOTS), tok)
    slot_shape = lambda dt: jax.ShapeDtypeStruct((n, PEER_SLOTS), dt)
    return pl.pallas_call(
        _peer_route_kernel,
        grid=(n // tt,),
        in_specs=[pl.BlockSpec((tt, D_MODEL), tok), per_seq, per_seq,
                  pl.BlockSpec((1, D_MODEL), lambda i: (0, 0)), whole, whole, whole, whole],
        out_specs=[slot_spec, slot_spec, slot_spec, pl.BlockSpec((tt, 2 * D_MODEL), tok)],
        out_shape=[slot_shape(jnp.int32), slot_shape(F32), slot_shape(F32),
                   jax.ShapeDtypeStruct((n, 2 * D_MODEL), BF16)],
        scratch_shapes=[pltpu.VMEM((2 * PEER_HEADS, tt, PEER_HALF), F32),
                        pltpu.VMEM((2 * PEER_HEADS, PEER_TOPK, tt), F32),
                        pltpu.VMEM((2 * PEER_HEADS, PEER_TOPK, tt), jnp.int32),
                        pltpu.VMEM((PEER_SLOTS, tt), F32),
                        pltpu.VMEM((PEER_SLOTS, tt), F32)],
        compiler_params=pltpu.CompilerParams(dimension_semantics=("arbitrary",),
                                             vmem_limit_bytes=ROUTE_VMEM_LIMIT),
        interpret=interpret,
        name="peer_route",
    )(x2, shift.reshape(batch, 1, D_MODEL), scale.reshape(batch, 1, D_MODEL), gain.reshape(1, D_MODEL),
      wq_hi, wq_lo, sk_hi, sk_lo)


def _peer(x, gain, shift, scale, wq, subkeys, tab_u, tab_v):
    B, S, D = x.shape
    rows, hs, gate, xs = _peer_route(x.reshape(B * S, D), shift, scale, gain, wq, subkeys, S)
    y = _peer_experts(xs.reshape(B * S, SUBROWS, LANES), rows, hs, gate, tab_u, tab_v)
    return y.reshape(B, S, D)


PROJ_TOKEN_TILE = 512
PROJ_VMEM_LIMIT = 48 * 1024 * 1024


def _ada_norm_tile(x, gain, shift, scale):
    xn = x * lax.rsqrt(jnp.mean(x * x, axis=-1, keepdims=True) + EPS) * gain
    return xn * (1.0 + scale) + shift


def _norm_proj_kernel(x_ref, shift_ref, scale_ref, gain_ref, w_ref, *rest, with_gate):
    x_hi, x_lo = _split_hi_lo(_ada_norm_tile(x_ref[...], gain_ref[...], shift_ref[0], scale_ref[0]))
    if with_gate:
        wgh_ref, wgl_ref, main_ref, gate_ref = rest
        gate_ref[...] = _dot3(x_hi, x_lo, wgh_ref[...], wgl_ref[...], ((1,), (0,)))
    else:
        (main_ref,) = rest
    main_ref[...] = jnp.dot(x_hi, w_ref[...], preferred_element_type=F32)


def _norm_proj(x2, shift, scale, gain, w_main, w_gate, seq_len, *, token_tile=PROJ_TOKEN_TILE, interpret=False):
    n = x2.shape[0]
    tt = token_tile
    assert seq_len % tt == 0 and n % seq_len == 0
    tiles_per_seq = seq_len // tt
    batch = n // seq_len
    m = w_main.shape[1]
    tok = lambda i: (i, 0)
    per_seq = pl.BlockSpec((1, 1, D_MODEL), lambda i: (i // tiles_per_seq, 0, 0))
    whole = pl.BlockSpec(memory_space=pltpu.VMEM)
    in_specs = [pl.BlockSpec((tt, D_MODEL), tok), per_seq, per_seq,
                pl.BlockSpec((1, D_MODEL), lambda i: (0, 0)), whole]
    args = [x2, shift.reshape(batch, 1, D_MODEL), scale.reshape(batch, 1, D_MODEL), gain.reshape(1, D_MODEL),
            w_main.astype(BF16)]
    out_specs = [pl.BlockSpec((tt, m), tok)]
    out_shape = [jax.ShapeDtypeStruct((n, m), F32)]
    if w_gate is not None:
        wg_hi, wg_lo = _split_hi_lo(jnp.pad(w_gate, ((0, 0), (0, LANES - w_gate.shape[1]))))
        in_specs += [whole, whole]
        args += [wg_hi, wg_lo]
        out_specs.append(pl.BlockSpec((tt, LANES), tok))
        out_shape.append(jax.ShapeDtypeStruct((n, LANES), F32))
    outs = pl.pallas_call(
        functools.partial(_norm_proj_kernel, with_gate=w_gate is not None),
        grid=(n // tt,), in_specs=in_specs, out_specs=out_specs, out_shape=out_shape,
        compiler_params=pltpu.CompilerParams(dimension_semantics=("arbitrary",),
                                             vmem_limit_bytes=PROJ_VMEM_LIMIT),
        interpret=interpret, name="norm_proj",
    )(*args)
    return (outs[0], outs[1]) if w_gate is not None else (outs[0], None)


def _out_proj_kernel(h_ref, og_ref, x_ref, g_ref, hg_ref, w_ref, o_ref, *, head_dim):
    h = h_ref[...]
    if head_dim is not None:
        parts = []
        for j in range(D_MODEL // head_dim):
            hb = h[:, j * head_dim:(j + 1) * head_dim]
            parts.append(hb * lax.rsqrt(jnp.mean(hb * hb, axis=-1, keepdims=True) + EPS))
        h = jnp.concatenate(parts, axis=1) * hg_ref[...]
    a = jax.nn.sigmoid(og_ref[...]) * h
    y = jnp.dot(a.astype(BF16), w_ref[...], preferred_element_type=F32)
    o_ref[...] = x_ref[...] + g_ref[0] * y


def _out_proj(h2, p, og_block, x2, g, h_gain, w_out, seq_len, head_dim, *, token_tile=PROJ_TOKEN_TILE,
              interpret=False):
    n = x2.shape[0]
    tt = token_tile
    tiles_per_seq = seq_len // tt
    batch = n // seq_len
    tok = lambda i: (i, 0)
    row = pl.BlockSpec((tt, D_MODEL), tok)
    gain = jnp.ones((1, D_MODEL), F32) if h_gain is None else h_gain.reshape(1, D_MODEL)
    return pl.pallas_call(
        functools.partial(_out_proj_kernel, head_dim=head_dim),
        grid=(n // tt,),
        in_specs=[row, pl.BlockSpec((tt, D_MODEL), lambda i: (i, og_block)), row,
                  pl.BlockSpec((1, 1, D_MODEL), lambda i: (i // tiles_per_seq, 0, 0)),
                  pl.BlockSpec((1, D_MODEL), lambda i: (0, 0)),
                  pl.BlockSpec(memory_space=pltpu.VMEM)],
        out_specs=row,
        out_shape=jax.ShapeDtypeStruct((n, D_MODEL), F32),
        compiler_params=pltpu.CompilerParams(dimension_semantics=("arbitrary",),
                                             vmem_limit_bytes=PROJ_VMEM_LIMIT),
        interpret=interpret, name="out_proj",
    )(h2, p, x2, g.reshape(batch, 1, D_MODEL), gain, w_out.astype(BF16))


MLSTM_STATE_W = MLSTM_V_DIM + LANES


def _mlstm_kernel(q_ref, k_ref, v_ref, gc_ref, gr_ref, h_ref, cn_ref, m_ref):
    H, DQK, DV, L = MLSTM_HEADS, MLSTM_QK_DIM, MLSTM_V_DIM, MLSTM_CHUNK

    @pl.when(pl.program_id(1) == 0)
    def _():
        cn_ref[...] = jnp.zeros_like(cn_ref)
        m_ref[...] = jnp.zeros_like(m_ref)

    tril = lax.broadcasted_iota(jnp.int32, (L, L), 1) <= lax.broadcasted_iota(jnp.int32, (L, L), 0)
    gc = gc_ref[...]
    gr = gr_ref[0]
    ones = jnp.ones((L, LANES), F32)
    for h in range(H):
        ig_col, b_col = gc[:, h:h + 1], gc[:, H + h:H + h + 1]
        ig_row, b_row = gr[h:h + 1, :], gr[H + h:H + h + 1, :]
        m_prev = m_ref[h:h + 1, 0:1]
        d_log = jnp.where(tril, b_col - b_row + ig_row, -jnp.inf)
        inter = b_col + m_prev
        m_t = jnp.maximum(inter, jnp.max(d_log, axis=1, keepdims=True))
        w = jnp.exp(d_log - m_t)
        a_inter = jnp.exp(inter - m_t)
        qh = (q_ref[:, h * DQK:(h + 1) * DQK] * (DQK ** -0.5)).astype(BF16)
        kf = k_ref[:, h * DQK:(h + 1) * DQK]
        vh = v_ref[:, h * DV:(h + 1) * DV]
        s = lax.dot_general(qh, kf.astype(BF16), (((1,), (1,)), ((), ())), preferred_element_type=F32) * w
        cn = cn_ref[h]
        qc = jnp.dot(qh, cn.astype(BF16), preferred_element_type=F32)
        num = a_inter * qc[:, :DV] + jnp.dot(s.astype(BF16), vh.astype(BF16), preferred_element_type=F32)
        den = a_inter * qc[:, DV:DV + 1] + jnp.sum(s, axis=1, keepdims=True)
        h_ref[:, h * DV:(h + 1) * DV] = num / jnp.maximum(jnp.abs(den), jnp.exp(-m_t))
        b_last = b_col[L - 1:L, :]
        g_col = b_last - b_col + ig_col
        m_new = jnp.maximum(b_last + m_prev, jnp.max(g_col, axis=0, keepdims=True))
        w_s = jnp.exp(g_col - m_new)
        decay = jnp.exp(b_last + m_prev - m_new)
        kw = (kf * w_s).astype(BF16)
        vaug = jnp.concatenate([vh, ones], axis=1).astype(BF16)
        cn_ref[h] = decay * cn + lax.dot_general(kw, vaug, (((0,), (0,)), ((), ())), preferred_element_type=F32)
        m_ref[h:h + 1, :] = jnp.broadcast_to(m_new, (1, LANES))


def _mlstm_scan(p, gates, b_i, b_f, batch, seq_len, *, interpret=False):
    H, L = MLSTM_HEADS, MLSTM_CHUNK
    n = p.shape[0]
    nc = seq_len // L
    ig = _softcap(gates[:, :H] + b_i)
    lf = jax.nn.log_sigmoid(_softcap(gates[:, H:2 * H] + b_f))
    b = jnp.cumsum(lf.reshape(n // L, L, H), axis=1).reshape(n, H)
    gc = jnp.concatenate([ig, b], axis=1)
    gr = gc.reshape(n // L, L, 2 * H).transpose(0, 2, 1)
    chunk = lambda bi, c: bi * nc + c
    return pl.pallas_call(
        _mlstm_kernel,
        grid=(batch, nc),
        in_specs=[pl.BlockSpec((L, MLSTM_QK_W), lambda bi, c: (chunk(bi, c), 0)),
                  pl.BlockSpec((L, MLSTM_QK_W), lambda bi, c: (chunk(bi, c), 1)),
                  pl.BlockSpec((L, MLSTM_V_W), lambda bi, c: (chunk(bi, c), 1)),
                  pl.BlockSpec((L, 2 * H), lambda bi, c: (chunk(bi, c), 0)),
                  pl.BlockSpec((1, 2 * H, L), lambda bi, c: (chunk(bi, c), 0, 0))],
        out_specs=pl.BlockSpec((L, D_MODEL), lambda bi, c: (chunk(bi, c), 0)),
        out_shape=jax.ShapeDtypeStruct((n, D_MODEL), F32),
        scratch_shapes=[pltpu.VMEM((H, MLSTM_QK_DIM, MLSTM_STATE_W), F32),
                        pltpu.VMEM((SUBLANES, LANES), F32)],
        compiler_params=pltpu.CompilerParams(dimension_semantics=("arbitrary", "arbitrary")),
        interpret=interpret, name="mlstm_scan",
    )(p, p, p, gc, gr)


FOX_BLOCK = 512
FOX_QUERY_PART = 256
FOX_SUM_ROWS = 16
FOX_VMEM_LIMIT = 48 * 1024 * 1024


def _bias_columns(f, query_side):
    hi = f.astype(BF16).astype(F32)
    r1 = f - hi
    lo = r1.astype(BF16).astype(F32)
    lo2 = r1 - lo
    lane = lax.broadcasted_iota(jnp.int32, (f.shape[0], LANES), 1)
    if query_side:
        vals = jnp.where(lane == 0, hi, jnp.where(lane == 1, lo, jnp.where(lane == 2, lo2,
                         jnp.where(lane < 6, 1.0, 0.0))))
    else:
        vals = jnp.where(lane < 3, 1.0, jnp.where(lane == 3, -hi, jnp.where(lane == 4, -lo,
                         jnp.where(lane == 5, -lo2, 0.0))))
    return vals.astype(BF16)


def _head_column(fc, h):
    lane = lax.broadcasted_iota(jnp.int32, fc.shape, 1)
    return jnp.sum(jnp.where(lane == h, fc, 0.0), axis=1, keepdims=True)


def _rms_rows(t, gain):
    return t * lax.rsqrt(jnp.mean(t * t, axis=-1, keepdims=True) + EPS) * gain


def _fox_attn_kernel(q_ref, k_ref, v_ref, fq_ref, fk_ref, qg_ref, kg_ref, o_ref,
                     ka_ref, vt_ref, qa_ref, st_cur_ref, st_next_ref, *state_refs):
    i = pl.program_id(1)
    h = pl.program_id(0) % FOX_HEADS
    tq = q_ref.shape[0]
    nk = k_ref.shape[0] // tq
    hd = FOX_HEAD_DIM
    part = FOX_QUERY_PART
    n_parts = tq // part
    m_refs, acc_refs = state_refs[:n_parts], state_refs[n_parts:]

    @pl.when(i == 0)
    def _():
        def prep(c, carry):
            r = pl.ds(pl.multiple_of(c * tq, tq), tq)
            ka_ref[r, :hd] = _rms_rows(k_ref[r, :], kg_ref[...]).astype(BF16)
            ka_ref[r, hd:] = _bias_columns(_head_column(fk_ref[r, :], h), False)
            vt_ref[c, :hd, :] = v_ref[r, :].T.astype(BF16)
            vt_ref[c, hd:, :] = jnp.ones((FOX_SUM_ROWS, tq), BF16)
            return carry
        lax.fori_loop(0, nk, prep, 0)

    qn = _rms_rows(q_ref[...], qg_ref[...]) * (hd ** -0.5)
    qa_ref[:, :hd] = qn.astype(BF16)
    qa_ref[:, hd:] = _bias_columns(_head_column(fq_ref[...], h), True)
    for m_ref, acc_ref in zip(m_refs, acc_refs):
        m_ref[...] = jnp.full_like(m_ref, -jnp.inf)
        acc_ref[...] = jnp.zeros_like(acc_ref)

    def scores(j, st_ref):
        kblk = ka_ref[pl.ds(pl.multiple_of(j * tq, tq), tq), :]
        for c in range(n_parts):
            st_ref[c] = lax.dot_general(kblk, qa_ref[c * part:(c + 1) * part, :],
                                        (((1,), (1,)), ((), ())), preferred_element_type=F32)

    def softmax_pv(j, masked):
        vtblk = vt_ref[j]
        pts, alphas = [], []
        for c, m_ref in enumerate(m_refs):
            st = st_cur_ref[c]
            if masked:
                key = lax.broadcasted_iota(jnp.int32, st.shape, 0)
                qry = lax.broadcasted_iota(jnp.int32, st.shape, 1) + c * part
                st = jnp.where(key <= qry, st, -jnp.inf)
            m_prev = m_ref[0:1, :]
            m_new = jnp.maximum(m_prev, jnp.max(st, axis=0, keepdims=True))
            alphas.append(jnp.exp(m_prev - m_new))
            pts.append(jnp.exp(st - m_new).astype(BF16))
            m_ref[0:1, :] = m_new
        for pt, alpha, acc_ref in zip(pts, alphas, acc_refs):
            acc_ref[...] = alpha * acc_ref[...] + jnp.dot(vtblk, pt, preferred_element_type=F32)

    def body(j, carry):
        scores(j + 1, st_next_ref)
        softmax_pv(j, False)
        st_cur_ref[...] = st_next_ref[...]
        return carry

    scores(0, st_cur_ref)
    lax.fori_loop(0, i, body, 0)
    softmax_pv(i, True)
    for c, acc_ref in enumerate(acc_refs):
        acc = acc_ref[...]
        o_ref[c * part:(c + 1) * part, :] = (acc[:hd] / acc[hd:hd + 1]).T


def _fox_attention(pq, kv, f_cum, q_gain, k_gain, batch, seq_len, *, block=FOX_BLOCK, interpret=False):
    n = pq.shape[0]
    hd, nh = FOX_HEAD_DIM, FOX_HEADS
    nq = seq_len // block
    return pl.pallas_call(
        _fox_attn_kernel,
        grid=(batch * nh, nq),
        in_specs=[pl.BlockSpec((block, hd), lambda bh, i: ((bh // nh) * nq + i, bh % nh)),
                  pl.BlockSpec((seq_len, hd), lambda bh, i: (bh // nh, bh % nh)),
                  pl.BlockSpec((seq_len, hd), lambda bh, i: (bh // nh, nh + bh % nh)),
                  pl.BlockSpec((block, nh), lambda bh, i: ((bh // nh) * nq + i, 0)),
                  pl.BlockSpec((seq_len, nh), lambda bh, i: (bh // nh, 0)),
                  pl.BlockSpec((1, hd), lambda bh, i: (0, 0)),
                  pl.BlockSpec((1, hd), lambda bh, i: (0, 0))],
        out_specs=pl.BlockSpec((block, hd), lambda bh, i: ((bh // nh) * nq + i, bh % nh)),
        out_shape=jax.ShapeDtypeStruct((n, D_MODEL), F32),
        scratch_shapes=[pltpu.VMEM((seq_len, 2 * hd), BF16),
                        pltpu.VMEM((nq, hd + FOX_SUM_ROWS, block), BF16),
                        pltpu.VMEM((block, 2 * hd), BF16),
                        *[pltpu.VMEM((block // FOX_QUERY_PART, block, FOX_QUERY_PART), F32)] * 2,
                        *[pltpu.VMEM((SUBLANES, FOX_QUERY_PART), F32)] * (block // FOX_QUERY_PART),
                        *[pltpu.VMEM((hd + FOX_SUM_ROWS, FOX_QUERY_PART), F32)] * (block // FOX_QUERY_PART)],
        compiler_params=pltpu.CompilerParams(dimension_semantics=("arbitrary", "arbitrary"),
                                             vmem_limit_bytes=FOX_VMEM_LIMIT),
        interpret=interpret, name="fox_attention",
    )(pq, kv, kv, f_cum, f_cum, q_gain.reshape(1, hd), k_gain.reshape(1, hd))


def kernel(x, c, ada_w, ada_b, mix_norm, ffn_norm, a_w_in, a_b_i, a_b_f, a_h_norm, a_w_out,
           kv_ada_w, kv_ada_b, kv_norm, kv_w, kv_b_f, kv_k_norm, b_w_qo, b_q_norm, b_w_out,
           peer_wq, peer_subkeys, peer_u, peer_v):
    B, S, D = x.shape
    n = B * S
    cs = jax.nn.silu(c)
    x2 = x.reshape(n, D)
    kv = f_cum = None
    for l in range(DEPTH):
        mod = cs @ ada_w[l] + ada_b[l]
        sh1, sc1, g1, sh2, sc2, g2 = jnp.split(mod, N_ADA, axis=-1)
        if l < N_A_LAYERS:
            split = 2 * MLSTM_QK_W + MLSTM_V_W + D_MODEL
            p, gates = _norm_proj(x2, sh1, sc1, mix_norm[l], a_w_in[l][:, :split], a_w_in[l][:, split:], S)
            h = _mlstm_scan(p, gates, a_b_i[l], a_b_f[l], B, S)
            x2 = _out_proj(h, p, 2, x2, g1, a_h_norm[l], a_w_out[l], S, MLSTM_V_DIM)
        else:
            j = l - N_A_LAYERS
            pq, _ = _norm_proj(x2, sh1, sc1, mix_norm[l], b_w_qo[j], None, S)
            att = _fox_attention(pq, kv, f_cum, b_q_norm[j], kv_k_norm, B, S)
            x2 = _out_proj(att, pq, 1, x2, g1, None, b_w_out[j], S, None)
        y = _peer(x2.reshape(B, S, D), ffn_norm[l], sh2, sc2, peer_wq[l], peer_subkeys[l],
                  _pack_expert_table(peer_u[l]), _pack_expert_table(peer_v[l]))
        x2 = x2 + (g2[:, None, :] * y).reshape(n, D)
        if l == N_A_LAYERS - 1:
            sh, sc = jnp.split(cs @ kv_ada_w + kv_ada_b, 2, axis=-1)
            kv, fg = _norm_proj(x2, sh, sc, kv_norm, kv_w[:, :2 * D], kv_w[:, 2 * D:], S)
            log_f = jax.nn.log_sigmoid(fg[:, :FOX_HEADS] + kv_b_f)
            f_cum = jnp.cumsum(log_f.reshape(B, S, FOX_HEADS), axis=1).reshape(n, FOX_HEADS)
    return x2.reshape(B, S, D)
```

```python
import functools
import math

import jax
import jax.numpy as jnp
from jax import lax
from jax.experimental import pallas as pl
from jax.experimental.pallas import tpu as pltpu

F32 = jnp.float32
BF16 = jnp.bfloat16

D_MODEL = 1024
DEPTH = 2
N_A_LAYERS = DEPTH // 2
EPS = 1e-6
N_ADA = 6

MLSTM_HEADS = 4
MLSTM_QK_DIM = D_MODEL // (2 * MLSTM_HEADS)
MLSTM_V_DIM = D_MODEL // MLSTM_HEADS
MLSTM_CHUNK = 64
GATE_SOFTCAP = 15.0
MLSTM_QK_W = MLSTM_HEADS * MLSTM_QK_DIM
MLSTM_V_W = MLSTM_HEADS * MLSTM_V_DIM

FOX_HEADS = 8
FOX_HEAD_DIM = D_MODEL // FOX_HEADS
Q_BLOCK = 128

PEER_HEADS = 8
PEER_KEYS = 128
PEER_EXPERTS = PEER_KEYS * PEER_KEYS
PEER_QUERY_DIM = 256
PEER_HALF = PEER_QUERY_DIM // 2
PEER_TOPK = 16

SUBLANES = 8
LANES = 128
ROW_WORDS = SUBLANES * LANES
assert ROW_WORDS == D_MODEL
PEER_SLOTS = PEER_HEADS * PEER_TOPK
HALF_EXPERTS = PEER_EXPERTS // 2
SUBROWS = 2 * SUBLANES
GATHER_ROWS = PEER_SLOTS * SUBROWS
TOKEN_GROUP = 2 * SUBLANES
PEER_TOKEN_TILE = 128
PEER_VMEM_LIMIT = 48 * 1024 * 1024


def _pack_expert_table(t):
    lo = lax.bitcast_convert_type(t[:HALF_EXPERTS].astype(BF16), jnp.uint16).astype(jnp.uint32)
    hi = lax.bitcast_convert_type(t[HALF_EXPERTS:].astype(BF16), jnp.uint16).astype(jnp.uint32)
    w = lo | (hi << 16)
    return lax.bitcast_convert_type(w, jnp.int32).reshape(HALF_EXPERTS, SUBLANES, LANES)


def _slot_expand_matrix():
    slot = lax.broadcasted_iota(jnp.int32, (PEER_SLOTS, GATHER_ROWS), 0)
    sub = lax.broadcasted_iota(jnp.int32, (PEER_SLOTS, GATHER_ROWS), 1)
    return (sub // SUBROWS == slot).astype(BF16)


def _diag_mask():
    r = lax.broadcasted_iota(jnp.int32, (SUBLANES, GATHER_ROWS), 0)
    sub = lax.broadcasted_iota(jnp.int32, (SUBLANES, GATHER_ROWS), 1)
    return (sub % SUBROWS) // 2 == r


def _half_mask(hs, e):
    hsx = jnp.dot(hs.astype(BF16), e, preferred_element_type=F32)
    par = (lax.broadcasted_iota(jnp.int32, hsx.shape, 1) % 2).astype(F32)
    return hsx == par


def _gather_token(idx_ref, tab_ref, gb_ref, t):
    for s in range(PEER_SLOTS // 2):
        w = idx_ref[t, s]
        a = pl.multiple_of(w & 0xFFFF, SUBLANES)
        b = pl.multiple_of(lax.shift_right_logical(w, 16), SUBLANES)
        gb_ref[pl.ds(2 * s * SUBLANES, SUBLANES), :] = tab_ref[pl.ds(a, SUBLANES), :]
        gb_ref[pl.ds((2 * s + 1) * SUBLANES, SUBLANES), :] = tab_ref[pl.ds(b, SUBLANES), :]
    return pltpu.bitcast(gb_ref[...], BF16)


def _split_hi_lo(a):
    hi = a.astype(BF16)
    lo = (a - hi.astype(F32)).astype(BF16)
    return hi, lo


def _peer_down_kernel(idx_ref, xs_ref, hs_ref, gate_ref, tab_ref, e_ref, sel_ref, w_ref, gb_ref, z_ref):
    tt = w_ref.shape[0]
    diag = _diag_mask()

    def group(g, carry):
        rows = []
        for i in range(TOKEN_GROUP):
            t = g * TOKEN_GROUP + i
            gath = _gather_token(idx_ref, tab_ref, gb_ref, t)
            y = lax.dot_general(xs_ref[t], gath, (((1,), (1,)), ((), ())),
                                preferred_element_type=F32)
            y8 = y[:SUBLANES] + y[SUBLANES:]
            rows.append(jnp.sum(jnp.where(diag, y8, 0.0), axis=0, keepdims=True))
        z_ref[pl.ds(pl.multiple_of(g * TOKEN_GROUP, TOKEN_GROUP), TOKEN_GROUP), :] = jnp.concatenate(rows, axis=0)
        return carry

    lax.fori_loop(0, tt // TOKEN_GROUP, group, 0)
    zm = jnp.where(_half_mask(hs_ref[...], e_ref[...]), z_ref[...], 0.0)
    act = jnp.dot(zm, sel_ref[...], precision=lax.Precision.HIGHEST, preferred_element_type=F32)
    gelu = 0.5 * act * (1.0 + lax.erf(act * (1.0 / math.sqrt(2.0))))
    w_ref[...] = gate_ref[...] * gelu


def _peer_up_kernel(idx_ref, w_ref, hs_ref, tab_ref, e_ref, y_ref, gb_ref, ahi_ref, alo_ref):
    tt = w_ref.shape[0]
    diag = _diag_mask()
    e = e_ref[...]
    hm = _half_mask(hs_ref[...], e)
    w_hi, w_lo = _split_hi_lo(w_ref[...])
    ahi_ref[...] = jnp.where(hm, jnp.dot(w_hi, e, preferred_element_type=F32), 0.0)
    alo_ref[...] = jnp.where(hm, jnp.dot(w_lo, e, preferred_element_type=F32), 0.0)

    def group(g, carry):
        base = pl.multiple_of(g * TOKEN_GROUP, TOKEN_GROUP)
        a_hi = ahi_ref[pl.ds(base, TOKEN_GROUP), :]
        a_lo = alo_ref[pl.ds(base, TOKEN_GROUP), :]
        for i in range(TOKEN_GROUP):
            t = g * TOKEN_GROUP + i
            gath = _gather_token(idx_ref, tab_ref, gb_ref, t)
            lhs = jnp.concatenate(
                [jnp.where(diag, a_hi[i:i + 1, :], 0.0), jnp.where(diag, a_lo[i:i + 1, :], 0.0)],
                axis=0).astype(BF16)
            out = jnp.dot(lhs, gath, preferred_element_type=F32)
            y_ref[t] = out[:SUBLANES] + out[SUBLANES:]
        return carry

    lax.fori_loop(0, tt // TOKEN_GROUP, group, 0)


def _peer_experts(xs, rows, hs, gate, tab_u, tab_v, *, token_tile=PEER_TOKEN_TILE, interpret=False):
    n = xs.shape[0]
    tt = token_tile
    assert n % tt == 0 and tt % TOKEN_GROUP == 0
    e = _slot_expand_matrix()
    sel = e.T.astype(F32)

    tok = lambda i: (i, 0)
    const2 = lambda i: (0, 0)
    smem_idx = pl.BlockSpec((tt, PEER_SLOTS // 2), tok, memory_space=pltpu.SMEM)
    rows = (rows[:, 0::2] * SUBLANES) | ((rows[:, 1::2] * SUBLANES) << 16)
    tab_u = tab_u.reshape(HALF_EXPERTS * SUBLANES, LANES)
    tab_v = tab_v.reshape(HALF_EXPERTS * SUBLANES, LANES)
    slot_spec = pl.BlockSpec((tt, PEER_SLOTS), tok)
    table_spec = pl.BlockSpec(memory_space=pltpu.VMEM)
    params = pltpu.CompilerParams(dimension_semantics=("arbitrary",), vmem_limit_bytes=PEER_VMEM_LIMIT)

    w = pl.pallas_call(
        _peer_down_kernel,
        grid=(n // tt,),
        in_specs=[smem_idx,
                  pl.BlockSpec((tt, SUBROWS, LANES), lambda i: (i, 0, 0)),
                  slot_spec, slot_spec, table_spec,
                  pl.BlockSpec((PEER_SLOTS, GATHER_ROWS), const2),
                  pl.BlockSpec((GATHER_ROWS, PEER_SLOTS), const2)],
        out_specs=slot_spec,
        out_shape=jax.ShapeDtypeStruct((n, PEER_SLOTS), F32),
        scratch_shapes=[pltpu.VMEM((PEER_SLOTS * SUBLANES, LANES), jnp.int32),
                        pltpu.VMEM((tt, GATHER_ROWS), F32)],
        compiler_params=params,
        interpret=interpret,
        name="peer_down",
    )(rows, xs, hs, gate, tab_u, e, sel)

    y = pl.pallas_call(
        _peer_up_kernel,
        grid=(n // tt,),
        in_specs=[smem_idx, slot_spec, slot_spec, table_spec,
                  pl.BlockSpec((PEER_SLOTS, GATHER_ROWS), const2)],
        out_specs=pl.BlockSpec((tt, SUBLANES, LANES), lambda i: (i, 0, 0)),
        out_shape=jax.ShapeDtypeStruct((n, SUBLANES, LANES), F32),
        scratch_shapes=[pltpu.VMEM((PEER_SLOTS * SUBLANES, LANES), jnp.int32),
                        pltpu.VMEM((tt, GATHER_ROWS), F32),
                        pltpu.VMEM((tt, GATHER_ROWS), F32)],
        compiler_params=params,
        interpret=interpret,
        name="peer_up",
    )(rows, w, hs, tab_v, e)
    return y.reshape(n, D_MODEL)


def _ada_rmsnorm(x, gain, shift, scale):
    xf = x * lax.rsqrt(jnp.mean(x * x, axis=-1, keepdims=True) + EPS) * gain
    return xf * (1.0 + scale[:, None, :]) + shift[:, None, :]


def _head_rms(t, gain):
    return t * lax.rsqrt(jnp.mean(t * t, axis=-1, keepdims=True) + EPS) * gain


def _softcap(z):
    return GATE_SOFTCAP * jnp.tanh(z / GATE_SOFTCAP)


def _dot3(a_hi, a_lo, b_hi, b_lo, dims):
    dot = functools.partial(lax.dot_general, dimension_numbers=(dims, ((), ())), preferred_element_type=F32)
    return dot(a_hi, b_hi) + (dot(a_hi, b_lo) + dot(a_lo, b_hi))


def _top_rows(s, k):
    nrows = s.shape[0]
    row = lax.broadcasted_iota(jnp.int32, s.shape, 0)
    vals, rows = [], []
    for _ in range(k):
        m = jnp.max(s, axis=0, keepdims=True)
        r = jnp.min(jnp.where(s == m, row, nrows), axis=0, keepdims=True)
        vals.append(m)
        rows.append(r)
        s = jnp.where(row == r, -jnp.inf, s)
    return jnp.concatenate(vals, axis=0), jnp.concatenate(rows, axis=0)


def _tree(op, xs):
    while len(xs) > 1:
        xs = [op(xs[i], xs[i + 1]) if i + 1 < len(xs) else xs[i] for i in range(0, len(xs), 2)]
    return xs[0]


def _product_candidates(s0, s1, i0, i1):
    k = PEER_TOPK
    t = s0.shape[1]
    sub = lax.broadcasted_iota(jnp.int32, (SUBLANES, t), 0)
    blocks = []
    for half in range(k // SUBLANES):
        b = sub + half * SUBLANES
        lo = half * SUBLANES
        blocks.append((s0[0:1] + s1[lo:lo + SUBLANES], b, i0[0:1] * PEER_KEYS + i1[lo:lo + SUBLANES]))
    for a in range(1, SUBLANES):
        valid = (a + 1) * (sub + 1) <= k
        blocks.append((jnp.where(valid, s0[a:a + 1] + s1[0:SUBLANES], -jnp.inf), a * k + sub,
                       i0[a:a + 1] * PEER_KEYS + i1[0:SUBLANES]))
    a = sub + SUBLANES
    blocks.append((s0[SUBLANES:k] + s1[0:1], a * k, i0[SUBLANES:k] * PEER_KEYS + i1[0:1]))
    return blocks


def _peer_route_kernel(x_ref, shift_ref, scale_ref, gain_ref, wqh_ref, wql_ref, skh_ref, skl_ref,
                       rows_ref, hs_ref, gate_ref, xs_ref, q_ref, tv_ref, ti_ref, bs_ref, be_ref):
    tt = x_ref.shape[0]
    k = PEER_TOPK
    x = x_ref[...]
    xn = x * lax.rsqrt(jnp.mean(x * x, axis=-1, keepdims=True) + EPS) * gain_ref[...]
    xn = xn * (1.0 + scale_ref[0]) + shift_ref[0]
    x_hi, x_lo = _split_hi_lo(xn)
    xs_ref[:, :D_MODEL] = x_hi
    xs_ref[:, D_MODEL:] = x_lo
    q = _dot3(x_hi, x_lo, wqh_ref[...], wql_ref[...], ((1,), (0,)))
    for hp in range(2 * PEER_HEADS):
        q_ref[hp] = q[:, hp * PEER_HALF:(hp + 1) * PEER_HALF]

    def sub_topk(hp, carry):
        q_hi, q_lo = _split_hi_lo(q_ref[hp])
        p = hp % 2
        s = _dot3(skh_ref[p], skl_ref[p], q_hi, q_lo, ((1,), (1,)))
        tv_ref[hp], ti_ref[hp] = _top_rows(s, k)
        return carry

    lax.fori_loop(0, 2 * PEER_HEADS, sub_topk, 0)

    def head_topk(h, carry):
        blocks = _product_candidates(tv_ref[2 * h], tv_ref[2 * h + 1], ti_ref[2 * h], ti_ref[2 * h + 1])
        sums = [b[0] for b in blocks]
        best_s, best_e = [], []
        for _ in range(k):
            m = jnp.max(_tree(jnp.maximum, sums), axis=0, keepdims=True)
            pos = jnp.min(_tree(jnp.minimum, [jnp.where(c == m, b[1], k * k) for c, b in zip(sums, blocks)]),
                          axis=0, keepdims=True)
            hit = [b[1] == pos for b in blocks]
            e = jnp.max(_tree(jnp.maximum, [jnp.where(hh, b[2], -1) for hh, b in zip(hit, blocks)]),
                        axis=0, keepdims=True)
            sums = [jnp.where(hh, -jnp.inf, c) for hh, c in zip(hit, sums)]
            best_s.append(m)
            best_e.append(e)
        bs = jnp.concatenate(best_s, axis=0)
        ex = jnp.exp(bs - bs[0:1])
        off = pl.multiple_of(h * k, k)
        bs_ref[pl.ds(off, k), :] = ex / jnp.sum(ex, axis=0, keepdims=True)
        be_ref[pl.ds(off, k), :] = jnp.concatenate(best_e, axis=0).astype(F32)
        return carry

    lax.fori_loop(0, PEER_HEADS, head_topk, 0)
    gate_ref[...] = bs_ref[...].T
    ids = be_ref[...].T.astype(jnp.int32)
    rows_ref[...] = ids % HALF_EXPERTS
    hs_ref[...] = (ids // HALF_EXPERTS).astype(F32)


ROUTE_TOKEN_TILE = 256
ROUTE_VMEM_LIMIT = 40 * 1024 * 1024


def _peer_route(x2, shift, scale, gain, wq, subkeys, seq_len, *, token_tile=ROUTE_TOKEN_TILE, interpret=False):
    n = x2.shape[0]
    tt = token_tile
    assert seq_len % tt == 0 and n % seq_len == 0
    tiles_per_seq = seq_len // tt
    batch = n // seq_len
    wq_hi, wq_lo = _split_hi_lo(wq)
    sk_hi, sk_lo = _split_hi_lo(subkeys)
    tok = lambda i: (i, 0)
    per_seq = pl.BlockSpec((1, 1, D_MODEL), lambda i: (i // tiles_per_seq, 0, 0))
    whole = pl.BlockSpec(memory_space=pltpu.VMEM)
    slot_spec = pl.BlockSpec((tt, PEER_SLOTS), tok)
    slot_shape = lambda dt: jax.ShapeDtypeStruct((n, PEER_SLOTS), dt)
    return pl.pallas_call(
        _peer_route_kernel,
        grid=(n // tt,),
        in_specs=[pl.BlockSpec((tt, D_MODEL), tok), per_seq, per_seq,
                  pl.BlockSpec((1, D_MODEL), lambda i: (0, 0)), whole, whole, whole, whole],
        out_specs=[slot_spec, slot_spec, slot_spec, pl.BlockSpec((tt, 2 * D_MODEL), tok)],
        out_shape=[slot_shape(jnp.int32), slot_shape(F32), slot_shape(F32),
                   jax.ShapeDtypeStruct((n, 2 * D_MODEL), BF16)],
        scratch_shapes=[pltpu.VMEM((2 * PEER_HEADS, tt, PEER_HALF), F32),
                        pltpu.VMEM((2 * PEER_HEADS, PEER_TOPK, tt), F32),
                        pltpu.VMEM((2 * PEER_HEADS, PEER_TOPK, tt), jnp.int32),
                        pltpu.VMEM((PEER_SLOTS, tt), F32),
                        pltpu.VMEM((PEER_SLOTS, tt), F32)],
        compiler_params=pltpu.CompilerParams(dimension_semantics=("arbitrary",),
                                             vmem_limit_bytes=ROUTE_VMEM_LIMIT),
        interpret=interpret,
        name="peer_route",
    )(x2, shift.reshape(batch, 1, D_MODEL), scale.reshape(batch, 1, D_MODEL), gain.reshape(1, D_MODEL),
      wq_hi, wq_lo, sk_hi, sk_lo)


def _peer(x, gain, shift, scale, wq, subkeys, tab_u, tab_v):
    B, S, D = x.shape
    rows, hs, gate, xs = _peer_route(x.reshape(B * S, D), shift, scale, gain, wq, subkeys, S)
    y = _peer_experts(xs.reshape(B * S, SUBROWS, LANES), rows, hs, gate, tab_u, tab_v)
    return y.reshape(B, S, D)


PROJ_TOKEN_TILE = 512
PROJ_VMEM_LIMIT = 48 * 1024 * 1024


def _ada_norm_tile(x, gain, shift, scale):
    xn = x * lax.rsqrt(jnp.mean(x * x, axis=-1, keepdims=True) + EPS) * gain
    return xn * (1.0 + scale) + shift


def _norm_proj_kernel(x_ref, shift_ref, scale_ref, gain_ref, w_ref, *rest, with_gate):
    x_hi, x_lo = _split_hi_lo(_ada_norm_tile(x_ref[...], gain_ref[...], shift_ref[0], scale_ref[0]))
    if with_gate:
        wgh_ref, wgl_ref, main_ref, gate_ref = rest
        gate_ref[...] = _dot3(x_hi, x_lo, wgh_ref[...], wgl_ref[...], ((1,), (0,)))
    else:
        (main_ref,) = rest
    main_ref[...] = jnp.dot(x_hi, w_ref[...], preferred_element_type=F32)


def _norm_proj(x2, shift, scale, gain, w_main, w_gate, seq_len, *, token_tile=PROJ_TOKEN_TILE, interpret=False):
    n = x2.shape[0]
    tt = token_tile
    assert seq_len % tt == 0 and n % seq_len == 0
    tiles_per_seq = seq_len // tt
    batch = n // seq_len
    m = w_main.shape[1]
    tok = lambda i: (i, 0)
    per_seq = pl.BlockSpec((1, 1, D_MODEL), lambda i: (i // tiles_per_seq, 0, 0))
    whole = pl.BlockSpec(memory_space=pltpu.VMEM)
    in_specs = [pl.BlockSpec((tt, D_MODEL), tok), per_seq, per_seq,
                pl.BlockSpec((1, D_MODEL), lambda i: (0, 0)), whole]
    args = [x2, shift.reshape(batch, 1, D_MODEL), scale.reshape(batch, 1, D_MODEL), gain.reshape(1, D_MODEL),
            w_main.astype(BF16)]
    out_specs = [pl.BlockSpec((tt, m), tok)]
    out_shape = [jax.ShapeDtypeStruct((n, m), F32)]
    if w_gate is not None:
        wg_hi, wg_lo = _split_hi_lo(jnp.pad(w_gate, ((0, 0), (0, LANES - w_gate.shape[1]))))
        in_specs += [whole, whole]
        args += [wg_hi, wg_lo]
        out_specs.append(pl.BlockSpec((tt, LANES), tok))
        out_shape.append(jax.ShapeDtypeStruct((n, LANES), F32))
    outs = pl.pallas_call(
        functools.partial(_norm_proj_kernel, with_gate=w_gate is not None),
        grid=(n // tt,), in_specs=in_specs, out_specs=out_specs, out_shape=out_shape,
        compiler_params=pltpu.CompilerParams(dimension_semantics=("arbitrary",),
                                             vmem_limit_bytes=PROJ_VMEM_LIMIT),
        interpret=interpret, name="norm_proj",
    )(*args)
    return (outs[0], outs[1]) if w_gate is not None else (outs[0], None)


def _out_proj_kernel(h_ref, og_ref, x_ref, g_ref, hg_ref, w_ref, o_ref, *, head_dim):
    h = h_ref[...]
    if head_dim is not None:
        parts = []
        for j in range(D_MODEL // head_dim):
            hb = h[:, j * head_dim:(j + 1) * head_dim]
            parts.append(hb * lax.rsqrt(jnp.mean(hb * hb, axis=-1, keepdims=True) + EPS))
        h = jnp.concatenate(parts, axis=1) * hg_ref[...]
    a = jax.nn.sigmoid(og_ref[...]) * h
    y = jnp.dot(a.astype(BF16), w_ref[...], preferred_element_type=F32)
    o_ref[...] = x_ref[...] + g_ref[0] * y


def _out_proj(h2, p, og_block, x2, g, h_gain, w_out, seq_len, head_dim, *, token_tile=PROJ_TOKEN_TILE,
              interpret=False):
    n = x2.shape[0]
    tt = token_tile
    tiles_per_seq = seq_len // tt
    batch = n // seq_len
    tok = lambda i: (i, 0)
    row = pl.BlockSpec((tt, D_MODEL), tok)
    gain = jnp.ones((1, D_MODEL), F32) if h_gain is None else h_gain.reshape(1, D_MODEL)
    return pl.pallas_call(
        functools.partial(_out_proj_kernel, head_dim=head_dim),
        grid=(n // tt,),
        in_specs=[row, pl.BlockSpec((tt, D_MODEL), lambda i: (i, og_block)), row,
                  pl.BlockSpec((1, 1, D_MODEL), lambda i: (i // tiles_per_seq, 0, 0)),
                  pl.BlockSpec((1, D_MODEL), lambda i: (0, 0)),
                  pl.BlockSpec(memory_space=pltpu.VMEM)],
        out_specs=row,
        out_shape=jax.ShapeDtypeStruct((n, D_MODEL), F32),
        compiler_params=pltpu.CompilerParams(dimension_semantics=("arbitrary",),
                                             vmem_limit_bytes=PROJ_VMEM_LIMIT),
        interpret=interpret, name="out_proj",
    )(h2, p, x2, g.reshape(batch, 1, D_MODEL), gain, w_out.astype(BF16))


MLSTM_STATE_W = MLSTM_V_DIM + LANES


def _mlstm_kernel(q_ref, k_ref, v_ref, gc_ref, gr_ref, h_ref, cn_ref, m_ref):
    H, DQK, DV, L = MLSTM_HEADS, MLSTM_QK_DIM, MLSTM_V_DIM, MLSTM_CHUNK

    @pl.when(pl.program_id(1) == 0)
    def _():
        cn_ref[...] = jnp.zeros_like(cn_ref)
        m_ref[...] = jnp.zeros_like(m_ref)

    tril = lax.broadcasted_iota(jnp.int32, (L, L), 1) <= lax.broadcasted_iota(jnp.int32, (L, L), 0)
    gc = gc_ref[...]
    gr = gr_ref[0]
    ones = jnp.ones((L, LANES), F32)
    for h in range(H):
        ig_col, b_col = gc[:, h:h + 1], gc[:, H + h:H + h + 1]
        ig_row, b_row = gr[h:h + 1, :], gr[H + h:H + h + 1, :]
        m_prev = m_ref[h:h + 1, 0:1]
        d_log = jnp.where(tril, b_col - b_row + ig_row, -jnp.inf)
        inter = b_col + m_prev
        m_t = jnp.maximum(inter, jnp.max(d_log, axis=1, keepdims=True))
        w = jnp.exp(d_log - m_t)
        a_inter = jnp.exp(inter - m_t)
        qh = (q_ref[:, h * DQK:(h + 1) * DQK] * (DQK ** -0.5)).astype(BF16)
        kf = k_ref[:, h * DQK:(h + 1) * DQK]
        vh = v_ref[:, h * DV:(h + 1) * DV]
        s = lax.dot_general(qh, kf.astype(BF16), (((1,), (1,)), ((), ())), preferred_element_type=F32) * w
        cn = cn_ref[h]
        qc = jnp.dot(qh, cn.astype(BF16), preferred_element_type=F32)
        num = a_inter * qc[:, :DV] + jnp.dot(s.astype(BF16), vh.astype(BF16), preferred_element_type=F32)
        den = a_inter * qc[:, DV:DV + 1] + jnp.sum(s, axis=1, keepdims=True)
        h_ref[:, h * DV:(h + 1) * DV] = num / jnp.maximum(jnp.abs(den), jnp.exp(-m_t))
        b_last = b_col[L - 1:L, :]
        g_col = b_last - b_col + ig_col
        m_new = jnp.maximum(b_last + m_prev, jnp.max(g_col, axis=0, keepdims=True))
        w_s = jnp.exp(g_col - m_new)
        decay = jnp.exp(b_last + m_prev - m_new)
        kw = (kf * w_s).astype(BF16)
        vaug = jnp.concatenate([vh, ones], axis=1).astype(BF16)
        cn_ref[h] = decay * cn + lax.dot_general(kw, vaug, (((0,), (0,)), ((), ())), preferred_element_type=F32)
        m_ref[h:h + 1, :] = jnp.broadcast_to(m_new, (1, LANES))


def _mlstm_scan(p, gates, b_i, b_f, batch, seq_len, *, interpret=False):
    H, L = MLSTM_HEADS, MLSTM_CHUNK
    n = p.shape[0]
    nc = seq_len // L
    ig = _softcap(gates[:, :H] + b_i)
    lf = jax.nn.log_sigmoid(_softcap(gates[:, H:2 * H] + b_f))
    b = jnp.cumsum(lf.reshape(n // L, L, H), axis=1).reshape(n, H)
    gc = jnp.concatenate([ig, b], axis=1)
    gr = gc.reshape(n // L, L, 2 * H).transpose(0, 2, 1)
    chunk = lambda bi, c: bi * nc + c
    return pl.pallas_call(
        _mlstm_kernel,
        grid=(batch, nc),
        in_specs=[pl.BlockSpec((L, MLSTM_QK_W), lambda bi, c: (chunk(bi, c), 0)),
                  pl.BlockSpec((L, MLSTM_QK_W), lambda bi, c: (chunk(bi, c), 1)),
                  pl.BlockSpec((L, MLSTM_V_W), lambda bi, c: (chunk(bi, c), 1)),
                  pl.BlockSpec((L, 2 * H), lambda bi, c: (chunk(bi, c), 0)),
                  pl.BlockSpec((1, 2 * H, L), lambda bi, c: (chunk(bi, c), 0, 0))],
        out_specs=pl.BlockSpec((L, D_MODEL), lambda bi, c: (chunk(bi, c), 0)),
        out_shape=jax.ShapeDtypeStruct((n, D_MODEL), F32),
        scratch_shapes=[pltpu.VMEM((H, MLSTM_QK_DIM, MLSTM_STATE_W), F32),
                        pltpu.VMEM((SUBLANES, LANES), F32)],
        compiler_params=pltpu.CompilerParams(dimension_semantics=("arbitrary", "arbitrary")),
        interpret=interpret, name="mlstm_scan",
    )(p, p, p, gc, gr)


FOX_BLOCK = 512
FOX_QUERY_PART = 256
FOX_SUM_ROWS = 16
FOX_VMEM_LIMIT = 48 * 1024 * 1024


def _bias_columns(f, query_side):
    hi = f.astype(BF16).astype(F32)
    r1 = f - hi
    lo = r1.astype(BF16).astype(F32)
    lo2 = r1 - lo
    lane = lax.broadcasted_iota(jnp.int32, (f.shape[0], LANES), 1)
    if query_side:
        vals = jnp.where(lane == 0, hi, jnp.where(lane == 1, lo, jnp.where(lane == 2, lo2,
                         jnp.where(lane < 6, 1.0, 0.0))))
    else:
        vals = jnp.where(lane < 3, 1.0, jnp.where(lane == 3, -hi, jnp.where(lane == 4, -lo,
                         jnp.where(lane == 5, -lo2, 0.0))))
    return vals.astype(BF16)


def _head_column(fc, h):
    lane = lax.broadcasted_iota(jnp.int32, fc.shape, 1)
    return jnp.sum(jnp.where(lane == h, fc, 0.0), axis=1, keepdims=True)


def _rms_rows(t, gain):
    return t * lax.rsqrt(jnp.mean(t * t, axis=-1, keepdims=True) + EPS) * gain


def _fox_attn_kernel(q_ref, k_ref, v_ref, fq_ref, fk_ref, qg_ref, kg_ref, o_ref,
                     ka_ref, vt_ref, qa_ref, st_cur_ref, st_next_ref, *state_refs):
    i = pl.program_id(1)
    h = pl.program_id(0) % FOX_HEADS
    tq = q_ref.shape[0]
    nk = k_ref.shape[0] // tq
    hd = FOX_HEAD_DIM
    part = FOX_QUERY_PART
    n_parts = tq // part
    m_refs, acc_refs = state_refs[:n_parts], state_refs[n_parts:]

    @pl.when(i == 0)
    def _():
        def prep(c, carry):
            r = pl.ds(pl.multiple_of(c * tq, tq), tq)
            ka_ref[r, :hd] = _rms_rows(k_ref[r, :], kg_ref[...]).astype(BF16)
            ka_ref[r, hd:] = _bias_columns(_head_column(fk_ref[r, :], h), False)
            vt_ref[c, :hd, :] = v_ref[r, :].T.astype(BF16)
            vt_ref[c, hd:, :] = jnp.ones((FOX_SUM_ROWS, tq), BF16)
            return carry
        lax.fori_loop(0, nk, prep, 0)

    qn = _rms_rows(q_ref[...], qg_ref[...]) * (hd ** -0.5)
    qa_ref[:, :hd] = qn.astype(BF16)
    qa_ref[:, hd:] = _bias_columns(_head_column(fq_ref[...], h), True)
    for m_ref, acc_ref in zip(m_refs, acc_refs):
        m_ref[...] = jnp.full_like(m_ref, -jnp.inf)
        acc_ref[...] = jnp.zeros_like(acc_ref)

    def scores(j, st_ref):
        kblk = ka_ref[pl.ds(pl.multiple_of(j * tq, tq), tq), :]
        for c in range(n_parts):
            st_ref[c] = lax.dot_general(kblk, qa_ref[c * part:(c + 1) * part, :],
                                        (((1,), (1,)), ((), ())), preferred_element_type=F32)

    def softmax_pv(j, masked):
        vtblk = vt_ref[j]
        pts, alphas = [], []
        for c, m_ref in enumerate(m_refs):
            st = st_cur_ref[c]
            if masked:
                key = lax.broadcasted_iota(jnp.int32, st.shape, 0)
                qry = lax.broadcasted_iota(jnp.int32, st.shape, 1) + c * part
                st = jnp.where(key <= qry, st, -jnp.inf)
            m_prev = m_ref[0:1, :]
            m_new = jnp.maximum(m_prev, jnp.max(st, axis=0, keepdims=True))
            alphas.append(jnp.exp(m_prev - m_new))
            pts.append(jnp.exp(st - m_new).astype(BF16))
            m_ref[0:1, :] = m_new
        for pt, alpha, acc_ref in zip(pts, alphas, acc_refs):
            acc_ref[...] = alpha * acc_ref[...] + jnp.dot(vtblk, pt, preferred_element_type=F32)

    def body(j, carry):
        scores(j + 1, st_next_ref)
        softmax_pv(j, False)
        st_cur_ref[...] = st_next_ref[...]
        return carry

    scores(0, st_cur_ref)
    lax.fori_loop(0, i, body, 0)
    softmax_pv(i, True)
    for c, acc_ref in enumerate(acc_refs):
        acc = acc_ref[...]
        o_ref[c * part:(c + 1) * part, :] = (acc[:hd] / acc[hd:hd + 1]).T


def _fox_attention(pq, kv, f_cum, q_gain, k_gain, batch, seq_len, *, block=FOX_BLOCK, interpret=False):
    n = pq.shape[0]
    hd, nh = FOX_HEAD_DIM, FOX_HEADS
    nq = seq_len // block
    return pl.pallas_call(
        _fox_attn_kernel,
        grid=(batch * nh, nq),
        in_specs=[pl.BlockSpec((block, hd), lambda bh, i: ((bh // nh) * nq + i, bh % nh)),
                  pl.BlockSpec((seq_len, hd), lambda bh, i: (bh // nh, bh % nh)),
                  pl.BlockSpec((seq_len, hd), lambda bh, i: (bh // nh, nh + bh % nh)),
                  pl.BlockSpec((block, nh), lambda bh, i: ((bh // nh) * nq + i, 0)),
                  pl.BlockSpec((seq_len, nh), lambda bh, i: (bh // nh, 0)),
                  pl.BlockSpec((1, hd), lambda bh, i: (0, 0)),
                  pl.BlockSpec((1, hd), lambda bh, i: (0, 0))],
        out_specs=pl.BlockSpec((block, hd), lambda bh, i: ((bh // nh) * nq + i, bh % nh)),
        out_shape=jax.ShapeDtypeStruct((n, D_MODEL), F32),
        scratch_shapes=[pltpu.VMEM((seq_len, 2 * hd), BF16),
                        pltpu.VMEM((nq, hd + FOX_SUM_ROWS, block), BF16),
                        pltpu.VMEM((block, 2 * hd), BF16),
                        *[pltpu.VMEM((block // FOX_QUERY_PART, block, FOX_QUERY_PART), F32)] * 2,
                        *[pltpu.VMEM((SUBLANES, FOX_QUERY_PART), F32)] * (block // FOX_QUERY_PART),
                        *[pltpu.VMEM((hd + FOX_SUM_ROWS, FOX_QUERY_PART), F32)] * (block // FOX_QUERY_PART)],
        compiler_params=pltpu.CompilerParams(dimension_semantics=("arbitrary", "arbitrary"),
                                             vmem_limit_bytes=FOX_VMEM_LIMIT),
        interpret=interpret, name="fox_attention",
    )(pq, kv, kv, f_cum, f_cum, q_gain.reshape(1, hd), k_gain.reshape(1, hd))


def kernel(x, c, ada_w, ada_b, mix_norm, ffn_norm, a_w_in, a_b_i, a_b_f, a_h_norm, a_w_out,
           kv_ada_w, kv_ada_b, kv_norm, kv_w, kv_b_f, kv_k_norm, b_w_qo, b_q_norm, b_w_out,
           peer_wq, peer_subkeys, peer_u, peer_v):
    B, S, D = x.shape
    n = B * S
    cs = jax.nn.silu(c)
    x2 = x.reshape(n, D)
    kv = f_cum = None
    for l in range(DEPTH):
        mod = cs @ ada_w[l] + ada_b[l]
        sh1, sc1, g1, sh2, sc2, g2 = jnp.split(mod, N_ADA, axis=-1)
        if l < N_A_LAYERS:
            split = 2 * MLSTM_QK_W + MLSTM_V_W + D_MODEL
            p, gates = _norm_proj(x2, sh1, sc1, mix_norm[l], a_w_in[l][:, :split], a_w_in[l][:, split:], S)
            h = _mlstm_scan(p, gates, a_b_i[l], a_b_f[l], B, S)
            x2 = _out_proj(h, p, 2, x2, g1, a_h_norm[l], a_w_out[l], S, MLSTM_V_DIM)
        else:
            j = l - N_A_LAYERS
            pq, _ = _norm_proj(x2, sh1, sc1, mix_norm[l], b_w_qo[j], None, S)
            att = _fox_attention(pq, kv, f_cum, b_q_norm[j], kv_k_norm, B, S)
            x2 = _out_proj(att, pq, 1, x2, g1, None, b_w_out[j], S, None)
        y = _peer(x2.reshape(B, S, D), ffn_norm[l], sh2, sc2, peer_wq[l], peer_subkeys[l],
                  _pack_expert_table(peer_u[l]), _pack_expert_table(peer_v[l]))
        x2 = x2 + (g2[:, None, :] * y).reshape(n, D)
        if l == N_A_LAYERS - 1:
            sh, sc = jnp.split(cs @ kv_ada_w + kv_ada_b, 2, axis=-1)
            kv, fg = _norm_proj(x2, sh, sc, kv_norm, kv_w[:, :2 * D], kv_w[:, 2 * D:], S)
            log_f = jax.nn.log_sigmoid(fg[:, :FOX_HEADS] + kv_b_f)
            f_cum = jnp.cumsum(log_f.reshape(B, S, FOX_HEADS), axis=1).reshape(n, FOX_HEADS)
    return x2.reshape(B, S, D)
```

```python
import functools
import math

import jax
import jax.numpy as jnp
from jax import lax
from jax.experimental import pallas as pl
from jax.experimental.pallas import tpu as pltpu

F32 = jnp.float32
BF16 = jnp.bfloat16

D_MODEL = 1024
DEPTH = 2
N_A_LAYERS = DEPTH // 2
EPS = 1e-6
N_ADA = 6

MLSTM_HEADS = 4
MLSTM_QK_DIM = D_MODEL // (2 * MLSTM_HEADS)
MLSTM_V_DIM = D_MODEL // MLSTM_HEADS
MLSTM_CHUNK = 64
GATE_SOFTCAP = 15.0
MLSTM_QK_W = MLSTM_HEADS * MLSTM_QK_DIM
MLSTM_V_W = MLSTM_HEADS * MLSTM_V_DIM

FOX_HEADS = 8
FOX_HEAD_DIM = D_MODEL // FOX_HEADS
Q_BLOCK = 128

PEER_HEADS = 8
PEER_KEYS = 128
PEER_EXPERTS = PEER_KEYS * PEER_KEYS
PEER_QUERY_DIM = 256
PEER_HALF = PEER_QUERY_DIM // 2
PEER_TOPK = 16

SUBLANES = 8
LANES = 128
ROW_WORDS = SUBLANES * LANES
assert ROW_WORDS == D_MODEL
PEER_SLOTS = PEER_HEADS * PEER_TOPK
HALF_EXPERTS = PEER_EXPERTS // 2
SUBROWS = 2 * SUBLANES
GATHER_ROWS = PEER_SLOTS * SUBROWS
TOKEN_GROUP = 2 * SUBLANES
PEER_TOKEN_TILE = 128
PEER_VMEM_LIMIT = 48 * 1024 * 1024


def _pack_expert_table(t):
    lo = lax.bitcast_convert_type(t[:HALF_EXPERTS].astype(BF16), jnp.uint16).astype(jnp.uint32)
    hi = lax.bitcast_convert_type(t[HALF_EXPERTS:].astype(BF16), jnp.uint16).astype(jnp.uint32)
    w = lo | (hi << 16)
    return lax.bitcast_convert_type(w, jnp.int32).reshape(HALF_EXPERTS, SUBLANES, LANES)


def _slot_expand_matrix():
    slot = lax.broadcasted_iota(jnp.int32, (PEER_SLOTS, GATHER_ROWS), 0)
    sub = lax.broadcasted_iota(jnp.int32, (PEER_SLOTS, GATHER_ROWS), 1)
    return (sub // SUBROWS == slot).astype(BF16)


def _diag_mask():
    r = lax.broadcasted_iota(jnp.int32, (SUBLANES, GATHER_ROWS), 0)
    sub = lax.broadcasted_iota(jnp.int32, (SUBLANES, GATHER_ROWS), 1)
    return (sub % SUBROWS) // 2 == r


def _half_mask(hs, e):
    hsx = jnp.dot(hs.astype(BF16), e, preferred_element_type=F32)
    par = (lax.broadcasted_iota(jnp.int32, hsx.shape, 1) % 2).astype(F32)
    return hsx == par


def _gather_token(idx_ref, tab_ref, gb_ref, t):
    for s in range(PEER_SLOTS // 2):
        w = idx_ref[t, s]
        a = pl.multiple_of(w & 0xFFFF, SUBLANES)
        b = pl.multiple_of(lax.shift_right_logical(w, 16), SUBLANES)
        gb_ref[pl.ds(2 * s * SUBLANES, SUBLANES), :] = tab_ref[pl.ds(a, SUBLANES), :]
        gb_ref[pl.ds((2 * s + 1) * SUBLANES, SUBLANES), :] = tab_ref[pl.ds(b, SUBLANES), :]
    return pltpu.bitcast(gb_ref[...], BF16)


def _split_hi_lo(a):
    hi = a.astype(BF16)
    lo = (a - hi.astype(F32)).astype(BF16)
    return hi, lo


def _peer_down_kernel(idx_ref, xs_ref, hs_ref, gate_ref, tab_ref, e_ref, sel_ref, w_ref, gb_ref, z_ref):
    tt = w_ref.shape[0]
    diag = _diag_mask()

    def group(g, carry):
        rows = []
        for i in range(TOKEN_GROUP):
            t = g * TOKEN_GROUP + i
            gath = _gather_token(idx_ref, tab_ref, gb_ref, t)
            y = lax.dot_general(xs_ref[t], gath, (((1,), (1,)), ((), ())),
                                preferred_element_type=F32)
            y8 = y[:SUBLANES] + y[SUBLANES:]
            rows.append(jnp.sum(jnp.where(diag, y8, 0.0), axis=0, keepdims=True))
        z_ref[pl.ds(pl.multiple_of(g * TOKEN_GROUP, TOKEN_GROUP), TOKEN_GROUP), :] = jnp.concatenate(rows, axis=0)
        return carry

    lax.fori_loop(0, tt // TOKEN_GROUP, group, 0)
    zm = jnp.where(_half_mask(hs_ref[...], e_ref[...]), z_ref[...], 0.0)
    act = jnp.dot(zm, sel_ref[...], precision=lax.Precision.HIGHEST, preferred_element_type=F32)
    gelu = 0.5 * act * (1.0 + lax.erf(act * (1.0 / math.sqrt(2.0))))
    w_ref[...] = gate_ref[...] * gelu


def _peer_up_kernel(idx_ref, w_ref, hs_ref, tab_ref, e_ref, y_ref, gb_ref, ahi_ref, alo_ref):
    tt = w_ref.shape[0]
    diag = _diag_mask()
    e = e_ref[...]
    hm = _half_mask(hs_ref[...], e)
    w_hi, w_lo = _split_hi_lo(w_ref[...])
    ahi_ref[...] = jnp.where(hm, jnp.dot(w_hi, e, preferred_element_type=F32), 0.0)
    alo_ref[...] = jnp.where(hm, jnp.dot(w_lo, e, preferred_element_type=F32), 0.0)

    def group(g, carry):
        base = pl.multiple_of(g * TOKEN_GROUP, TOKEN_GROUP)
        a_hi = ahi_ref[pl.ds(base, TOKEN_GROUP), :]
        a_lo = alo_ref[pl.ds(base, TOKEN_GROUP), :]
        for i in range(TOKEN_GROUP):
            t = g * TOKEN_GROUP + i
            gath = _gather_token(idx_ref, tab_ref, gb_ref, t)
            lhs = jnp.concatenate(
                [jnp.where(diag, a_hi[i:i + 1, :], 0.0), jnp.where(diag, a_lo[i:i + 1, :], 0.0)],
                axis=0).astype(BF16)
            out = jnp.dot(lhs, gath, preferred_element_type=F32)
            y_ref[t] = out[:SUBLANES] + out[SUBLANES:]
        return carry

    lax.fori_loop(0, tt // TOKEN_GROUP, group, 0)


def _peer_experts(xs, rows, hs, gate, tab_u, tab_v, *, token_tile=PEER_TOKEN_TILE, interpret=False):
    n = xs.shape[0]
    tt = token_tile
    assert n % tt == 0 and tt % TOKEN_GROUP == 0
    e = _slot_expand_matrix()
    sel = e.T.astype(F32)

    tok = lambda i: (i, 0)
    const2 = lambda i: (0, 0)
    smem_idx = pl.BlockSpec((tt, PEER_SLOTS // 2), tok, memory_space=pltpu.SMEM)
    tab_u = tab_u.reshape(HALF_EXPERTS * SUBLANES, LANES)
    tab_v = tab_v.reshape(HALF_EXPERTS * SUBLANES, LANES)
    slot_spec = pl.BlockSpec((tt, PEER_SLOTS), tok)
    table_spec = pl.BlockSpec(memory_space=pltpu.VMEM)
    params = pltpu.CompilerParams(dimension_semantics=("arbitrary",), vmem_limit_bytes=PEER_VMEM_LIMIT)

    w = pl.pallas_call(
        _peer_down_kernel,
        grid=(n // tt,),
        in_specs=[smem_idx,
                  pl.BlockSpec((tt, SUBROWS, LANES), lambda i: (i, 0, 0)),
                  slot_spec, slot_spec, table_spec,
                  pl.BlockSpec((PEER_SLOTS, GATHER_ROWS), const2),
                  pl.BlockSpec((GATHER_ROWS, PEER_SLOTS), const2)],
        out_specs=slot_spec,
        out_shape=jax.ShapeDtypeStruct((n, PEER_SLOTS), F32),
        scratch_shapes=[pltpu.VMEM((PEER_SLOTS * SUBLANES, LANES), jnp.int32),
                        pltpu.VMEM((tt, GATHER_ROWS), F32)],
        compiler_params=params,
        interpret=interpret,
        name="peer_down",
    )(rows, xs, hs, gate, tab_u, e, sel)

    y = pl.pallas_call(
        _peer_up_kernel,
        grid=(n // tt,),
        in_specs=[smem_idx, slot_spec, slot_spec, table_spec,
                  pl.BlockSpec((PEER_SLOTS, GATHER_ROWS), const2)],
        out_specs=pl.BlockSpec((tt, SUBLANES, LANES), lambda i: (i, 0, 0)),
        out_shape=jax.ShapeDtypeStruct((n, SUBLANES, LANES), F32),
        scratch_shapes=[pltpu.VMEM((PEER_SLOTS * SUBLANES, LANES), jnp.int32),
                        pltpu.VMEM((tt, GATHER_ROWS), F32),
                        pltpu.VMEM((tt, GATHER_ROWS), F32)],
        compiler_params=params,
        interpret=interpret,
        name="peer_up",
    )(rows, w, hs, tab_v, e)
    return y.reshape(n, D_MODEL)


def _ada_rmsnorm(x, gain, shift, scale):
    xf = x * lax.rsqrt(jnp.mean(x * x, axis=-1, keepdims=True) + EPS) * gain
    return xf * (1.0 + scale[:, None, :]) + shift[:, None, :]


def _head_rms(t, gain):
    return t * lax.rsqrt(jnp.mean(t * t, axis=-1, keepdims=True) + EPS) * gain


def _softcap(z):
    return GATE_SOFTCAP * jnp.tanh(z / GATE_SOFTCAP)


def _dot3(a_hi, a_lo, b_hi, b_lo, dims):
    dot = functools.partial(lax.dot_general, dimension_numbers=(dims, ((), ())), preferred_element_type=F32)
    return dot(a_hi, b_hi) + (dot(a_hi, b_lo) + dot(a_lo, b_hi))


def _top_rows(s, k):
    nrows = s.shape[0]
    row = lax.broadcasted_iota(jnp.int32, s.shape, 0)
    vals, rows = [], []
    for _ in range(k):
        m = jnp.max(s, axis=0, keepdims=True)
        r = jnp.min(jnp.where(s == m, row, nrows), axis=0, keepdims=True)
        vals.append(m)
        rows.append(r)
        s = jnp.where(row == r, -jnp.inf, s)
    return jnp.concatenate(vals, axis=0), jnp.concatenate(rows, axis=0)


def _tree(op, xs):
    while len(xs) > 1:
        xs = [op(xs[i], xs[i + 1]) if i + 1 < len(xs) else xs[i] for i in range(0, len(xs), 2)]
    return xs[0]


def _product_candidates(s0, s1, i0, i1):
    k = PEER_TOPK
    t = s0.shape[1]
    sub = lax.broadcasted_iota(jnp.int32, (SUBLANES, t), 0)
    blocks = []
    for half in range(k // SUBLANES):
        b = sub + half * SUBLANES
        lo = half * SUBLANES
        blocks.append((s0[0:1] + s1[lo:lo + SUBLANES], b, i0[0:1] * PEER_KEYS + i1[lo:lo + SUBLANES]))
    for a in range(1, SUBLANES):
        valid = (a + 1) * (sub + 1) <= k
        blocks.append((jnp.where(valid, s0[a:a + 1] + s1[0:SUBLANES], -jnp.inf), a * k + sub,
                       i0[a:a + 1] * PEER_KEYS + i1[0:SUBLANES]))
    a = sub + SUBLANES
    blocks.append((s0[SUBLANES:k] + s1[0:1], a * k, i0[SUBLANES:k] * PEER_KEYS + i1[0:1]))
    return blocks


def _peer_route_kernel(x_ref, shift_ref, scale_ref, gain_ref, wqh_ref, wql_ref, skh_ref, skl_ref,
                       rows_ref, hs_ref, gate_ref, xs_ref, q_ref, tv_ref, ti_ref, bs_ref, be_ref, pos_ref):
    tt = x_ref.shape[0]
    k = PEER_TOPK
    x = x_ref[...]
    xn = x * lax.rsqrt(jnp.mean(x * x, axis=-1, keepdims=True) + EPS) * gain_ref[...]
    xn = xn * (1.0 + scale_ref[0]) + shift_ref[0]
    x_hi, x_lo = _split_hi_lo(xn)
    xs_ref[:, :D_MODEL] = x_hi
    xs_ref[:, D_MODEL:] = x_lo
    q = _dot3(x_hi, x_lo, wqh_ref[...], wql_ref[...], ((1,), (0,)))
    for hp in range(2 * PEER_HEADS):
        q_ref[hp] = q[:, hp * PEER_HALF:(hp + 1) * PEER_HALF]

    def sub_topk(h, carry):
        for p in range(2):
            hp = 2 * h + p
            q_hi, q_lo = _split_hi_lo(q_ref[hp])
            s = _dot3(skh_ref[p], skl_ref[p], q_hi, q_lo, ((1,), (1,)))
            tv_ref[hp], ti_ref[hp] = _top_rows(s, k)
        return carry

    lax.fori_loop(0, PEER_HEADS, sub_topk, 0)

    def head_topk(h, carry):
        blocks = _product_candidates(tv_ref[2 * h], tv_ref[2 * h + 1], ti_ref[2 * h], ti_ref[2 * h + 1])
        sums = [b[0] for b in blocks]
        best_s, best_e = [], []
        for _ in range(k):
            m = jnp.max(_tree(jnp.maximum, sums), axis=0, keepdims=True)
            pos = jnp.min(_tree(jnp.minimum, [jnp.where(c == m, b[1], k * k) for c, b in zip(sums, blocks)]),
                          axis=0, keepdims=True)
            hit = [b[1] == pos for b in blocks]
            e = jnp.max(_tree(jnp.maximum, [jnp.where(hh, b[2], -1) for hh, b in zip(hit, blocks)]),
                        axis=0, keepdims=True)
            sums = [jnp.where(hh, -jnp.inf, c) for hh, c in zip(hit, sums)]
            best_s.append(m)
            best_e.append(e)
        bs = jnp.concatenate(best_s, axis=0)
        ex = jnp.exp(bs - bs[0:1])
        off = pl.multiple_of(h * k, k)
        bs_ref[pl.ds(off, k), :] = ex / jnp.sum(ex, axis=0, keepdims=True)
        be_ref[pl.ds(off, k), :] = jnp.concatenate(best_e, axis=0).astype(F32)
        return carry

    lax.fori_loop(0, PEER_HEADS, head_topk, 0)
    half = PEER_SLOTS // 2

    def gather_order(a):
        for c in range(tt // LANES):
            cols = slice(c * LANES, (c + 1) * LANES)
            tile_ref = pos_ref.at[c]
            tile_ref[pl.ds(0, half, stride=2), :] = a[:half, cols]
            tile_ref[pl.ds(1, half, stride=2), :] = a[half:, cols]
        return jnp.concatenate([pos_ref[c].T for c in range(tt // LANES)], axis=0)

    ids = be_ref[...]
    gate_ref[...] = gather_order(bs_ref[...])
    hs_ref[...] = (gather_order(ids).astype(jnp.int32) // HALF_EXPERTS).astype(F32)
    row_off = lambda f: (f.astype(jnp.int32) % HALF_EXPERTS) * SUBLANES
    lo = row_off(ids.T[:, :half])
    hi = row_off(jnp.concatenate([ids[half:], ids[half:]], axis=0).T[:, :half])
    rows_ref[...] = lo | (hi << 16)


ROUTE_TOKEN_TILE = 256
ROUTE_VMEM_LIMIT = 40 * 1024 * 1024


def _peer_route(x2, shift, scale, gain, wq, subkeys, seq_len, *, token_tile=ROUTE_TOKEN_TILE, interpret=False):
    n = x2.shape[0]
    tt = token_tile
    assert seq_len % tt == 0 and n % seq_len == 0
    tiles_per_seq = seq_len // tt
    batch = n // seq_len
    wq_hi, wq_lo = _split_hi_lo(wq)
    sk_hi, sk_lo = _split_hi_lo(subkeys)
    tok = lambda i: (i, 0)
    per_seq = pl.BlockSpec((1, 1, D_MODEL), lambda i: (i // tiles_per_seq, 0, 0))
    whole = pl.BlockSpec(memory_space=pltpu.VMEM)
    slot_spec = pl.BlockSpec((tt, PEER_SLOTS), tok)
    slot_shape = lambda dt: jax.ShapeDtypeStruct((n, PEER_SLOTS), dt)
    pair_spec = pl.BlockSpec((tt, PEER_SLOTS // 2), tok)
    return pl.pallas_call(
        _peer_route_kernel,
        grid=(n // tt,),
        in_specs=[pl.BlockSpec((tt, D_MODEL), tok), per_seq, per_seq,
                  pl.BlockSpec((1, D_MODEL), lambda i: (0, 0)), whole, whole, whole, whole],
        out_specs=[pair_spec, slot_spec, slot_spec, pl.BlockSpec((tt, 2 * D_MODEL), tok)],
        out_shape=[jax.ShapeDtypeStruct((n, PEER_SLOTS // 2), jnp.int32), slot_shape(F32), slot_shape(F32),
                   jax.ShapeDtypeStruct((n, 2 * D_MODEL), BF16)],
        scratch_shapes=[pltpu.VMEM((2 * PEER_HEADS, tt, PEER_HALF), F32),
                        pltpu.VMEM((2 * PEER_HEADS, PEER_TOPK, tt), F32),
                        pltpu.VMEM((2 * PEER_HEADS, PEER_TOPK, tt), jnp.int32),
                        pltpu.VMEM((PEER_SLOTS, tt), F32),
                        pltpu.VMEM((PEER_SLOTS, tt), F32),
                        pltpu.VMEM((tt // LANES, PEER_SLOTS, LANES), F32)],
        compiler_params=pltpu.CompilerParams(dimension_semantics=("arbitrary",),
                                             vmem_limit_bytes=ROUTE_VMEM_LIMIT),
        interpret=interpret,
        name="peer_route",
    )(x2, shift.reshape(batch, 1, D_MODEL), scale.reshape(batch, 1, D_MODEL), gain.reshape(1, D_MODEL),
      wq_hi, wq_lo, sk_hi, sk_lo)


def _peer(x, gain, shift, scale, wq, subkeys, tab_u, tab_v):
    B, S, D = x.shape
    rows, hs, gate, xs = _peer_route(x.reshape(B * S, D), shift, scale, gain, wq, subkeys, S)
    y = _peer_experts(xs.reshape(B * S, SUBROWS, LANES), rows, hs, gate, tab_u, tab_v)
    return y.reshape(B, S, D)


PROJ_TOKEN_TILE = 512
PROJ_VMEM_LIMIT = 48 * 1024 * 1024


def _ada_norm_tile(x, gain, shift, scale):
    xn = x * lax.rsqrt(jnp.mean(x * x, axis=-1, keepdims=True) + EPS) * gain
    return xn * (1.0 + scale) + shift


def _norm_proj_kernel(x_ref, shift_ref, scale_ref, gain_ref, w_ref, *rest, with_gate):
    x_hi, x_lo = _split_hi_lo(_ada_norm_tile(x_ref[...], gain_ref[...], shift_ref[0], scale_ref[0]))
    if with_gate:
        wgh_ref, wgl_ref, main_ref, gate_ref = rest
        gate_ref[...] = _dot3(x_hi, x_lo, wgh_ref[...], wgl_ref[...], ((1,), (0,)))
    else:
        (main_ref,) = rest
    main_ref[...] = jnp.dot(x_hi, w_ref[...], preferred_element_type=F32)


def _norm_proj(x2, shift, scale, gain, w_main, w_gate, seq_len, *, token_tile=PROJ_TOKEN_TILE, interpret=False):
    n = x2.shape[0]
    tt = token_tile
    assert seq_len % tt == 0 and n % seq_len == 0
    tiles_per_seq = seq_len // tt
    batch = n // seq_len
    m = w_main.shape[1]
    tok = lambda i: (i, 0)
    per_seq = pl.BlockSpec((1, 1, D_MODEL), lambda i: (i // tiles_per_seq, 0, 0))
    whole = pl.BlockSpec(memory_space=pltpu.VMEM)
    in_specs = [pl.BlockSpec((tt, D_MODEL), tok), per_seq, per_seq,
                pl.BlockSpec((1, D_MODEL), lambda i: (0, 0)), whole]
    args = [x2, shift.reshape(batch, 1, D_MODEL), scale.reshape(batch, 1, D_MODEL), gain.reshape(1, D_MODEL),
            w_main.astype(BF16)]
    out_specs = [pl.BlockSpec((tt, m), tok)]
    out_shape = [jax.ShapeDtypeStruct((n, m), F32)]
    if w_gate is not None:
        wg_hi, wg_lo = _split_hi_lo(jnp.pad(w_gate, ((0, 0), (0, LANES - w_gate.shape[1]))))
        in_specs += [whole, whole]
        args += [wg_hi, wg_lo]
        out_specs.append(pl.BlockSpec((tt, LANES), tok))
        out_shape.append(jax.ShapeDtypeStruct((n, LANES), F32))
    outs = pl.pallas_call(
        functools.partial(_norm_proj_kernel, with_gate=w_gate is not None),
        grid=(n // tt,), in_specs=in_specs, out_specs=out_specs, out_shape=out_shape,
        compiler_params=pltpu.CompilerParams(dimension_semantics=("arbitrary",),
                                             vmem_limit_bytes=PROJ_VMEM_LIMIT),
        interpret=interpret, name="norm_proj",
    )(*args)
    return (outs[0], outs[1]) if w_gate is not None else (outs[0], None)


def _out_proj_kernel(h_ref, og_ref, x_ref, g_ref, hg_ref, w_ref, o_ref, *, head_dim):
    h = h_ref[...]
    if head_dim is not None:
        parts = []
        for j in range(D_MODEL // head_dim):
            hb = h[:, j * head_dim:(j + 1) * head_dim]
            parts.append(hb * lax.rsqrt(jnp.mean(hb * hb, axis=-1, keepdims=True) + EPS))
        h = jnp.concatenate(parts, axis=1) * hg_ref[...]
    a = jax.nn.sigmoid(og_ref[...]) * h
    y = jnp.dot(a.astype(BF16), w_ref[...], preferred_element_type=F32)
    o_ref[...] = x_ref[...] + g_ref[0] * y


def _out_proj(h2, p, og_block, x2, g, h_gain, w_out, seq_len, head_dim, *, token_tile=PROJ_TOKEN_TILE,
              interpret=False):
    n = x2.shape[0]
    tt = token_tile
    tiles_per_seq = seq_len // tt
    batch = n // seq_len
    tok = lambda i: (i, 0)
    row = pl.BlockSpec((tt, D_MODEL), tok)
    gain = jnp.ones((1, D_MODEL), F32) if h_gain is None else h_gain.reshape(1, D_MODEL)
    return pl.pallas_call(
        functools.partial(_out_proj_kernel, head_dim=head_dim),
        grid=(n // tt,),
        in_specs=[row, pl.BlockSpec((tt, D_MODEL), lambda i: (i, og_block)), row,
                  pl.BlockSpec((1, 1, D_MODEL), lambda i: (i // tiles_per_seq, 0, 0)),
                  pl.BlockSpec((1, D_MODEL), lambda i: (0, 0)),
                  pl.BlockSpec(memory_space=pltpu.VMEM)],
        out_specs=row,
        out_shape=jax.ShapeDtypeStruct((n, D_MODEL), F32),
        compiler_params=pltpu.CompilerParams(dimension_semantics=("arbitrary",),
                                             vmem_limit_bytes=PROJ_VMEM_LIMIT),
        interpret=interpret, name="out_proj",
    )(h2, p, x2, g.reshape(batch, 1, D_MODEL), gain, w_out.astype(BF16))


MLSTM_STATE_W = MLSTM_V_DIM + LANES


def _mlstm_kernel(q_ref, k_ref, v_ref, gc_ref, gr_ref, h_ref, cn_ref, m_ref):
    H, DQK, DV, L = MLSTM_HEADS, MLSTM_QK_DIM, MLSTM_V_DIM, MLSTM_CHUNK

    @pl.when(pl.program_id(1) == 0)
    def _():
        cn_ref[...] = jnp.zeros_like(cn_ref)
        m_ref[...] = jnp.zeros_like(m_ref)

    tril = lax.broadcasted_iota(jnp.int32, (L, L), 1) <= lax.broadcasted_iota(jnp.int32, (L, L), 0)
    gc = gc_ref[...]
    gr = gr_ref[0]
    ones = jnp.ones((L, LANES), F32)
    for h in range(H):
        ig_col, b_col = gc[:, h:h + 1], gc[:, H + h:H + h + 1]
        ig_row, b_row = gr[h:h + 1, :], gr[H + h:H + h + 1, :]
        m_prev = m_ref[h:h + 1, 0:1]
        d_log = jnp.where(tril, b_col - b_row + ig_row, -jnp.inf)
        inter = b_col + m_prev
        m_t = jnp.maximum(inter, jnp.max(d_log, axis=1, keepdims=True))
        w = jnp.exp(d_log - m_t)
        a_inter = jnp.exp(inter - m_t)
        qh = (q_ref[:, h * DQK:(h + 1) * DQK] * (DQK ** -0.5)).astype(BF16)
        kf = k_ref[:, h * DQK:(h + 1) * DQK]
        vh = v_ref[:, h * DV:(h + 1) * DV]
        s = lax.dot_general(qh, kf.astype(BF16), (((1,), (1,)), ((), ())), preferred_element_type=F32) * w
        cn = cn_ref[h]
        qc = jnp.dot(qh, cn.astype(BF16), preferred_element_type=F32)
        num = a_inter * qc[:, :DV] + jnp.dot(s.astype(BF16), vh.astype(BF16), preferred_element_type=F32)
        den = a_inter * qc[:, DV:DV + 1] + jnp.sum(s, axis=1, keepdims=True)
        h_ref[:, h * DV:(h + 1) * DV] = num / jnp.maximum(jnp.abs(den), jnp.exp(-m_t))
        b_last = b_col[L - 1:L, :]
        g_col = b_last - b_col + ig_col
        m_new = jnp.maximum(b_last + m_prev, jnp.max(g_col, axis=0, keepdims=True))
        w_s = jnp.exp(g_col - m_new)
        decay = jnp.exp(b_last + m_prev - m_new)
        kw = (kf * w_s).astype(BF16)
        vaug = jnp.concatenate([vh, ones], axis=1).astype(BF16)
        cn_ref[h] = decay * cn + lax.dot_general(kw, vaug, (((0,), (0,)), ((), ())), preferred_element_type=F32)
        m_ref[h:h + 1, :] = jnp.broadcast_to(m_new, (1, LANES))


def _mlstm_scan(p, gates, b_i, b_f, batch, seq_len, *, interpret=False):
    H, L = MLSTM_HEADS, MLSTM_CHUNK
    n = p.shape[0]
    nc = seq_len // L
    ig = _softcap(gates[:, :H] + b_i)
    lf = jax.nn.log_sigmoid(_softcap(gates[:, H:2 * H] + b_f))
    b = jnp.cumsum(lf.reshape(n // L, L, H), axis=1).reshape(n, H)
    gc = jnp.concatenate([ig, b], axis=1)
    gr = gc.reshape(n // L, L, 2 * H).transpose(0, 2, 1)
    chunk = lambda bi, c: bi * nc + c
    return pl.pallas_call(
        _mlstm_kernel,
        grid=(batch, nc),
        in_specs=[pl.BlockSpec((L, MLSTM_QK_W), lambda bi, c: (chunk(bi, c), 0)),
                  pl.BlockSpec((L, MLSTM_QK_W), lambda bi, c: (chunk(bi, c), 1)),
                  pl.BlockSpec((L, MLSTM_V_W), lambda bi, c: (chunk(bi, c), 1)),
                  pl.BlockSpec((L, 2 * H), lambda bi, c: (chunk(bi, c), 0)),
                  pl.BlockSpec((1, 2 * H, L), lambda bi, c: (chunk(bi, c), 0, 0))],
        out_specs=pl.BlockSpec((L, D_MODEL), lambda bi, c: (chunk(bi, c), 0)),
        out_shape=jax.ShapeDtypeStruct((n, D_MODEL), F32),
        scratch_shapes=[pltpu.VMEM((H, MLSTM_QK_DIM, MLSTM_STATE_W), F32),
                        pltpu.VMEM((SUBLANES, LANES), F32)],
        compiler_params=pltpu.CompilerParams(dimension_semantics=("arbitrary", "arbitrary")),
        interpret=interpret, name="mlstm_scan",
    )(p, p, p, gc, gr)


FOX_BLOCK = 512
FOX_QUERY_PART = 256
FOX_SUM_ROWS = 16
FOX_VMEM_LIMIT = 48 * 1024 * 1024


def _bias_columns(f, query_side):
    hi = f.astype(BF16).astype(F32)
    r1 = f - hi
    lo = r1.astype(BF16).astype(F32)
    lo2 = r1 - lo
    lane = lax.broadcasted_iota(jnp.int32, (f.shape[0], LANES), 1)
    if query_side:
        vals = jnp.where(lane == 0, hi, jnp.where(lane == 1, lo, jnp.where(lane == 2, lo2,
                         jnp.where(lane < 6, 1.0, 0.0))))
    else:
        vals = jnp.where(lane < 3, 1.0, jnp.where(lane == 3, -hi, jnp.where(lane == 4, -lo,
                         jnp.where(lane == 5, -lo2, 0.0))))
    return vals.astype(BF16)


def _head_column(fc, h):
    lane = lax.broadcasted_iota(jnp.int32, fc.shape, 1)
    return jnp.sum(jnp.where(lane == h, fc, 0.0), axis=1, keepdims=True)


def _rms_rows(t, gain):
    return t * lax.rsqrt(jnp.mean(t * t, axis=-1, keepdims=True) + EPS) * gain


def _fox_attn_kernel(q_ref, k_ref, v_ref, fq_ref, fk_ref, qg_ref, kg_ref, o_ref,
                     ka_ref, vt_ref, qa_ref, st_cur_ref, st_next_ref, *state_refs):
    i = pl.program_id(1)
    h = pl.program_id(0) % FOX_HEADS
    tq = q_ref.shape[0]
    nk = k_ref.shape[0] // tq
    hd = FOX_HEAD_DIM
    part = FOX_QUERY_PART
    n_parts = tq // part
    m_refs, acc_refs = state_refs[:n_parts], state_refs[n_parts:]

    @pl.when(i == 0)
    def _():
        def prep(c, carry):
            r = pl.ds(pl.multiple_of(c * tq, tq), tq)
            ka_ref[r, :hd] = _rms_rows(k_ref[r, :], kg_ref[...]).astype(BF16)
            ka_ref[r, hd:] = _bias_columns(_head_column(fk_ref[r, :], h), False)
            vt_ref[c, :hd, :] = v_ref[r, :].T.astype(BF16)
            vt_ref[c, hd:, :] = jnp.ones((FOX_SUM_ROWS, tq), BF16)
            return carry
        lax.fori_loop(0, nk, prep, 0)

    qn = _rms_rows(q_ref[...], qg_ref[...]) * (hd ** -0.5)
    qa_ref[:, :hd] = qn.astype(BF16)
    qa_ref[:, hd:] = _bias_columns(_head_column(fq_ref[...], h), True)
    for m_ref, acc_ref in zip(m_refs, acc_refs):
        m_ref[...] = jnp.full_like(m_ref, -jnp.inf)
        acc_ref[...] = jnp.zeros_like(acc_ref)

    def scores(j, st_ref):
        kblk = ka_ref[pl.ds(pl.multiple_of(j * tq, tq), tq), :]
        for c in range(n_parts):
            st_ref[c] = lax.dot_general(kblk, qa_ref[c * part:(c + 1) * part, :],
                                        (((1,), (1,)), ((), ())), preferred_element_type=F32)

    def softmax_pv(j, masked):
        vtblk = vt_ref[j]
        pts, alphas = [], []
        for c, m_ref in enumerate(m_refs):
            st = st_cur_ref[c]
            if masked:
                key = lax.broadcasted_iota(jnp.int32, st.shape, 0)
                qry = lax.broadcasted_iota(jnp.int32, st.shape, 1) + c * part
                st = jnp.where(key <= qry, st, -jnp.inf)
            m_prev = m_ref[0:1, :]
            m_new = jnp.maximum(m_prev, jnp.max(st, axis=0, keepdims=True))
            alphas.append(jnp.exp(m_prev - m_new))
            pts.append(jnp.exp(st - m_new).astype(BF16))
            m_ref[0:1, :] = m_new
        for pt, alpha, acc_ref in zip(pts, alphas, acc_refs):
            acc_ref[...] = alpha * acc_ref[...] + jnp.dot(vtblk, pt, preferred_element_type=F32)

    def body(j, carry):
        scores(j + 1, st_next_ref)
        softmax_pv(j, False)
        st_cur_ref[...] = st_next_ref[...]
        return carry

    scores(0, st_cur_ref)
    lax.fori_loop(0, i, body, 0)
    softmax_pv(i, True)
    for c, acc_ref in enumerate(acc_refs):
        acc = acc_ref[...]
        o_ref[c * part:(c + 1) * part, :] = (acc[:hd] / acc[hd:hd + 1]).T


def _fox_attention(pq, kv, f_cum, q_gain, k_gain, batch, seq_len, *, block=FOX_BLOCK, interpret=False):
    n = pq.shape[0]
    hd, nh = FOX_HEAD_DIM, FOX_HEADS
    nq = seq_len // block
    return pl.pallas_call(
        _fox_attn_kernel,
        grid=(batch * nh, nq),
        in_specs=[pl.BlockSpec((block, hd), lambda bh, i: ((bh // nh) * nq + i, bh % nh)),
                  pl.BlockSpec((seq_len, hd), lambda bh, i: (bh // nh, bh % nh)),
                  pl.BlockSpec((seq_len, hd), lambda bh, i: (bh // nh, nh + bh % nh)),
                  pl.BlockSpec((block, nh), lambda bh, i: ((bh // nh) * nq + i, 0)),
                  pl.BlockSpec((seq_len, nh), lambda bh, i: (bh // nh, 0)),
                  pl.BlockSpec((1, hd), lambda bh, i: (0, 0)),
                  pl.BlockSpec((1, hd), lambda bh, i: (0, 0))],
        out_specs=pl.BlockSpec((block, hd), lambda bh, i: ((bh // nh) * nq + i, bh % nh)),
        out_shape=jax.ShapeDtypeStruct((n, D_MODEL), F32),
        scratch_shapes=[pltpu.VMEM((seq_len, 2 * hd), BF16),
                        pltpu.VMEM((nq, hd + FOX_SUM_ROWS, block), BF16),
                        pltpu.VMEM((block, 2 * hd), BF16),
                        *[pltpu.VMEM((block // FOX_QUERY_PART, block, FOX_QUERY_PART), F32)] * 2,
                        *[pltpu.VMEM((SUBLANES, FOX_QUERY_PART), F32)] * (block // FOX_QUERY_PART),
                        *[pltpu.VMEM((hd + FOX_SUM_ROWS, FOX_QUERY_PART), F32)] * (block // FOX_QUERY_PART)],
        compiler_params=pltpu.CompilerParams(dimension_semantics=("arbitrary", "arbitrary"),
                                             vmem_limit_bytes=FOX_VMEM_LIMIT),
        interpret=interpret, name="fox_attention",
    )(pq, kv, kv, f_cum, f_cum, q_gain.reshape(1, hd), k_gain.reshape(1, hd))


def kernel(x, c, ada_w, ada_b, mix_norm, ffn_norm, a_w_in, a_b_i, a_b_f, a_h_norm, a_w_out,
           kv_ada_w, kv_ada_b, kv_norm, kv_w, kv_b_f, kv_k_norm, b_w_qo, b_q_norm, b_w_out,
           peer_wq, peer_subkeys, peer_u, peer_v):
    B, S, D = x.shape
    n = B * S
    cs = jax.nn.silu(c)
    x2 = x.reshape(n, D)
    kv = f_cum = None
    for l in range(DEPTH):
        mod = cs @ ada_w[l] + ada_b[l]
        sh1, sc1, g1, sh2, sc2, g2 = jnp.split(mod, N_ADA, axis=-1)
        if l < N_A_LAYERS:
            split = 2 * MLSTM_QK_W + MLSTM_V_W + D_MODEL
            p, gates = _norm_proj(x2, sh1, sc1, mix_norm[l], a_w_in[l][:, :split], a_w_in[l][:, split:], S)
            h = _mlstm_scan(p, gates, a_b_i[l], a_b_f[l], B, S)
            x2 = _out_proj(h, p, 2, x2, g1, a_h_norm[l], a_w_out[l], S, MLSTM_V_DIM)
        else:
            j = l - N_A_LAYERS
            pq, _ = _norm_proj(x2, sh1, sc1, mix_norm[l], b_w_qo[j], None, S)
            att = _fox_attention(pq, kv, f_cum, b_q_norm[j], kv_k_norm, B, S)
            x2 = _out_proj(att, pq, 1, x2, g1, None, b_w_out[j], S, None)
        y = _peer(x2.reshape(B, S, D), ffn_norm[l], sh2, sc2, peer_wq[l], peer_subkeys[l],
                  _pack_expert_table(peer_u[l]), _pack_expert_table(peer_v[l]))
        x2 = x2 + (g2[:, None, :] * y).reshape(n, D)
        if l == N_A_LAYERS - 1:
            sh, sc = jnp.split(cs @ kv_ada_w + kv_ada_b, 2, axis=-1)
            kv, fg = _norm_proj(x2, sh, sc, kv_norm, kv_w[:, :2 * D], kv_w[:, 2 * D:], S)
            log_f = jax.nn.log_sigmoid(fg[:, :FOX_HEADS] + kv_b_f)
            f_cum = jnp.cumsum(log_f.reshape(B, S, FOX_HEADS), axis=1).reshape(n, FOX_HEADS)
    return x2.reshape(B, S, D)
```

```python
import functools
import math

import jax
import jax.numpy as jnp
from jax import lax
from jax.experimental import pallas as pl
from jax.experimental.pallas import tpu as pltpu

F32 = jnp.float32
BF16 = jnp.bfloat16

D_MODEL = 1024
DEPTH = 2
N_A_LAYERS = DEPTH // 2
EPS = 1e-6
N_ADA = 6

MLSTM_HEADS = 4
MLSTM_QK_DIM = D_MODEL // (2 * MLSTM_HEADS)
MLSTM_V_DIM = D_MODEL // MLSTM_HEADS
MLSTM_CHUNK = 64
GATE_SOFTCAP = 15.0
MLSTM_QK_W = MLSTM_HEADS * MLSTM_QK_DIM
MLSTM_V_W = MLSTM_HEADS * MLSTM_V_DIM

FOX_HEADS = 8
FOX_HEAD_DIM = D_MODEL // FOX_HEADS
Q_BLOCK = 128

PEER_HEADS = 8
PEER_KEYS = 128
PEER_EXPERTS = PEER_KEYS * PEER_KEYS
PEER_QUERY_DIM = 256
PEER_HALF = PEER_QUERY_DIM // 2
PEER_TOPK = 16

SUBLANES = 8
LANES = 128
ROW_WORDS = SUBLANES * LANES
assert ROW_WORDS == D_MODEL
PEER_SLOTS = PEER_HEADS * PEER_TOPK
HALF_EXPERTS = PEER_EXPERTS // 2
SUBROWS = 2 * SUBLANES
GATHER_ROWS = PEER_SLOTS * SUBROWS
DOWN_SLAB_ROWS = SUBLANES // 2
DOWN_ROWS = PEER_SLOTS * SUBLANES
TOKEN_GROUP = 2 * SUBLANES
PEER_TOKEN_TILE = 128
PEER_VMEM_LIMIT = 48 * 1024 * 1024


def _pack_expert_table(t):
    lo = lax.bitcast_convert_type(t[:HALF_EXPERTS].astype(BF16), jnp.uint16).astype(jnp.uint32)
    hi = lax.bitcast_convert_type(t[HALF_EXPERTS:].astype(BF16), jnp.uint16).astype(jnp.uint32)
    w = lo | (hi << 16)
    return lax.bitcast_convert_type(w, jnp.int32).reshape(HALF_EXPERTS, SUBLANES, LANES)


def _pack_down_table(t):
    bits = lax.bitcast_convert_type(t.astype(BF16), jnp.uint16).astype(jnp.uint32)
    bits = bits.reshape(t.shape[0], 2, DOWN_SLAB_ROWS, LANES)
    w = bits[:, 0] | (bits[:, 1] << 16)
    return lax.bitcast_convert_type(w, jnp.int32).reshape(t.shape[0] * DOWN_SLAB_ROWS, LANES)


def _slot_expand_matrix():
    slot = lax.broadcasted_iota(jnp.int32, (PEER_SLOTS, GATHER_ROWS), 0)
    sub = lax.broadcasted_iota(jnp.int32, (PEER_SLOTS, GATHER_ROWS), 1)
    return (sub // SUBROWS == slot).astype(BF16)


def _diag_mask():
    r = lax.broadcasted_iota(jnp.int32, (SUBLANES, GATHER_ROWS), 0)
    sub = lax.broadcasted_iota(jnp.int32, (SUBLANES, GATHER_ROWS), 1)
    return (sub % SUBROWS) // 2 == r


def _down_diag_mask():
    j = lax.broadcasted_iota(jnp.int32, (SUBLANES, DOWN_ROWS), 0)
    sub = lax.broadcasted_iota(jnp.int32, (SUBLANES, DOWN_ROWS), 1)
    return sub % SUBLANES == j


def _down_select_matrix():
    sub = lax.broadcasted_iota(jnp.int32, (DOWN_ROWS, PEER_SLOTS), 0)
    pos = lax.broadcasted_iota(jnp.int32, (DOWN_ROWS, PEER_SLOTS), 1)
    return (sub // SUBLANES == pos).astype(F32)


def _half_mask(hs, e):
    hsx = jnp.dot(hs.astype(BF16), e, preferred_element_type=F32)
    par = (lax.broadcasted_iota(jnp.int32, hsx.shape, 1) % 2).astype(F32)
    return hsx == par


def _gather_token(idx_ref, tab_ref, gb_ref, t):
    for s in range(PEER_SLOTS // 2):
        w = idx_ref[t, s]
        a = pl.multiple_of(w & 0xFFFF, SUBLANES)
        b = pl.multiple_of(lax.shift_right_logical(w, 16), SUBLANES)
        gb_ref[pl.ds(2 * s * SUBLANES, SUBLANES), :] = tab_ref[pl.ds(a, SUBLANES), :]
        gb_ref[pl.ds((2 * s + 1) * SUBLANES, SUBLANES), :] = tab_ref[pl.ds(b, SUBLANES), :]
    return pltpu.bitcast(gb_ref[...], BF16)


def _gather_token_down(idx_ref, tab_ref, gb_ref, t):
    for s in range(PEER_SLOTS // 2):
        w = idx_ref[t, s]
        a = pl.multiple_of(w & 0xFFFF, DOWN_SLAB_ROWS)
        b = pl.multiple_of(lax.shift_right_logical(w, 16), DOWN_SLAB_ROWS)
        gb_ref[pl.ds(s * SUBLANES, SUBLANES), :] = jnp.concatenate(
            [tab_ref[pl.ds(a, DOWN_SLAB_ROWS), :], tab_ref[pl.ds(b, DOWN_SLAB_ROWS), :]], axis=0)
    return pltpu.bitcast(gb_ref[...], BF16)


def _split_hi_lo(a):
    hi = a.astype(BF16)
    lo = (a - hi.astype(F32)).astype(BF16)
    return hi, lo


def _peer_down_kernel(idx_ref, xs_ref, gate_ref, tab_ref, sel_ref, w_ref, gb_ref, z_ref):
    tt = w_ref.shape[0]
    diag = _down_diag_mask()

    def group(g, carry):
        rows = []
        for i in range(TOKEN_GROUP):
            t = g * TOKEN_GROUP + i
            gath = _gather_token_down(idx_ref, tab_ref, gb_ref, t)
            y = lax.dot_general(xs_ref[t], gath, (((1,), (1,)), ((), ())),
                                preferred_element_type=F32)
            y8 = y[:SUBLANES] + y[SUBLANES:]
            rows.append(jnp.sum(jnp.where(diag, y8, 0.0), axis=0, keepdims=True))
        z_ref[pl.ds(pl.multiple_of(g * TOKEN_GROUP, TOKEN_GROUP), TOKEN_GROUP), :] = jnp.concatenate(rows, axis=0)
        return carry

    lax.fori_loop(0, tt // TOKEN_GROUP, group, 0)
    act = jnp.dot(z_ref[...], sel_ref[...], precision=lax.Precision.HIGHEST, preferred_element_type=F32)
    gelu = 0.5 * act * (1.0 + lax.erf(act * (1.0 / math.sqrt(2.0))))
    w_ref[...] = gate_ref[...] * gelu


def _peer_up_kernel(idx_ref, w_ref, hs_ref, tab_ref, e_ref, y_ref, gb_ref, ahi_ref, alo_ref):
    tt = w_ref.shape[0]
    diag = _diag_mask()
    e = e_ref[...]
    hm = _half_mask(hs_ref[...], e)
    w_hi, w_lo = _split_hi_lo(w_ref[...])
    ahi_ref[...] = jnp.where(hm, jnp.dot(w_hi, e, preferred_element_type=F32), 0.0)
    alo_ref[...] = jnp.where(hm, jnp.dot(w_lo, e, preferred_element_type=F32), 0.0)

    def group(g, carry):
        base = pl.multiple_of(g * TOKEN_GROUP, TOKEN_GROUP)
        a_hi = ahi_ref[pl.ds(base, TOKEN_GROUP), :]
        a_lo = alo_ref[pl.ds(base, TOKEN_GROUP), :]
        for i in range(TOKEN_GROUP):
            t = g * TOKEN_GROUP + i
            gath = _gather_token(idx_ref, tab_ref, gb_ref, t)
            lhs = jnp.concatenate(
                [jnp.where(diag, a_hi[i:i + 1, :], 0.0), jnp.where(diag, a_lo[i:i + 1, :], 0.0)],
                axis=0).astype(BF16)
            out = jnp.dot(lhs, gath, preferred_element_type=F32)
            y_ref[t] = out[:SUBLANES] + out[SUBLANES:]
        return carry

    lax.fori_loop(0, tt // TOKEN_GROUP, group, 0)


def _peer_experts(xs, rows_down, rows_up, hs, gate, tab_u, tab_v, *, token_tile=PEER_TOKEN_TILE, interpret=False):
    n = xs.shape[0]
    tt = token_tile
    assert n % tt == 0 and tt % TOKEN_GROUP == 0
    e = _slot_expand_matrix()
    sel = _down_select_matrix()

    tok = lambda i: (i, 0)
    const2 = lambda i: (0, 0)
    smem_idx = pl.BlockSpec((tt, PEER_SLOTS // 2), tok, memory_space=pltpu.SMEM)
    tab_v = tab_v.reshape(HALF_EXPERTS * SUBLANES, LANES)
    slot_spec = pl.BlockSpec((tt, PEER_SLOTS), tok)
    table_spec = pl.BlockSpec(memory_space=pltpu.VMEM)
    params = pltpu.CompilerParams(dimension_semantics=("arbitrary",), vmem_limit_bytes=PEER_VMEM_LIMIT)

    w = pl.pallas_call(
        _peer_down_kernel,
        grid=(n // tt,),
        in_specs=[smem_idx,
                  pl.BlockSpec((tt, SUBROWS, LANES), lambda i: (i, 0, 0)),
                  slot_spec, table_spec,
                  pl.BlockSpec((DOWN_ROWS, PEER_SLOTS), const2)],
        out_specs=slot_spec,
        out_shape=jax.ShapeDtypeStruct((n, PEER_SLOTS), F32),
        scratch_shapes=[pltpu.VMEM((PEER_SLOTS // 2 * SUBLANES, LANES), jnp.int32),
                        pltpu.VMEM((tt, DOWN_ROWS), F32)],
        compiler_params=params,
        interpret=interpret,
        name="peer_down",
    )(rows_down, xs, gate, tab_u, sel)

    y = pl.pallas_call(
        _peer_up_kernel,
        grid=(n // tt,),
        in_specs=[smem_idx, slot_spec, slot_spec, table_spec,
                  pl.BlockSpec((PEER_SLOTS, GATHER_ROWS), const2)],
        out_specs=pl.BlockSpec((tt, SUBLANES, LANES), lambda i: (i, 0, 0)),
        out_shape=jax.ShapeDtypeStruct((n, SUBLANES, LANES), F32),
        scratch_shapes=[pltpu.VMEM((PEER_SLOTS * SUBLANES, LANES), jnp.int32),
                        pltpu.VMEM((tt, GATHER_ROWS), F32),
                        pltpu.VMEM((tt, GATHER_ROWS), F32)],
        compiler_params=params,
        interpret=interpret,
        name="peer_up",
    )(rows_up, w, hs, tab_v, e)
    return y.reshape(n, D_MODEL)


def _ada_rmsnorm(x, gain, shift, scale):
    xf = x * lax.rsqrt(jnp.mean(x * x, axis=-1, keepdims=True) + EPS) * gain
    return xf * (1.0 + scale[:, None, :]) + shift[:, None, :]


def _head_rms(t, gain):
    return t * lax.rsqrt(jnp.mean(t * t, axis=-1, keepdims=True) + EPS) * gain


def _softcap(z):
    return GATE_SOFTCAP * jnp.tanh(z / GATE_SOFTCAP)


def _dot3(a_hi, a_lo, b_hi, b_lo, dims):
    dot = functools.partial(lax.dot_general, dimension_numbers=(dims, ((), ())), preferred_element_type=F32)
    return dot(a_hi, b_hi) + (dot(a_hi, b_lo) + dot(a_lo, b_hi))


def _top_rows(s, k):
    nrows = s.shape[0]
    row = lax.broadcasted_iota(jnp.int32, s.shape, 0)
    vals, rows = [], []
    for _ in range(k):
        m = jnp.max(s, axis=0, keepdims=True)
        r = jnp.min(jnp.where(s == m, row, nrows), axis=0, keepdims=True)
        vals.append(m)
        rows.append(r)
        s = jnp.where(row == r, -jnp.inf, s)
    return jnp.concatenate(vals, axis=0), jnp.concatenate(rows, axis=0)


def _tree(op, xs):
    while len(xs) > 1:
        xs = [op(xs[i], xs[i + 1]) if i + 1 < len(xs) else xs[i] for i in range(0, len(xs), 2)]
    return xs[0]


def _product_candidates(s0, s1, i0, i1):
    k = PEER_TOPK
    t = s0.shape[1]
    sub = lax.broadcasted_iota(jnp.int32, (SUBLANES, t), 0)
    blocks = []
    for half in range(k // SUBLANES):
        b = sub + half * SUBLANES
        lo = half * SUBLANES
        blocks.append((s0[0:1] + s1[lo:lo + SUBLANES], b, i0[0:1] * PEER_KEYS + i1[lo:lo + SUBLANES]))
    for a in range(1, SUBLANES):
        valid = (a + 1) * (sub + 1) <= k
        blocks.append((jnp.where(valid, s0[a:a + 1] + s1[0:SUBLANES], -jnp.inf), a * k + sub,
                       i0[a:a + 1] * PEER_KEYS + i1[0:SUBLANES]))
    a = sub + SUBLANES
    blocks.append((s0[SUBLANES:k] + s1[0:1], a * k, i0[SUBLANES:k] * PEER_KEYS + i1[0:1]))
    return blocks


def _peer_route_kernel(x_ref, shift_ref, scale_ref, gain_ref, wqh_ref, wql_ref, skh_ref, skl_ref,
                       rows_down_ref, rows_up_ref, hs_ref, gate_ref, xs_ref, q_ref, tv_ref, ti_ref, bs_ref, be_ref, pos_ref):
    tt = x_ref.shape[0]
    k = PEER_TOPK
    x = x_ref[...]
    xn = x * lax.rsqrt(jnp.mean(x * x, axis=-1, keepdims=True) + EPS) * gain_ref[...]
    xn = xn * (1.0 + scale_ref[0]) + shift_ref[0]
    x_hi, x_lo = _split_hi_lo(xn)
    for j in range(SUBLANES):
        c = (j % 2) * DOWN_SLAB_ROWS + j // 2
        xs_ref[:, j * LANES:(j + 1) * LANES] = x_hi[:, c * LANES:(c + 1) * LANES]
        xs_ref[:, D_MODEL + j * LANES:D_MODEL + (j + 1) * LANES] = x_lo[:, c * LANES:(c + 1) * LANES]
    q = _dot3(x_hi, x_lo, wqh_ref[...], wql_ref[...], ((1,), (0,)))
    for hp in range(2 * PEER_HEADS):
        q_ref[hp] = q[:, hp * PEER_HALF:(hp + 1) * PEER_HALF]

    def sub_topk(h, carry):
        for p in range(2):
            hp = 2 * h + p
            q_hi, q_lo = _split_hi_lo(q_ref[hp])
            s = _dot3(skh_ref[p], skl_ref[p], q_hi, q_lo, ((1,), (1,)))
            tv_ref[hp], ti_ref[hp] = _top_rows(s, k)
        return carry

    lax.fori_loop(0, PEER_HEADS, sub_topk, 0)

    def head_topk(h, carry):
        blocks = _product_candidates(tv_ref[2 * h], tv_ref[2 * h + 1], ti_ref[2 * h], ti_ref[2 * h + 1])
        sums = [b[0] for b in blocks]
        best_s, best_e = [], []
        for _ in range(k):
            m = jnp.max(_tree(jnp.maximum, sums), axis=0, keepdims=True)
            pos = jnp.min(_tree(jnp.minimum, [jnp.where(c == m, b[1], k * k) for c, b in zip(sums, blocks)]),
                          axis=0, keepdims=True)
            hit = [b[1] == pos for b in blocks]
            e = jnp.max(_tree(jnp.maximum, [jnp.where(hh, b[2], -1) for hh, b in zip(hit, blocks)]),
                        axis=0, keepdims=True)
            sums = [jnp.where(hh, -jnp.inf, c) for hh, c in zip(hit, sums)]
            best_s.append(m)
            best_e.append(e)
        bs = jnp.concatenate(best_s, axis=0)
        ex = jnp.exp(bs - bs[0:1])
        off = pl.multiple_of(h * k, k)
        bs_ref[pl.ds(off, k), :] = ex / jnp.sum(ex, axis=0, keepdims=True)
        be_ref[pl.ds(off, k), :] = jnp.concatenate(best_e, axis=0).astype(F32)
        return carry

    lax.fori_loop(0, PEER_HEADS, head_topk, 0)
    half = PEER_SLOTS // 2

    def gather_order(a):
        for c in range(tt // LANES):
            cols = slice(c * LANES, (c + 1) * LANES)
            tile_ref = pos_ref.at[c]
            tile_ref[pl.ds(0, half, stride=2), :] = a[:half, cols]
            tile_ref[pl.ds(1, half, stride=2), :] = a[half:, cols]
        return jnp.concatenate([pos_ref[c].T for c in range(tt // LANES)], axis=0)

    ids = be_ref[...]
    gate_ref[...] = gather_order(bs_ref[...])
    hs_ref[...] = (gather_order(ids).astype(jnp.int32) // HALF_EXPERTS).astype(F32)
    lo = ids.T[:, :half].astype(jnp.int32)
    hi = jnp.concatenate([ids[half:], ids[half:]], axis=0).T[:, :half].astype(jnp.int32)
    up_off = lambda e: (e % HALF_EXPERTS) * SUBLANES
    rows_up_ref[...] = up_off(lo) | (up_off(hi) << 16)
    rows_down_ref[...] = (lo * DOWN_SLAB_ROWS) | ((hi * DOWN_SLAB_ROWS) << 16)


ROUTE_TOKEN_TILE = 256
ROUTE_VMEM_LIMIT = 40 * 1024 * 1024


def _peer_route(x2, shift, scale, gain, wq, subkeys, seq_len, *, token_tile=ROUTE_TOKEN_TILE, interpret=False):
    n = x2.shape[0]
    tt = token_tile
    assert seq_len % tt == 0 and n % seq_len == 0
    tiles_per_seq = seq_len // tt
    batch = n // seq_len
    wq_hi, wq_lo = _split_hi_lo(wq)
    sk_hi, sk_lo = _split_hi_lo(subkeys)
    tok = lambda i: (i, 0)
    per_seq = pl.BlockSpec((1, 1, D_MODEL), lambda i: (i // tiles_per_seq, 0, 0))
    whole = pl.BlockSpec(memory_space=pltpu.VMEM)
    slot_spec = pl.BlockSpec((tt, PEER_SLOTS), tok)
    slot_shape = lambda dt: jax.ShapeDtypeStruct((n, PEER_SLOTS), dt)
    pair_spec = pl.BlockSpec((tt, PEER_SLOTS // 2), tok)
    return pl.pallas_call(
        _peer_route_kernel,
        grid=(n // tt,),
        in_specs=[pl.BlockSpec((tt, D_MODEL), tok), per_seq, per_seq,
                  pl.BlockSpec((1, D_MODEL), lambda i: (0, 0)), whole, whole, whole, whole],
        out_specs=[pair_spec, pair_spec, slot_spec, slot_spec, pl.BlockSpec((tt, 2 * D_MODEL), tok)],
        out_shape=[jax.ShapeDtypeStruct((n, PEER_SLOTS // 2), jnp.int32)] * 2 + [slot_shape(F32), slot_shape(F32),
                   jax.ShapeDtypeStruct((n, 2 * D_MODEL), BF16)],
        scratch_shapes=[pltpu.VMEM((2 * PEER_HEADS, tt, PEER_HALF), F32),
                        pltpu.VMEM((2 * PEER_HEADS, PEER_TOPK, tt), F32),
                        pltpu.VMEM((2 * PEER_HEADS, PEER_TOPK, tt), jnp.int32),
                        pltpu.VMEM((PEER_SLOTS, tt), F32),
                        pltpu.VMEM((PEER_SLOTS, tt), F32),
                        pltpu.VMEM((tt // LANES, PEER_SLOTS, LANES), F32)],
        compiler_params=pltpu.CompilerParams(dimension_semantics=("arbitrary",),
                                             vmem_limit_bytes=ROUTE_VMEM_LIMIT),
        interpret=interpret,
        name="peer_route",
    )(x2, shift.reshape(batch, 1, D_MODEL), scale.reshape(batch, 1, D_MODEL), gain.reshape(1, D_MODEL),
      wq_hi, wq_lo, sk_hi, sk_lo)


def _peer(x, gain, shift, scale, wq, subkeys, tab_u, tab_v):
    B, S, D = x.shape
    rows_down, rows_up, hs, gate, xs = _peer_route(x.reshape(B * S, D), shift, scale, gain, wq, subkeys, S)
    y = _peer_experts(xs.reshape(B * S, SUBROWS, LANES), rows_down, rows_up, hs, gate, tab_u, tab_v)
    return y.reshape(B, S, D)


PROJ_TOKEN_TILE = 512
PROJ_VMEM_LIMIT = 48 * 1024 * 1024


def _ada_norm_tile(x, gain, shift, scale):
    xn = x * lax.rsqrt(jnp.mean(x * x, axis=-1, keepdims=True) + EPS) * gain
    return xn * (1.0 + scale) + shift


def _norm_proj_kernel(x_ref, shift_ref, scale_ref, gain_ref, w_ref, *rest, with_gate):
    x_hi, x_lo = _split_hi_lo(_ada_norm_tile(x_ref[...], gain_ref[...], shift_ref[0], scale_ref[0]))
    if with_gate:
        wgh_ref, wgl_ref, main_ref, gate_ref = rest
        gate_ref[...] = _dot3(x_hi, x_lo, wgh_ref[...], wgl_ref[...], ((1,), (0,)))
    else:
        (main_ref,) = rest
    main_ref[...] = jnp.dot(x_hi, w_ref[...], preferred_element_type=F32)


def _norm_proj(x2, shift, scale, gain, w_main, w_gate, seq_len, *, token_tile=PROJ_TOKEN_TILE, interpret=False):
    n = x2.shape[0]
    tt = token_tile
    assert seq_len % tt == 0 and n % seq_len == 0
    tiles_per_seq = seq_len // tt
    batch = n // seq_len
    m = w_main.shape[1]
    tok = lambda i: (i, 0)
    per_seq = pl.BlockSpec((1, 1, D_MODEL), lambda i: (i // tiles_per_seq, 0, 0))
    whole = pl.BlockSpec(memory_space=pltpu.VMEM)
    in_specs = [pl.BlockSpec((tt, D_MODEL), tok), per_seq, per_seq,
                pl.BlockSpec((1, D_MODEL), lambda i: (0, 0)), whole]
    args = [x2, shift.reshape(batch, 1, D_MODEL), scale.reshape(batch, 1, D_MODEL), gain.reshape(1, D_MODEL),
            w_main.astype(BF16)]
    out_specs = [pl.BlockSpec((tt, m), tok)]
    out_shape = [jax.ShapeDtypeStruct((n, m), F32)]
    if w_gate is not None:
        wg_hi, wg_lo = _split_hi_lo(jnp.pad(w_gate, ((0, 0), (0, LANES - w_gate.shape[1]))))
        in_specs += [whole, whole]
        args += [wg_hi, wg_lo]
        out_specs.append(pl.BlockSpec((tt, LANES), tok))
        out_shape.append(jax.ShapeDtypeStruct((n, LANES), F32))
    outs = pl.pallas_call(
        functools.partial(_norm_proj_kernel, with_gate=w_gate is not None),
        grid=(n // tt,), in_specs=in_specs, out_specs=out_specs, out_shape=out_shape,
        compiler_params=pltpu.CompilerParams(dimension_semantics=("arbitrary",),
                                             vmem_limit_bytes=PROJ_VMEM_LIMIT),
        interpret=interpret, name="norm_proj",
    )(*args)
    return (outs[0], outs[1]) if w_gate is not None else (outs[0], None)


def _out_proj_kernel(h_ref, og_ref, x_ref, g_ref, hg_ref, w_ref, o_ref, *, head_dim):
    h = h_ref[...]
    if head_dim is not None:
        parts = []
        for j in range(D_MODEL // head_dim):
            hb = h[:, j * head_dim:(j + 1) * head_dim]
            parts.append(hb * lax.rsqrt(jnp.mean(hb * hb, axis=-1, keepdims=True) + EPS))
        h = jnp.concatenate(parts, axis=1) * hg_ref[...]
    a = jax.nn.sigmoid(og_ref[...]) * h
    y = jnp.dot(a.astype(BF16), w_ref[...], preferred_element_type=F32)
    o_ref[...] = x_ref[...] + g_ref[0] * y


def _out_proj(h2, p, og_block, x2, g, h_gain, w_out, seq_len, head_dim, *, token_tile=PROJ_TOKEN_TILE,
              interpret=False):
    n = x2.shape[0]
    tt = token_tile
    tiles_per_seq = seq_len // tt
    batch = n // seq_len
    tok = lambda i: (i, 0)
    row = pl.BlockSpec((tt, D_MODEL), tok)
    gain = jnp.ones((1, D_MODEL), F32) if h_gain is None else h_gain.reshape(1, D_MODEL)
    return pl.pallas_call(
        functools.partial(_out_proj_kernel, head_dim=head_dim),
        grid=(n // tt,),
        in_specs=[row, pl.BlockSpec((tt, D_MODEL), lambda i: (i, og_block)), row,
                  pl.BlockSpec((1, 1, D_MODEL), lambda i: (i // tiles_per_seq, 0, 0)),
                  pl.BlockSpec((1, D_MODEL), lambda i: (0, 0)),
                  pl.BlockSpec(memory_space=pltpu.VMEM)],
        out_specs=row,
        out_shape=jax.ShapeDtypeStruct((n, D_MODEL), F32),
        compiler_params=pltpu.CompilerParams(dimension_semantics=("arbitrary",),
                                             vmem_limit_bytes=PROJ_VMEM_LIMIT),
        interpret=interpret, name="out_proj",
    )(h2, p, x2, g.reshape(batch, 1, D_MODEL), gain, w_out.astype(BF16))


MLSTM_STATE_W = MLSTM_V_DIM + LANES


def _mlstm_kernel(q_ref, k_ref, v_ref, gc_ref, gr_ref, h_ref, cn_ref, m_ref):
    H, DQK, DV, L = MLSTM_HEADS, MLSTM_QK_DIM, MLSTM_V_DIM, MLSTM_CHUNK

    @pl.when(pl.program_id(1) == 0)
    def _():
        cn_ref[...] = jnp.zeros_like(cn_ref)
        m_ref[...] = jnp.zeros_like(m_ref)

    tril = lax.broadcasted_iota(jnp.int32, (L, L), 1) <= lax.broadcasted_iota(jnp.int32, (L, L), 0)
    gc = gc_ref[...]
    gr = gr_ref[0]
    ones = jnp.ones((L, LANES), F32)
    for h in range(H):
        ig_col, b_col = gc[:, h:h + 1], gc[:, H + h:H + h + 1]
        ig_row, b_row = gr[h:h + 1, :], gr[H + h:H + h + 1, :]
        m_prev = m_ref[h:h + 1, 0:1]
        d_log = jnp.where(tril, b_col - b_row + ig_row, -jnp.inf)
        inter = b_col + m_prev
        m_t = jnp.maximum(inter, jnp.max(d_log, axis=1, keepdims=True))
        w = jnp.exp(d_log - m_t)
        a_inter = jnp.exp(inter - m_t)
        qh = (q_ref[:, h * DQK:(h + 1) * DQK] * (DQK ** -0.5)).astype(BF16)
        kf = k_ref[:, h * DQK:(h + 1) * DQK]
        vh = v_ref[:, h * DV:(h + 1) * DV]
        s = lax.dot_general(qh, kf.astype(BF16), (((1,), (1,)), ((), ())), preferred_element_type=F32) * w
        cn = cn_ref[h]
        qc = jnp.dot(qh, cn.astype(BF16), preferred_element_type=F32)
        num = a_inter * qc[:, :DV] + jnp.dot(s.astype(BF16), vh.astype(BF16), preferred_element_type=F32)
        den = a_inter * qc[:, DV:DV + 1] + jnp.sum(s, axis=1, keepdims=True)
        h_ref[:, h * DV:(h + 1) * DV] = num / jnp.maximum(jnp.abs(den), jnp.exp(-m_t))
        b_last = b_col[L - 1:L, :]
        g_col = b_last - b_col + ig_col
        m_new = jnp.maximum(b_last + m_prev, jnp.max(g_col, axis=0, keepdims=True))
        w_s = jnp.exp(g_col - m_new)
        decay = jnp.exp(b_last + m_prev - m_new)
        kw = (kf * w_s).astype(BF16)
        vaug = jnp.concatenate([vh, ones], axis=1).astype(BF16)
        cn_ref[h] = decay * cn + lax.dot_general(kw, vaug, (((0,), (0,)), ((), ())), preferred_element_type=F32)
        m_ref[h:h + 1, :] = jnp.broadcast_to(m_new, (1, LANES))


def _mlstm_scan(p, gates, b_i, b_f, batch, seq_len, *, interpret=False):
    H, L = MLSTM_HEADS, MLSTM_CHUNK
    n = p.shape[0]
    nc = seq_len // L
    ig = _softcap(gates[:, :H] + b_i)
    lf = jax.nn.log_sigmoid(_softcap(gates[:, H:2 * H] + b_f))
    b = jnp.cumsum(lf.reshape(n // L, L, H), axis=1).reshape(n, H)
    gc = jnp.concatenate([ig, b], axis=1)
    gr = gc.reshape(n // L, L, 2 * H).transpose(0, 2, 1)
    chunk = lambda bi, c: bi * nc + c
    return pl.pallas_call(
        _mlstm_kernel,
        grid=(batch, nc),
        in_specs=[pl.BlockSpec((L, MLSTM_QK_W), lambda bi, c: (chunk(bi, c), 0)),
                  pl.BlockSpec((L, MLSTM_QK_W), lambda bi, c: (chunk(bi, c), 1)),
                  pl.BlockSpec((L, MLSTM_V_W), lambda bi, c: (chunk(bi, c), 1)),
                  pl.BlockSpec((L, 2 * H), lambda bi, c: (chunk(bi, c), 0)),
                  pl.BlockSpec((1, 2 * H, L), lambda bi, c: (chunk(bi, c), 0, 0))],
        out_specs=pl.BlockSpec((L, D_MODEL), lambda bi, c: (chunk(bi, c), 0)),
        out_shape=jax.ShapeDtypeStruct((n, D_MODEL), F32),
        scratch_shapes=[pltpu.VMEM((H, MLSTM_QK_DIM, MLSTM_STATE_W), F32),
                        pltpu.VMEM((SUBLANES, LANES), F32)],
        compiler_params=pltpu.CompilerParams(dimension_semantics=("arbitrary", "arbitrary")),
        interpret=interpret, name="mlstm_scan",
    )(p, p, p, gc, gr)


FOX_BLOCK = 512
FOX_QUERY_PART = 256
FOX_SUM_ROWS = 16
FOX_VMEM_LIMIT = 48 * 1024 * 1024


def _bias_columns(f, query_side):
    hi = f.astype(BF16).astype(F32)
    r1 = f - hi
    lo = r1.astype(BF16).astype(F32)
    lo2 = r1 - lo
    lane = lax.broadcasted_iota(jnp.int32, (f.shape[0], LANES), 1)
    if query_side:
        vals = jnp.where(lane == 0, hi, jnp.where(lane == 1, lo, jnp.where(lane == 2, lo2,
                         jnp.where(lane < 6, 1.0, 0.0))))
    else:
        vals = jnp.where(lane < 3, 1.0, jnp.where(lane == 3, -hi, jnp.where(lane == 4, -lo,
                         jnp.where(lane == 5, -lo2, 0.0))))
    return vals.astype(BF16)


def _head_column(fc, h):
    lane = lax.broadcasted_iota(jnp.int32, fc.shape, 1)
    return jnp.sum(jnp.where(lane == h, fc, 0.0), axis=1, keepdims=True)


def _rms_rows(t, gain):
    return t * lax.rsqrt(jnp.mean(t * t, axis=-1, keepdims=True) + EPS) * gain


def _fox_attn_kernel(q_ref, k_ref, v_ref, fq_ref, fk_ref, qg_ref, kg_ref, o_ref,
                     ka_ref, vt_ref, qa_ref, st_cur_ref, st_next_ref, *state_refs):
    i = pl.program_id(1)
    h = pl.program_id(0) % FOX_HEADS
    tq = q_ref.shape[0]
    nk = k_ref.shape[0] // tq
    hd = FOX_HEAD_DIM
    part = FOX_QUERY_PART
    n_parts = tq // part
    m_refs, acc_refs = state_refs[:n_parts], state_refs[n_parts:]

    @pl.when(i == 0)
    def _():
        def prep(c, carry):
            r = pl.ds(pl.multiple_of(c * tq, tq), tq)
            ka_ref[r, :hd] = _rms_rows(k_ref[r, :], kg_ref[...]).astype(BF16)
            ka_ref[r, hd:] = _bias_columns(_head_column(fk_ref[r, :], h), False)
            vt_ref[c, :hd, :] = v_ref[r, :].T.astype(BF16)
            vt_ref[c, hd:, :] = jnp.ones((FOX_SUM_ROWS, tq), BF16)
            return carry
        lax.fori_loop(0, nk, prep, 0)

    qn = _rms_rows(q_ref[...], qg_ref[...]) * (hd ** -0.5)
    qa_ref[:, :hd] = qn.astype(BF16)
    qa_ref[:, hd:] = _bias_columns(_head_column(fq_ref[...], h), True)
    for m_ref, acc_ref in zip(m_refs, acc_refs):
        m_ref[...] = jnp.full_like(m_ref, -jnp.inf)
        acc_ref[...] = jnp.zeros_like(acc_ref)

    def scores(j, st_ref):
        kblk = ka_ref[pl.ds(pl.multiple_of(j * tq, tq), tq), :]
        for c in range(n_parts):
            st_ref[c] = lax.dot_general(kblk, qa_ref[c * part:(c + 1) * part, :],
                                        (((1,), (1,)), ((), ())), preferred_element_type=F32)

    def softmax_pv(j, masked):
        vtblk = vt_ref[j]
        pts, alphas = [], []
        for c, m_ref in enumerate(m_refs):
            st = st_cur_ref[c]
            if masked:
                key = lax.broadcasted_iota(jnp.int32, st.shape, 0)
                qry = lax.broadcasted_iota(jnp.int32, st.shape, 1) + c * part
                st = jnp.where(key <= qry, st, -jnp.inf)
            m_prev = m_ref[0:1, :]
            m_new = jnp.maximum(m_prev, jnp.max(st, axis=0, keepdims=True))
            alphas.append(jnp.exp(m_prev - m_new))
            pts.append(jnp.exp(st - m_new).astype(BF16))
            m_ref[0:1, :] = m_new
        for pt, alpha, acc_ref in zip(pts, alphas, acc_refs):
            acc_ref[...] = alpha * acc_ref[...] + jnp.dot(vtblk, pt, preferred_element_type=F32)

    def body(j, carry):
        scores(j + 1, st_next_ref)
        softmax_pv(j, False)
        st_cur_ref[...] = st_next_ref[...]
        return carry

    scores(0, st_cur_ref)
    lax.fori_loop(0, i, body, 0)
    softmax_pv(i, True)
    for c, acc_ref in enumerate(acc_refs):
        acc = acc_ref[...]
        o_ref[c * part:(c + 1) * part, :] = (acc[:hd] / acc[hd:hd + 1]).T


def _fox_attention(pq, kv, f_cum, q_gain, k_gain, batch, seq_len, *, block=FOX_BLOCK, interpret=False):
    n = pq.shape[0]
    hd, nh = FOX_HEAD_DIM, FOX_HEADS
    nq = seq_len // block
    return pl.pallas_call(
        _fox_attn_kernel,
        grid=(batch * nh, nq),
        in_specs=[pl.BlockSpec((block, hd), lambda bh, i: ((bh // nh) * nq + i, bh % nh)),
                  pl.BlockSpec((seq_len, hd), lambda bh, i: (bh // nh, bh % nh)),
                  pl.BlockSpec((seq_len, hd), lambda bh, i: (bh // nh, nh + bh % nh)),
                  pl.BlockSpec((block, nh), lambda bh, i: ((bh // nh) * nq + i, 0)),
                  pl.BlockSpec((seq_len, nh), lambda bh, i: (bh // nh, 0)),
                  pl.BlockSpec((1, hd), lambda bh, i: (0, 0)),
                  pl.BlockSpec((1, hd), lambda bh, i: (0, 0))],
        out_specs=pl.BlockSpec((block, hd), lambda bh, i: ((bh // nh) * nq + i, bh % nh)),
        out_shape=jax.ShapeDtypeStruct((n, D_MODEL), F32),
        scratch_shapes=[pltpu.VMEM((seq_len, 2 * hd), BF16),
                        pltpu.VMEM((nq, hd + FOX_SUM_ROWS, block), BF16),
                        pltpu.VMEM((block, 2 * hd), BF16),
                        *[pltpu.VMEM((block // FOX_QUERY_PART, block, FOX_QUERY_PART), F32)] * 2,
                        *[pltpu.VMEM((SUBLANES, FOX_QUERY_PART), F32)] * (block // FOX_QUERY_PART),
                        *[pltpu.VMEM((hd + FOX_SUM_ROWS, FOX_QUERY_PART), F32)] * (block // FOX_QUERY_PART)],
        compiler_params=pltpu.CompilerParams(dimension_semantics=("arbitrary", "arbitrary"),
                                             vmem_limit_bytes=FOX_VMEM_LIMIT),
        interpret=interpret, name="fox_attention",
    )(pq, kv, kv, f_cum, f_cum, q_gain.reshape(1, hd), k_gain.reshape(1, hd))


def kernel(x, c, ada_w, ada_b, mix_norm, ffn_norm, a_w_in, a_b_i, a_b_f, a_h_norm, a_w_out,
           kv_ada_w, kv_ada_b, kv_norm, kv_w, kv_b_f, kv_k_norm, b_w_qo, b_q_norm, b_w_out,
           peer_wq, peer_subkeys, peer_u, peer_v):
    B, S, D = x.shape
    n = B * S
    cs = jax.nn.silu(c)
    x2 = x.reshape(n, D)
    kv = f_cum = None
    for l in range(DEPTH):
        mod = cs @ ada_w[l] + ada_b[l]
        sh1, sc1, g1, sh2, sc2, g2 = jnp.split(mod, N_ADA, axis=-1)
        if l < N_A_LAYERS:
            split = 2 * MLSTM_QK_W + MLSTM_V_W + D_MODEL
            p, gates = _norm_proj(x2, sh1, sc1, mix_norm[l], a_w_in[l][:, :split], a_w_in[l][:, split:], S)
            h = _mlstm_scan(p, gates, a_b_i[l], a_b_f[l], B, S)
            x2 = _out_proj(h, p, 2, x2, g1, a_h_norm[l], a_w_out[l], S, MLSTM_V_DIM)
        else:
            j = l - N_A_LAYERS
            pq, _ = _norm_proj(x2, sh1, sc1, mix_norm[l], b_w_qo[j], None, S)
            att = _fox_attention(pq, kv, f_cum, b_q_norm[j], kv_k_norm, B, S)
            x2 = _out_proj(att, pq, 1, x2, g1, None, b_w_out[j], S, None)
        y = _peer(x2.reshape(B, S, D), ffn_norm[l], sh2, sc2, peer_wq[l], peer_subkeys[l],
                  _pack_down_table(peer_u[l]), _pack_expert_table(peer_v[l]))
        x2 = x2 + (g2[:, None, :] * y).reshape(n, D)
        if l == N_A_LAYERS - 1:
            sh, sc = jnp.split(cs @ kv_ada_w + kv_ada_b, 2, axis=-1)
            kv, fg = _norm_proj(x2, sh, sc, kv_norm, kv_w[:, :2 * D], kv_w[:, 2 * D:], S)
            log_f = jax.nn.log_sigmoid(fg[:, :FOX_HEADS] + kv_b_f)
            f_cum = jnp.cumsum(log_f.reshape(B, S, FOX_HEADS), axis=1).reshape(n, FOX_HEADS)
    return x2.reshape(B, S, D)
```

```python
import functools
import math

import jax
import jax.numpy as jnp
from jax import lax
from jax.experimental import pallas as pl
from jax.experimental.pallas import tpu as pltpu

F32 = jnp.float32
BF16 = jnp.bfloat16

D_MODEL = 1024
DEPTH = 2
N_A_LAYERS = DEPTH // 2
EPS = 1e-6
N_ADA = 6

MLSTM_HEADS = 4
MLSTM_QK_DIM = D_MODEL // (2 * MLSTM_HEADS)
MLSTM_V_DIM = D_MODEL // MLSTM_HEADS
MLSTM_CHUNK = 64
GATE_SOFTCAP = 15.0
MLSTM_QK_W = MLSTM_HEADS * MLSTM_QK_DIM
MLSTM_V_W = MLSTM_HEADS * MLSTM_V_DIM

FOX_HEADS = 8
FOX_HEAD_DIM = D_MODEL // FOX_HEADS
Q_BLOCK = 128

PEER_HEADS = 8
PEER_KEYS = 128
PEER_EXPERTS = PEER_KEYS * PEER_KEYS
PEER_QUERY_DIM = 256
PEER_HALF = PEER_QUERY_DIM // 2
PEER_TOPK = 16

SUBLANES = 8
LANES = 128
ROW_WORDS = SUBLANES * LANES
assert ROW_WORDS == D_MODEL
PEER_SLOTS = PEER_HEADS * PEER_TOPK
HALF_EXPERTS = PEER_EXPERTS // 2
SUBROWS = 2 * SUBLANES
GATHER_ROWS = PEER_SLOTS * SUBROWS
DOWN_SLAB_ROWS = SUBLANES // 2
DOWN_ROWS = PEER_SLOTS * SUBLANES
TOKEN_GROUP = 2 * SUBLANES
PEER_TOKEN_TILE = 128
PEER_VMEM_LIMIT = 48 * 1024 * 1024


def _pack_expert_table(t):
    lo = lax.bitcast_convert_type(t[:HALF_EXPERTS].astype(BF16), jnp.uint16).astype(jnp.uint32)
    hi = lax.bitcast_convert_type(t[HALF_EXPERTS:].astype(BF16), jnp.uint16).astype(jnp.uint32)
    w = lo | (hi << 16)
    return lax.bitcast_convert_type(w, jnp.int32).reshape(HALF_EXPERTS, SUBLANES, LANES)


def _pack_down_table(t):
    bits = lax.bitcast_convert_type(t.astype(BF16), jnp.uint16).astype(jnp.uint32)
    bits = bits.reshape(t.shape[0], 2, DOWN_SLAB_ROWS, LANES)
    w = bits[:, 0] | (bits[:, 1] << 16)
    return lax.bitcast_convert_type(w, jnp.int32).reshape(t.shape[0] * DOWN_SLAB_ROWS, LANES)


def _slot_expand_matrix():
    slot = lax.broadcasted_iota(jnp.int32, (PEER_SLOTS, GATHER_ROWS), 0)
    sub = lax.broadcasted_iota(jnp.int32, (PEER_SLOTS, GATHER_ROWS), 1)
    return (sub // SUBROWS == slot).astype(BF16)


def _diag_mask():
    r = lax.broadcasted_iota(jnp.int32, (SUBLANES, GATHER_ROWS), 0)
    sub = lax.broadcasted_iota(jnp.int32, (SUBLANES, GATHER_ROWS), 1)
    return (sub % SUBROWS) // 2 == r


def _down_diag_mask():
    j = lax.broadcasted_iota(jnp.int32, (SUBLANES, DOWN_ROWS), 0)
    sub = lax.broadcasted_iota(jnp.int32, (SUBLANES, DOWN_ROWS), 1)
    return sub % SUBLANES == j


def _down_select_matrix():
    sub = lax.broadcasted_iota(jnp.int32, (DOWN_ROWS, PEER_SLOTS), 0)
    pos = lax.broadcasted_iota(jnp.int32, (DOWN_ROWS, PEER_SLOTS), 1)
    return (sub // SUBLANES == pos).astype(F32)


def _half_mask(hs, e):
    hsx = jnp.dot(hs.astype(BF16), e, preferred_element_type=F32)
    par = (lax.broadcasted_iota(jnp.int32, hsx.shape, 1) % 2).astype(F32)
    return hsx == par


def _gather_token(idx_ref, tab_ref, gb_ref, t):
    for s in range(PEER_SLOTS // 2):
        w = idx_ref[t, s]
        a = pl.multiple_of(w & 0xFFFF, SUBLANES)
        b = pl.multiple_of(lax.shift_right_logical(w, 16), SUBLANES)
        gb_ref[pl.ds(2 * s * SUBLANES, SUBLANES), :] = tab_ref[pl.ds(a, SUBLANES), :]
        gb_ref[pl.ds((2 * s + 1) * SUBLANES, SUBLANES), :] = tab_ref[pl.ds(b, SUBLANES), :]
    return pltpu.bitcast(gb_ref[...], BF16)


def _gather_token_down(idx_ref, tab_ref, gb_ref, t):
    for s in range(PEER_SLOTS // 2):
        w = idx_ref[t, s]
        a = pl.multiple_of(w & 0xFFFF, DOWN_SLAB_ROWS)
        b = pl.multiple_of(lax.shift_right_logical(w, 16), DOWN_SLAB_ROWS)
        gb_ref[pl.ds(s * SUBLANES, SUBLANES), :] = jnp.concatenate(
            [tab_ref[pl.ds(a, DOWN_SLAB_ROWS), :], tab_ref[pl.ds(b, DOWN_SLAB_ROWS), :]], axis=0)
    return pltpu.bitcast(gb_ref[...], BF16)


def _split_hi_lo(a):
    hi = a.astype(BF16)
    lo = (a - hi.astype(F32)).astype(BF16)
    return hi, lo


def _peer_down_kernel(idx_ref, xs_ref, gate_ref, tab_ref, sel_ref, w_ref, gb_ref, z_ref):
    tt = w_ref.shape[0]
    diag = _down_diag_mask()

    def group(g, carry):
        rows = []
        for i in range(TOKEN_GROUP):
            t = g * TOKEN_GROUP + i
            gath = _gather_token_down(idx_ref, tab_ref, gb_ref, t)
            y = lax.dot_general(xs_ref[t], gath, (((1,), (1,)), ((), ())),
                                preferred_element_type=F32)
            y8 = y[:SUBLANES] + y[SUBLANES:]
            rows.append(jnp.sum(jnp.where(diag, y8, 0.0), axis=0, keepdims=True))
        z_ref[pl.ds(pl.multiple_of(g * TOKEN_GROUP, TOKEN_GROUP), TOKEN_GROUP), :] = jnp.concatenate(rows, axis=0)
        return carry

    lax.fori_loop(0, tt // TOKEN_GROUP, group, 0)
    act = jnp.dot(z_ref[...], sel_ref[...], precision=lax.Precision.HIGHEST, preferred_element_type=F32)
    gelu = 0.5 * act * (1.0 + lax.erf(act * (1.0 / math.sqrt(2.0))))
    w_ref[...] = gate_ref[...] * gelu


def _down_group(g, idx_ref, xs_ref, tab_ref, gb_ref, z_ref):
    diag = _down_diag_mask()
    rows = []
    for i in range(TOKEN_GROUP):
        t = g * TOKEN_GROUP + i
        gath = _gather_token_down(idx_ref, tab_ref, gb_ref, t)
        y = lax.dot_general(xs_ref[t], gath, (((1,), (1,)), ((), ())),
                            preferred_element_type=F32)
        y8 = y[:SUBLANES] + y[SUBLANES:]
        rows.append(jnp.sum(jnp.where(diag, y8, 0.0), axis=0, keepdims=True))
    z_ref[pl.ds(pl.multiple_of(g * TOKEN_GROUP, TOKEN_GROUP), TOKEN_GROUP), :] = jnp.concatenate(rows, axis=0)


def _down_epilogue(z_ref, sel_ref, gate_ref, w_ref):
    act = jnp.dot(z_ref[...], sel_ref[...], precision=lax.Precision.HIGHEST, preferred_element_type=F32)
    gelu = 0.5 * act * (1.0 + lax.erf(act * (1.0 / math.sqrt(2.0))))
    w_ref[...] = gate_ref[...] * gelu


def _peer_up_kernel(idx_ref, w_ref, hs_ref, tab_ref, e_ref, y_ref, gb_ref, ahi_ref, alo_ref):
    tt = w_ref.shape[0]
    diag = _diag_mask()
    e = e_ref[...]
    hm = _half_mask(hs_ref[...], e)
    w_hi, w_lo = _split_hi_lo(w_ref[...])
    ahi_ref[...] = jnp.where(hm, jnp.dot(w_hi, e, preferred_element_type=F32), 0.0)
    alo_ref[...] = jnp.where(hm, jnp.dot(w_lo, e, preferred_element_type=F32), 0.0)

    def group(g, carry):
        base = pl.multiple_of(g * TOKEN_GROUP, TOKEN_GROUP)
        a_hi = ahi_ref[pl.ds(base, TOKEN_GROUP), :]
        a_lo = alo_ref[pl.ds(base, TOKEN_GROUP), :]
        for i in range(TOKEN_GROUP):
            t = g * TOKEN_GROUP + i
            gath = _gather_token(idx_ref, tab_ref, gb_ref, t)
            lhs = jnp.concatenate(
                [jnp.where(diag, a_hi[i:i + 1, :], 0.0), jnp.where(diag, a_lo[i:i + 1, :], 0.0)],
                axis=0).astype(BF16)
            out = jnp.dot(lhs, gath, preferred_element_type=F32)
            y_ref[t] = out[:SUBLANES] + out[SUBLANES:]
        return carry

    lax.fori_loop(0, tt // TOKEN_GROUP, group, 0)


def _peer_experts(xs, rows_down, rows_up, hs, gate, tab_u, tab_v, *, token_tile=PEER_TOKEN_TILE, interpret=False):
    n = xs.shape[0]
    tt = token_tile
    assert n % tt == 0 and tt % TOKEN_GROUP == 0
    e = _slot_expand_matrix()
    sel = _down_select_matrix()

    tok = lambda i: (i, 0)
    const2 = lambda i: (0, 0)
    smem_idx = pl.BlockSpec((tt, PEER_SLOTS // 2), tok, memory_space=pltpu.SMEM)
    tab_v = tab_v.reshape(HALF_EXPERTS * SUBLANES, LANES)
    slot_spec = pl.BlockSpec((tt, PEER_SLOTS), tok)
    table_spec = pl.BlockSpec(memory_space=pltpu.VMEM)
    params = pltpu.CompilerParams(dimension_semantics=("arbitrary",), vmem_limit_bytes=PEER_VMEM_LIMIT)

    w = pl.pallas_call(
        _peer_down_kernel,
        grid=(n // tt,),
        in_specs=[smem_idx,
                  pl.BlockSpec((tt, SUBROWS, LANES), lambda i: (i, 0, 0)),
                  slot_spec, table_spec,
                  pl.BlockSpec((DOWN_ROWS, PEER_SLOTS), const2)],
        out_specs=slot_spec,
        out_shape=jax.ShapeDtypeStruct((n, PEER_SLOTS), F32),
        scratch_shapes=[pltpu.VMEM((PEER_SLOTS // 2 * SUBLANES, LANES), jnp.int32),
                        pltpu.VMEM((tt, DOWN_ROWS), F32)],
        compiler_params=params,
        interpret=interpret,
        name="peer_down",
    )(rows_down, xs, gate, tab_u, sel)

    y = pl.pallas_call(
        _peer_up_kernel,
        grid=(n // tt,),
        in_specs=[smem_idx, slot_spec, slot_spec, table_spec,
                  pl.BlockSpec((PEER_SLOTS, GATHER_ROWS), const2)],
        out_specs=pl.BlockSpec((tt, SUBLANES, LANES), lambda i: (i, 0, 0)),
        out_shape=jax.ShapeDtypeStruct((n, SUBLANES, LANES), F32),
        scratch_shapes=[pltpu.VMEM((PEER_SLOTS * SUBLANES, LANES), jnp.int32),
                        pltpu.VMEM((tt, GATHER_ROWS), F32),
                        pltpu.VMEM((tt, GATHER_ROWS), F32)],
        compiler_params=params,
        interpret=interpret,
        name="peer_up",
    )(rows_up, w, hs, tab_v, e)
    return y.reshape(n, D_MODEL)


def _peer_up(rows_up, w, hs, tab_v, *, token_tile=PEER_TOKEN_TILE, interpret=False):
    n = w.shape[0]
    tt = token_tile
    assert n % tt == 0 and tt % TOKEN_GROUP == 0
    tok = lambda i: (i, 0)
    slot_spec = pl.BlockSpec((tt, PEER_SLOTS), tok)
    y = pl.pallas_call(
        _peer_up_kernel,
        grid=(n // tt,),
        in_specs=[pl.BlockSpec((tt, PEER_SLOTS // 2), tok, memory_space=pltpu.SMEM), slot_spec, slot_spec,
                  pl.BlockSpec(memory_space=pltpu.VMEM),
                  pl.BlockSpec((PEER_SLOTS, GATHER_ROWS), lambda i: (0, 0))],
        out_specs=pl.BlockSpec((tt, SUBLANES, LANES), lambda i: (i, 0, 0)),
        out_shape=jax.ShapeDtypeStruct((n, SUBLANES, LANES), F32),
        scratch_shapes=[pltpu.VMEM((PEER_SLOTS * SUBLANES, LANES), jnp.int32),
                        pltpu.VMEM((tt, GATHER_ROWS), F32),
                        pltpu.VMEM((tt, GATHER_ROWS), F32)],
        compiler_params=pltpu.CompilerParams(dimension_semantics=("arbitrary",),
                                             vmem_limit_bytes=PEER_VMEM_LIMIT),
        interpret=interpret,
        name="peer_up",
    )(rows_up, w, hs, tab_v.reshape(HALF_EXPERTS * SUBLANES, LANES), _slot_expand_matrix())
    return y.reshape(n, D_MODEL)


def _ada_rmsnorm(x, gain, shift, scale):
    xf = x * lax.rsqrt(jnp.mean(x * x, axis=-1, keepdims=True) + EPS) * gain
    return xf * (1.0 + scale[:, None, :]) + shift[:, None, :]


def _head_rms(t, gain):
    return t * lax.rsqrt(jnp.mean(t * t, axis=-1, keepdims=True) + EPS) * gain


def _softcap(z):
    return GATE_SOFTCAP * jnp.tanh(z / GATE_SOFTCAP)


def _dot3(a_hi, a_lo, b_hi, b_lo, dims):
    dot = functools.partial(lax.dot_general, dimension_numbers=(dims, ((), ())), preferred_element_type=F32)
    return dot(a_hi, b_hi) + (dot(a_hi, b_lo) + dot(a_lo, b_hi))


def _top_rows(s, k):
    nrows = s.shape[0]
    row = lax.broadcasted_iota(jnp.int32, s.shape, 0)
    vals, rows = [], []
    for _ in range(k):
        m = jnp.max(s, axis=0, keepdims=True)
        r = jnp.min(jnp.where(s == m, row, nrows), axis=0, keepdims=True)
        vals.append(m)
        rows.append(r)
        s = jnp.where(row == r, -jnp.inf, s)
    return jnp.concatenate(vals, axis=0), jnp.concatenate(rows, axis=0)


def _tree(op, xs):
    while len(xs) > 1:
        xs = [op(xs[i], xs[i + 1]) if i + 1 < len(xs) else xs[i] for i in range(0, len(xs), 2)]
    return xs[0]


def _product_candidates(s0, s1, i0, i1):
    k = PEER_TOPK
    t = s0.shape[1]
    sub = lax.broadcasted_iota(jnp.int32, (SUBLANES, t), 0)
    blocks = []
    for half in range(k // SUBLANES):
        b = sub + half * SUBLANES
        lo = half * SUBLANES
        blocks.append((s0[0:1] + s1[lo:lo + SUBLANES], b, i0[0:1] * PEER_KEYS + i1[lo:lo + SUBLANES]))
    for a in range(1, SUBLANES):
        valid = (a + 1) * (sub + 1) <= k
        blocks.append((jnp.where(valid, s0[a:a + 1] + s1[0:SUBLANES], -jnp.inf), a * k + sub,
                       i0[a:a + 1] * PEER_KEYS + i1[0:SUBLANES]))
    a = sub + SUBLANES
    blocks.append((s0[SUBLANES:k] + s1[0:1], a * k, i0[SUBLANES:k] * PEER_KEYS + i1[0:1]))
    return blocks


def _peer_route_kernel(x_ref, shift_ref, scale_ref, gain_ref, wqh_ref, wql_ref, skh_ref, skl_ref,
                       rows_down_ref, rows_up_ref, hs_ref, gate_ref, xs_ref, q_ref, tv_ref, ti_ref, bs_ref, be_ref, pos_ref):
    tt = x_ref.shape[0]
    k = PEER_TOPK
    x = x_ref[...]
    xn = x * lax.rsqrt(jnp.mean(x * x, axis=-1, keepdims=True) + EPS) * gain_ref[...]
    xn = xn * (1.0 + scale_ref[0]) + shift_ref[0]
    x_hi, x_lo = _split_hi_lo(xn)
    for j in range(SUBLANES):
        c = (j % 2) * DOWN_SLAB_ROWS + j // 2
        xs_ref[:, j * LANES:(j + 1) * LANES] = x_hi[:, c * LANES:(c + 1) * LANES]
        xs_ref[:, D_MODEL + j * LANES:D_MODEL + (j + 1) * LANES] = x_lo[:, c * LANES:(c + 1) * LANES]
    q = _dot3(x_hi, x_lo, wqh_ref[...], wql_ref[...], ((1,), (0,)))
    for hp in range(2 * PEER_HEADS):
        q_ref[hp] = q[:, hp * PEER_HALF:(hp + 1) * PEER_HALF]

    def sub_topk(h, carry):
        for p in range(2):
            hp = 2 * h + p
            q_hi, q_lo = _split_hi_lo(q_ref[hp])
            s = _dot3(skh_ref[p], skl_ref[p], q_hi, q_lo, ((1,), (1,)))
            tv_ref[hp], ti_ref[hp] = _top_rows(s, k)
        return carry

    lax.fori_loop(0, PEER_HEADS, sub_topk, 0)

    def head_topk(h, carry):
        blocks = _product_candidates(tv_ref[2 * h], tv_ref[2 * h + 1], ti_ref[2 * h], ti_ref[2 * h + 1])
        sums = [b[0] for b in blocks]
        best_s, best_e = [], []
        for _ in range(k):
            m = jnp.max(_tree(jnp.maximum, sums), axis=0, keepdims=True)
            pos = jnp.min(_tree(jnp.minimum, [jnp.where(c == m, b[1], k * k) for c, b in zip(sums, blocks)]),
                          axis=0, keepdims=True)
            hit = [b[1] == pos for b in blocks]
            e = jnp.max(_tree(jnp.maximum, [jnp.where(hh, b[2], -1) for hh, b in zip(hit, blocks)]),
                        axis=0, keepdims=True)
            sums = [jnp.where(hh, -jnp.inf, c) for hh, c in zip(hit, sums)]
            best_s.append(m)
            best_e.append(e)
        bs = jnp.concatenate(best_s, axis=0)
        ex = jnp.exp(bs - bs[0:1])
        off = pl.multiple_of(h * k, k)
        bs_ref[pl.ds(off, k), :] = ex / jnp.sum(ex, axis=0, keepdims=True)
        be_ref[pl.ds(off, k), :] = jnp.concatenate(best_e, axis=0).astype(F32)
        return carry

    lax.fori_loop(0, PEER_HEADS, head_topk, 0)
    half = PEER_SLOTS // 2

    def gather_order(a):
        for c in range(tt // LANES):
            cols = slice(c * LANES, (c + 1) * LANES)
            tile_ref = pos_ref.at[c]
            tile_ref[pl.ds(0, half, stride=2), :] = a[:half, cols]
            tile_ref[pl.ds(1, half, stride=2), :] = a[half:, cols]
        return jnp.concatenate([pos_ref[c].T for c in range(tt // LANES)], axis=0)

    ids = be_ref[...]
    gate_ref[...] = gather_order(bs_ref[...])
    hs_ref[...] = (gather_order(ids).astype(jnp.int32) // HALF_EXPERTS).astype(F32)
    lo = ids.T[:, :half].astype(jnp.int32)
    hi = jnp.concatenate([ids[half:], ids[half:]], axis=0).T[:, :half].astype(jnp.int32)
    up_off = lambda e: (e % HALF_EXPERTS) * SUBLANES
    rows_up_ref[...] = up_off(lo) | (up_off(hi) << 16)
    rows_down_ref[...] = (lo * DOWN_SLAB_ROWS) | ((hi * DOWN_SLAB_ROWS) << 16)


def _route_prologue(x_ref, shift_ref, scale_ref, gain_ref, wqh_ref, wql_ref, xs_ref, q_ref):
    x = x_ref[...]
    xn = x * lax.rsqrt(jnp.mean(x * x, axis=-1, keepdims=True) + EPS) * gain_ref[...]
    xn = xn * (1.0 + scale_ref[0]) + shift_ref[0]
    x_hi, x_lo = _split_hi_lo(xn)
    for j in range(SUBLANES):
        c = (j % 2) * DOWN_SLAB_ROWS + j // 2
        xs_ref[:, j * LANES:(j + 1) * LANES] = x_hi[:, c * LANES:(c + 1) * LANES]
        xs_ref[:, D_MODEL + j * LANES:D_MODEL + (j + 1) * LANES] = x_lo[:, c * LANES:(c + 1) * LANES]
    q = _dot3(x_hi, x_lo, wqh_ref[...], wql_ref[...], ((1,), (0,)))
    for hp in range(2 * PEER_HEADS):
        q_ref[hp] = q[:, hp * PEER_HALF:(hp + 1) * PEER_HALF]


def _route_head(h, skh_ref, skl_ref, q_ref, tv_ref, ti_ref, bs_ref, be_ref):
    k = PEER_TOPK
    for p in range(2):
        q_hi, q_lo = _split_hi_lo(q_ref[2 * h + p])
        s = _dot3(skh_ref[p], skl_ref[p], q_hi, q_lo, ((1,), (1,)))
        tv_ref[p], ti_ref[p] = _top_rows(s, k)
    blocks = _product_candidates(tv_ref[0], tv_ref[1], ti_ref[0], ti_ref[1])
    sums = [b[0] for b in blocks]
    best_s, best_e = [], []
    for _ in range(k):
        m = jnp.max(_tree(jnp.maximum, sums), axis=0, keepdims=True)
        pos = jnp.min(_tree(jnp.minimum, [jnp.where(c == m, b[1], k * k) for c, b in zip(sums, blocks)]),
                      axis=0, keepdims=True)
        hit = [b[1] == pos for b in blocks]
        e = jnp.max(_tree(jnp.maximum, [jnp.where(hh, b[2], -1) for hh, b in zip(hit, blocks)]),
                    axis=0, keepdims=True)
        sums = [jnp.where(hh, -jnp.inf, c) for hh, c in zip(hit, sums)]
        best_s.append(m)
        best_e.append(e)
    bs = jnp.concatenate(best_s, axis=0)
    ex = jnp.exp(bs - bs[0:1])
    off = pl.multiple_of(h * k, k)
    bs_ref[pl.ds(off, k), :] = ex / jnp.sum(ex, axis=0, keepdims=True)
    be_ref[pl.ds(off, k), :] = jnp.concatenate(best_e, axis=0).astype(F32)


def _route_epilogue(bs_ref, be_ref, pos_ref, rows_down_ref, rows_up_ref, hs_ref, gate_ref):
    tt = gate_ref.shape[0]
    half = PEER_SLOTS // 2

    def gather_order(a):
        for c in range(tt // LANES):
            cols = slice(c * LANES, (c + 1) * LANES)
            tile_ref = pos_ref.at[c]
            tile_ref[pl.ds(0, half, stride=2), :] = a[:half, cols]
            tile_ref[pl.ds(1, half, stride=2), :] = a[half:, cols]
        return jnp.concatenate([pos_ref[c].T for c in range(tt // LANES)], axis=0)

    ids = be_ref[...]
    gate_ref[...] = gather_order(bs_ref[...])
    hs_ref[...] = (gather_order(ids).astype(jnp.int32) // HALF_EXPERTS).astype(F32)
    lo = ids.T[:, :half].astype(jnp.int32)
    hi = jnp.concatenate([ids[half:], ids[half:]], axis=0).T[:, :half].astype(jnp.int32)
    up_off = lambda e: (e % HALF_EXPERTS) * SUBLANES
    rows_up_ref[...] = up_off(lo) | (up_off(hi) << 16)
    rows_down_ref[...] = (lo * DOWN_SLAB_ROWS) | ((hi * DOWN_SLAB_ROWS) << 16)


N_ROUTE_IN, N_DOWN_IN, N_ROUTE_OUT, N_ROUTE_SCRATCH = 8, 5, 5, 6


def _route_down_kernel(*refs, do_route, do_down):
    refs = list(refs)
    take = lambda count: [refs.pop(0) for _ in range(count)]
    route_in = take(N_ROUTE_IN) if do_route else None
    down_in = take(N_DOWN_IN) if do_down else None
    route_out = take(N_ROUTE_OUT) if do_route else None
    w_ref = take(1)[0] if do_down else None
    route_scratch = take(N_ROUTE_SCRATCH) if do_route else None
    down_scratch = take(2) if do_down else None

    if do_route:
        x_ref, shift_ref, scale_ref, gain_ref, wqh_ref, wql_ref, skh_ref, skl_ref = route_in
        rows_down_ref, rows_up_ref, hs_ref, gate_ref, xs_ref = route_out
        q_ref, tv_ref, ti_ref, bs_ref, be_ref, pos_ref = route_scratch
        _route_prologue(x_ref, shift_ref, scale_ref, gain_ref, wqh_ref, wql_ref, xs_ref, q_ref)
    if do_down:
        idx_ref, xs_in_ref, gate_in_ref, tab_ref, sel_ref = down_in
        gb_ref, z_ref = down_scratch

    def step(h, carry):
        if do_route:
            _route_head(h, skh_ref, skl_ref, q_ref, tv_ref, ti_ref, bs_ref, be_ref)
        if do_down:
            _down_group(h, idx_ref, xs_in_ref, tab_ref, gb_ref, z_ref)
        return carry

    lax.fori_loop(0, PEER_HEADS, step, 0)
    if do_route:
        _route_epilogue(bs_ref, be_ref, pos_ref, rows_down_ref, rows_up_ref, hs_ref, gate_ref)
    if do_down:
        _down_epilogue(z_ref, sel_ref, gate_in_ref, w_ref)


ROUTE_DOWN_VMEM_LIMIT = 56 * 1024 * 1024


def _route_down(route_args, down_args, seq_len, *, interpret=False):
    tt = PEER_TOKEN_TILE
    assert tt // TOKEN_GROUP == PEER_HEADS and seq_len % tt == 0
    tiles = seq_len // tt
    tok = lambda i: (i, 0)
    whole = pl.BlockSpec(memory_space=pltpu.VMEM)
    slot_spec = pl.BlockSpec((tt, PEER_SLOTS), tok)
    pair_spec = pl.BlockSpec((tt, PEER_SLOTS // 2), tok)
    slot_shape = jax.ShapeDtypeStruct((seq_len, PEER_SLOTS), F32)
    pair_shape = jax.ShapeDtypeStruct((seq_len, PEER_SLOTS // 2), jnp.int32)
    in_specs, args, out_specs, out_shape, scratch = [], [], [], [], []
    if route_args is not None:
        x2, b, shift, scale, gain, wq_hi, wq_lo, sk_hi, sk_lo = route_args
        batch = shift.shape[0]
        per_seq = pl.BlockSpec((1, 1, D_MODEL), lambda i: (b, 0, 0))
        in_specs += [pl.BlockSpec((tt, D_MODEL), lambda i: (b * tiles + i, 0)), per_seq, per_seq,
                     pl.BlockSpec((1, D_MODEL), lambda i: (0, 0)), whole, whole, whole, whole]
        args += [x2, shift.reshape(batch, 1, D_MODEL), scale.reshape(batch, 1, D_MODEL),
                 gain.reshape(1, D_MODEL), wq_hi, wq_lo, sk_hi, sk_lo]
    if down_args is not None:
        rows_down, xs, gate, tab_u = down_args
        in_specs += [pl.BlockSpec((tt, PEER_SLOTS // 2), tok, memory_space=pltpu.SMEM),
                     pl.BlockSpec((tt, SUBROWS, LANES), lambda i: (i, 0, 0)), slot_spec, whole,
                     pl.BlockSpec((DOWN_ROWS, PEER_SLOTS), lambda i: (0, 0))]
        args += [rows_down, xs.reshape(seq_len, SUBROWS, LANES), gate, tab_u, _down_select_matrix()]
    if route_args is not None:
        out_specs += [pair_spec, pair_spec, slot_spec, slot_spec, pl.BlockSpec((tt, 2 * D_MODEL), tok)]
        out_shape += [pair_shape, pair_shape, slot_shape, slot_shape,
                      jax.ShapeDtypeStruct((seq_len, 2 * D_MODEL), BF16)]
    if down_args is not None:
        out_specs.append(slot_spec)
        out_shape.append(slot_shape)
    if route_args is not None:
        scratch += [pltpu.VMEM((2 * PEER_HEADS, tt, PEER_HALF), F32),
                    pltpu.VMEM((2, PEER_TOPK, tt), F32),
                    pltpu.VMEM((2, PEER_TOPK, tt), jnp.int32),
                    pltpu.VMEM((PEER_SLOTS, tt), F32),
                    pltpu.VMEM((PEER_SLOTS, tt), F32),
                    pltpu.VMEM((tt // LANES, PEER_SLOTS, LANES), F32)]
    if down_args is not None:
        scratch += [pltpu.VMEM((PEER_SLOTS // 2 * SUBLANES, LANES), jnp.int32),
                    pltpu.VMEM((tt, DOWN_ROWS), F32)]
    outs = pl.pallas_call(
        functools.partial(_route_down_kernel, do_route=route_args is not None, do_down=down_args is not None),
        grid=(tiles,), in_specs=in_specs, out_specs=out_specs, out_shape=out_shape, scratch_shapes=scratch,
        compiler_params=pltpu.CompilerParams(dimension_semantics=("arbitrary",),
                                             vmem_limit_bytes=ROUTE_DOWN_VMEM_LIMIT),
        interpret=interpret, name="peer_route_down",
    )(*args)
    routed = tuple(outs[:N_ROUTE_OUT]) if route_args is not None else None
    w = outs[-1] if down_args is not None else None
    return routed, w


ROUTE_TOKEN_TILE = 256
ROUTE_VMEM_LIMIT = 40 * 1024 * 1024


def _peer_route(x2, shift, scale, gain, wq, subkeys, seq_len, *, token_tile=ROUTE_TOKEN_TILE, interpret=False):
    n = x2.shape[0]
    tt = token_tile
    assert seq_len % tt == 0 and n % seq_len == 0
    tiles_per_seq = seq_len // tt
    batch = n // seq_len
    wq_hi, wq_lo = _split_hi_lo(wq)
    sk_hi, sk_lo = _split_hi_lo(subkeys)
    tok = lambda i: (i, 0)
    per_seq = pl.BlockSpec((1, 1, D_MODEL), lambda i: (i // tiles_per_seq, 0, 0))
    whole = pl.BlockSpec(memory_space=pltpu.VMEM)
    slot_spec = pl.BlockSpec((tt, PEER_SLOTS), tok)
    slot_shape = lambda dt: jax.ShapeDtypeStruct((n, PEER_SLOTS), dt)
    pair_spec = pl.BlockSpec((tt, PEER_SLOTS // 2), tok)
    return pl.pallas_call(
        _peer_route_kernel,
        grid=(n // tt,),
        in_specs=[pl.BlockSpec((tt, D_MODEL), tok), per_seq, per_seq,
                  pl.BlockSpec((1, D_MODEL), lambda i: (0, 0)), whole, whole, whole, whole],
        out_specs=[pair_spec, pair_spec, slot_spec, slot_spec, pl.BlockSpec((tt, 2 * D_MODEL), tok)],
        out_shape=[jax.ShapeDtypeStruct((n, PEER_SLOTS // 2), jnp.int32)] * 2 + [slot_shape(F32), slot_shape(F32),
                   jax.ShapeDtypeStruct((n, 2 * D_MODEL), BF16)],
        scratch_shapes=[pltpu.VMEM((2 * PEER_HEADS, tt, PEER_HALF), F32),
                        pltpu.VMEM((2 * PEER_HEADS, PEER_TOPK, tt), F32),
                        pltpu.VMEM((2 * PEER_HEADS, PEER_TOPK, tt), jnp.int32),
                        pltpu.VMEM((PEER_SLOTS, tt), F32),
                        pltpu.VMEM((PEER_SLOTS, tt), F32),
                        pltpu.VMEM((tt // LANES, PEER_SLOTS, LANES), F32)],
        compiler_params=pltpu.CompilerParams(dimension_semantics=("arbitrary",),
                                             vmem_limit_bytes=ROUTE_VMEM_LIMIT),
        interpret=interpret,
        name="peer_route",
    )(x2, shift.reshape(batch, 1, D_MODEL), scale.reshape(batch, 1, D_MODEL), gain.reshape(1, D_MODEL),
      wq_hi, wq_lo, sk_hi, sk_lo)


def _peer(x, gain, shift, scale, wq, subkeys, tab_u, tab_v):
    B, S, D = x.shape
    x2 = x.reshape(B * S, D)
    wq_hi, wq_lo = _split_hi_lo(wq)
    sk_hi, sk_lo = _split_hi_lo(subkeys)
    routed, rows_up, hs, ws = None, [], [], []
    for b in range(B + 1):
        route_args = (x2, b, shift, scale, gain, wq_hi, wq_lo, sk_hi, sk_lo) if b < B else None
        down_args = (routed[0], routed[4], routed[3], tab_u) if routed is not None else None
        routed, w = _route_down(route_args, down_args, S)
        if w is not None:
            ws.append(w)
        if routed is not None:
            rows_up.append(routed[1])
            hs.append(routed[2])
    y = _peer_up(jnp.concatenate(rows_up), jnp.concatenate(ws), jnp.concatenate(hs), tab_v)
    return y.reshape(B, S, D)


PROJ_TOKEN_TILE = 512
PROJ_VMEM_LIMIT = 48 * 1024 * 1024


def _ada_norm_tile(x, gain, shift, scale):
    xn = x * lax.rsqrt(jnp.mean(x * x, axis=-1, keepdims=True) + EPS) * gain
    return xn * (1.0 + scale) + shift


def _norm_proj_kernel(x_ref, shift_ref, scale_ref, gain_ref, w_ref, *rest, with_gate):
    x_hi, x_lo = _split_hi_lo(_ada_norm_tile(x_ref[...], gain_ref[...], shift_ref[0], scale_ref[0]))
    if with_gate:
        wgh_ref, wgl_ref, main_ref, gate_ref = rest
        gate_ref[...] = _dot3(x_hi, x_lo, wgh_ref[...], wgl_ref[...], ((1,), (0,)))
    else:
        (main_ref,) = rest
    main_ref[...] = jnp.dot(x_hi, w_ref[...], preferred_element_type=F32)


def _norm_proj(x2, shift, scale, gain, w_main, w_gate, seq_len, *, token_tile=PROJ_TOKEN_TILE, interpret=False):
    n = x2.shape[0]
    tt = token_tile
    assert seq_len % tt == 0 and n % seq_len == 0
    tiles_per_seq = seq_len // tt
    batch = n // seq_len
    m = w_main.shape[1]
    tok = lambda i: (i, 0)
    per_seq = pl.BlockSpec((1, 1, D_MODEL), lambda i: (i // tiles_per_seq, 0, 0))
    whole = pl.BlockSpec(memory_space=pltpu.VMEM)
    in_specs = [pl.BlockSpec((tt, D_MODEL), tok), per_seq, per_seq,
                pl.BlockSpec((1, D_MODEL), lambda i: (0, 0)), whole]
    args = [x2, shift.reshape(batch, 1, D_MODEL), scale.reshape(batch, 1, D_MODEL), gain.reshape(1, D_MODEL),
            w_main.astype(BF16)]
    out_specs = [pl.BlockSpec((tt, m), tok)]
    out_shape = [jax.ShapeDtypeStruct((n, m), F32)]
    if w_gate is not None:
        wg_hi, wg_lo = _split_hi_lo(jnp.pad(w_gate, ((0, 0), (0, LANES - w_gate.shape[1]))))
        in_specs += [whole, whole]
        args += [wg_hi, wg_lo]
        out_specs.append(pl.BlockSpec((tt, LANES), tok))
        out_shape.append(jax.ShapeDtypeStruct((n, LANES), F32))
    outs = pl.pallas_call(
        functools.partial(_norm_proj_kernel, with_gate=w_gate is not None),
        grid=(n // tt,), in_specs=in_specs, out_specs=out_specs, out_shape=out_shape,
        compiler_params=pltpu.CompilerParams(dimension_semantics=("arbitrary",),
                                             vmem_limit_bytes=PROJ_VMEM_LIMIT),
        interpret=interpret, name="norm_proj",
    )(*args)
    return (outs[0], outs[1]) if w_gate is not None else (outs[0], None)


def _out_proj_kernel(h_ref, og_ref, x_ref, g_ref, hg_ref, w_ref, o_ref, *, head_dim):
    h = h_ref[...]
    if head_dim is not None:
        parts = []
        for j in range(D_MODEL // head_dim):
            hb = h[:, j * head_dim:(j + 1) * head_dim]
            parts.append(hb * lax.rsqrt(jnp.mean(hb * hb, axis=-1, keepdims=True) + EPS))
        h = jnp.concatenate(parts, axis=1) * hg_ref[...]
    a = jax.nn.sigmoid(og_ref[...]) * h
    y = jnp.dot(a.astype(BF16), w_ref[...], preferred_element_type=F32)
    o_ref[...] = x_ref[...] + g_ref[0] * y


def _out_proj(h2, p, og_block, x2, g, h_gain, w_out, seq_len, head_dim, *, token_tile=PROJ_TOKEN_TILE,
              interpret=False):
    n = x2.shape[0]
    tt = token_tile
    tiles_per_seq = seq_len // tt
    batch = n // seq_len
    tok = lambda i: (i, 0)
    row = pl.BlockSpec((tt, D_MODEL), tok)
    gain = jnp.ones((1, D_MODEL), F32) if h_gain is None else h_gain.reshape(1, D_MODEL)
    return pl.pallas_call(
        functools.partial(_out_proj_kernel, head_dim=head_dim),
        grid=(n // tt,),
        in_specs=[row, pl.BlockSpec((tt, D_MODEL), lambda i: (i, og_block)), row,
                  pl.BlockSpec((1, 1, D_MODEL), lambda i: (i // tiles_per_seq, 0, 0)),
                  pl.BlockSpec((1, D_MODEL), lambda i: (0, 0)),
                  pl.BlockSpec(memory_space=pltpu.VMEM)],
        out_specs=row,
        out_shape=jax.ShapeDtypeStruct((n, D_MODEL), F32),
        compiler_params=pltpu.CompilerParams(dimension_semantics=("arbitrary",),
                                             vmem_limit_bytes=PROJ_VMEM_LIMIT),
        interpret=interpret, name="out_proj",
    )(h2, p, x2, g.reshape(batch, 1, D_MODEL), gain, w_out.astype(BF16))


MLSTM_STATE_W = MLSTM_V_DIM + LANES


def _mlstm_kernel(q_ref, k_ref, v_ref, gc_ref, gr_ref, h_ref, cn_ref, m_ref):
    H, DQK, DV, L = MLSTM_HEADS, MLSTM_QK_DIM, MLSTM_V_DIM, MLSTM_CHUNK

    @pl.when(pl.program_id(1) == 0)
    def _():
        cn_ref[...] = jnp.zeros_like(cn_ref)
        m_ref[...] = jnp.zeros_like(m_ref)

    tril = lax.broadcasted_iota(jnp.int32, (L, L), 1) <= lax.broadcasted_iota(jnp.int32, (L, L), 0)
    gc = gc_ref[...]
    gr = gr_ref[0]
    ones = jnp.ones((L, LANES), F32)
    for h in range(H):
        ig_col, b_col = gc[:, h:h + 1], gc[:, H + h:H + h + 1]
        ig_row, b_row = gr[h:h + 1, :], gr[H + h:H + h + 1, :]
        m_prev = m_ref[h:h + 1, 0:1]
        d_log = jnp.where(tril, b_col - b_row + ig_row, -jnp.inf)
        inter = b_col + m_prev
        m_t = jnp.maximum(inter, jnp.max(d_log, axis=1, keepdims=True))
        w = jnp.exp(d_log - m_t)
        a_inter = jnp.exp(inter - m_t)
        qh = (q_ref[:, h * DQK:(h + 1) * DQK] * (DQK ** -0.5)).astype(BF16)
        kf = k_ref[:, h * DQK:(h + 1) * DQK]
        vh = v_ref[:, h * DV:(h + 1) * DV]
        s = lax.dot_general(qh, kf.astype(BF16), (((1,), (1,)), ((), ())), preferred_element_type=F32) * w
        cn = cn_ref[h]
        qc = jnp.dot(qh, cn.astype(BF16), preferred_element_type=F32)
        num = a_inter * qc[:, :DV] + jnp.dot(s.astype(BF16), vh.astype(BF16), preferred_element_type=F32)
        den = a_inter * qc[:, DV:DV + 1] + jnp.sum(s, axis=1, keepdims=True)
        h_ref[:, h * DV:(h + 1) * DV] = num / jnp.maximum(jnp.abs(den), jnp.exp(-m_t))
        b_last = b_col[L - 1:L, :]
        g_col = b_last - b_col + ig_col
        m_new = jnp.maximum(b_last + m_prev, jnp.max(g_col, axis=0, keepdims=True))
        w_s = jnp.exp(g_col - m_new)
        decay = jnp.exp(b_last + m_prev - m_new)
        kw = (kf * w_s).astype(BF16)
        vaug = jnp.concatenate([vh, ones], axis=1).astype(BF16)
        cn_ref[h] = decay * cn + lax.dot_general(kw, vaug, (((0,), (0,)), ((), ())), preferred_element_type=F32)
        m_ref[h:h + 1, :] = jnp.broadcast_to(m_new, (1, LANES))


def _mlstm_scan(p, gates, b_i, b_f, batch, seq_len, *, interpret=False):
    H, L = MLSTM_HEADS, MLSTM_CHUNK
    n = p.shape[0]
    nc = seq_len // L
    ig = _softcap(gates[:, :H] + b_i)
    lf = jax.nn.log_sigmoid(_softcap(gates[:, H:2 * H] + b_f))
    b = jnp.cumsum(lf.reshape(n // L, L, H), axis=1).reshape(n, H)
    gc = jnp.concatenate([ig, b], axis=1)
    gr = gc.reshape(n // L, L, 2 * H).transpose(0, 2, 1)
    chunk = lambda bi, c: bi * nc + c
    return pl.pallas_call(
        _mlstm_kernel,
        grid=(batch, nc),
        in_specs=[pl.BlockSpec((L, MLSTM_QK_W), lambda bi, c: (chunk(bi, c), 0)),
                  pl.BlockSpec((L, MLSTM_QK_W), lambda bi, c: (chunk(bi, c), 1)),
                  pl.BlockSpec((L, MLSTM_V_W), lambda bi, c: (chunk(bi, c), 1)),
                  pl.BlockSpec((L, 2 * H), lambda bi, c: (chunk(bi, c), 0)),
                  pl.BlockSpec((1, 2 * H, L), lambda bi, c: (chunk(bi, c), 0, 0))],
        out_specs=pl.BlockSpec((L, D_MODEL), lambda bi, c: (chunk(bi, c), 0)),
        out_shape=jax.ShapeDtypeStruct((n, D_MODEL), F32),
        scratch_shapes=[pltpu.VMEM((H, MLSTM_QK_DIM, MLSTM_STATE_W), F32),
                        pltpu.VMEM((SUBLANES, LANES), F32)],
        compiler_params=pltpu.CompilerParams(dimension_semantics=("arbitrary", "arbitrary")),
        interpret=interpret, name="mlstm_scan",
    )(p, p, p, gc, gr)


FOX_BLOCK = 512
FOX_QUERY_PART = 256
FOX_SUM_ROWS = 16
FOX_VMEM_LIMIT = 48 * 1024 * 1024


def _bias_columns(f, query_side):
    hi = f.astype(BF16).astype(F32)
    r1 = f - hi
    lo = r1.astype(BF16).astype(F32)
    lo2 = r1 - lo
    lane = lax.broadcasted_iota(jnp.int32, (f.shape[0], LANES), 1)
    if query_side:
        vals = jnp.where(lane == 0, hi, jnp.where(lane == 1, lo, jnp.where(lane == 2, lo2,
                         jnp.where(lane < 6, 1.0, 0.0))))
    else:
        vals = jnp.where(lane < 3, 1.0, jnp.where(lane == 3, -hi, jnp.where(lane == 4, -lo,
                         jnp.where(lane == 5, -lo2, 0.0))))
    return vals.astype(BF16)


def _head_column(fc, h):
    lane = lax.broadcasted_iota(jnp.int32, fc.shape, 1)
    return jnp.sum(jnp.where(lane == h, fc, 0.0), axis=1, keepdims=True)


def _rms_rows(t, gain):
    return t * lax.rsqrt(jnp.mean(t * t, axis=-1, keepdims=True) + EPS) * gain


def _fox_attn_kernel(q_ref, k_ref, v_ref, fq_ref, fk_ref, qg_ref, kg_ref, o_ref,
                     ka_ref, vt_ref, qa_ref, st_cur_ref, st_next_ref, *state_refs):
    i = pl.program_id(1)
    h = pl.program_id(0) % FOX_HEADS
    tq = q_ref.shape[0]
    nk = k_ref.shape[0] // tq
    hd = FOX_HEAD_DIM
    part = FOX_QUERY_PART
    n_parts = tq // part
    m_refs, acc_refs = state_refs[:n_parts], state_refs[n_parts:]

    @pl.when(i == 0)
    def _():
        def prep(c, carry):
            r = pl.ds(pl.multiple_of(c * tq, tq), tq)
            ka_ref[r, :hd] = _rms_rows(k_ref[r, :], kg_ref[...]).astype(BF16)
            ka_ref[r, hd:] = _bias_columns(_head_column(fk_ref[r, :], h), False)
            vt_ref[c, :hd, :] = v_ref[r, :].T.astype(BF16)
            vt_ref[c, hd:, :] = jnp.ones((FOX_SUM_ROWS, tq), BF16)
            return carry
        lax.fori_loop(0, nk, prep, 0)

    qn = _rms_rows(q_ref[...], qg_ref[...]) * (hd ** -0.5)
    qa_ref[:, :hd] = qn.astype(BF16)
    qa_ref[:, hd:] = _bias_columns(_head_column(fq_ref[...], h), True)
    for m_ref, acc_ref in zip(m_refs, acc_refs):
        m_ref[...] = jnp.full_like(m_ref, -jnp.inf)
        acc_ref[...] = jnp.zeros_like(acc_ref)

    def scores(j, st_ref):
        kblk = ka_ref[pl.ds(pl.multiple_of(j * tq, tq), tq), :]
        for c in range(n_parts):
            st_ref[c] = lax.dot_general(kblk, qa_ref[c * part:(c + 1) * part, :],
                                        (((1,), (1,)), ((), ())), preferred_element_type=F32)

    def softmax_pv(j, masked):
        vtblk = vt_ref[j]
        pts, alphas = [], []
        for c, m_ref in enumerate(m_refs):
            st = st_cur_ref[c]
            if masked:
                key = lax.broadcasted_iota(jnp.int32, st.shape, 0)
                qry = lax.broadcasted_iota(jnp.int32, st.shape, 1) + c * part
                st = jnp.where(key <= qry, st, -jnp.inf)
            m_prev = m_ref[0:1, :]
            m_new = jnp.maximum(m_prev, jnp.max(st, axis=0, keepdims=True))
            alphas.append(jnp.exp(m_prev - m_new))
            pts.append(jnp.exp(st - m_new).astype(BF16))
            m_ref[0:1, :] = m_new
        for pt, alpha, acc_ref in zip(pts, alphas, acc_refs):
            acc_ref[...] = alpha * acc_ref[...] + jnp.dot(vtblk, pt, preferred_element_type=F32)

    def body(j, carry):
        scores(j + 1, st_next_ref)
        softmax_pv(j, False)
        st_cur_ref[...] = st_next_ref[...]
        return carry

    scores(0, st_cur_ref)
    lax.fori_loop(0, i, body, 0)
    softmax_pv(i, True)
    for c, acc_ref in enumerate(acc_refs):
        acc = acc_ref[...]
        o_ref[c * part:(c + 1) * part, :] = (acc[:hd] / acc[hd:hd + 1]).T


def _fox_attention(pq, kv, f_cum, q_gain, k_gain, batch, seq_len, *, block=FOX_BLOCK, interpret=False):
    n = pq.shape[0]
    hd, nh = FOX_HEAD_DIM, FOX_HEADS
    nq = seq_len // block
    return pl.pallas_call(
        _fox_attn_kernel,
        grid=(batch * nh, nq),
        in_specs=[pl.BlockSpec((block, hd), lambda bh, i: ((bh // nh) * nq + i, bh % nh)),
                  pl.BlockSpec((seq_len, hd), lambda bh, i: (bh // nh, bh % nh)),
                  pl.BlockSpec((seq_len, hd), lambda bh, i: (bh // nh, nh + bh % nh)),
                  pl.BlockSpec((block, nh), lambda bh, i: ((bh // nh) * nq + i, 0)),
                  pl.BlockSpec((seq_len, nh), lambda bh, i: (bh // nh, 0)),
                  pl.BlockSpec((1, hd), lambda bh, i: (0, 0)),
                  pl.BlockSpec((1, hd), lambda bh, i: (0, 0))],
        out_specs=pl.BlockSpec((block, hd), lambda bh, i: ((bh // nh) * nq + i, bh % nh)),
        out_shape=jax.ShapeDtypeStruct((n, D_MODEL), F32),
        scratch_shapes=[pltpu.VMEM((seq_len, 2 * hd), BF16),
                        pltpu.VMEM((nq, hd + FOX_SUM_ROWS, block), BF16),
                        pltpu.VMEM((block, 2 * hd), BF16),
                        *[pltpu.VMEM((block // FOX_QUERY_PART, block, FOX_QUERY_PART), F32)] * 2,
                        *[pltpu.VMEM((SUBLANES, FOX_QUERY_PART), F32)] * (block // FOX_QUERY_PART),
                        *[pltpu.VMEM((hd + FOX_SUM_ROWS, FOX_QUERY_PART), F32)] * (block // FOX_QUERY_PART)],
        compiler_params=pltpu.CompilerParams(dimension_semantics=("arbitrary", "arbitrary"),
                                             vmem_limit_bytes=FOX_VMEM_LIMIT),
        interpret=interpret, name="fox_attention",
    )(pq, kv, kv, f_cum, f_cum, q_gain.reshape(1, hd), k_gain.reshape(1, hd))


def kernel(x, c, ada_w, ada_b, mix_norm, ffn_norm, a_w_in, a_b_i, a_b_f, a_h_norm, a_w_out,
           kv_ada_w, kv_ada_b, kv_norm, kv_w, kv_b_f, kv_k_norm, b_w_qo, b_q_norm, b_w_out,
           peer_wq, peer_subkeys, peer_u, peer_v):
    B, S, D = x.shape
    n = B * S
    cs = jax.nn.silu(c)
    x2 = x.reshape(n, D)
    kv = f_cum = None
    for l in range(DEPTH):
        mod = cs @ ada_w[l] + ada_b[l]
        sh1, sc1, g1, sh2, sc2, g2 = jnp.split(mod, N_ADA, axis=-1)
        if l < N_A_LAYERS:
            split = 2 * MLSTM_QK_W + MLSTM_V_W + D_MODEL
            p, gates = _norm_proj(x2, sh1, sc1, mix_norm[l], a_w_in[l][:, :split], a_w_in[l][:, split:], S)
            h = _mlstm_scan(p, gates, a_b_i[l], a_b_f[l], B, S)
            x2 = _out_proj(h, p, 2, x2, g1, a_h_norm[l], a_w_out[l], S, MLSTM_V_DIM)
        else:
            j = l - N_A_LAYERS
            pq, _ = _norm_proj(x2, sh1, sc1, mix_norm[l], b_w_qo[j], None, S)
            att = _fox_attention(pq, kv, f_cum, b_q_norm[j], kv_k_norm, B, S)
            x2 = _out_proj(att, pq, 1, x2, g1, None, b_w_out[j], S, None)
        y = _peer(x2.reshape(B, S, D), ffn_norm[l], sh2, sc2, peer_wq[l], peer_subkeys[l],
                  _pack_down_table(peer_u[l]), _pack_expert_table(peer_v[l]))
        x2 = x2 + (g2[:, None, :] * y).reshape(n, D)
        if l == N_A_LAYERS - 1:
            sh, sc = jnp.split(cs @ kv_ada_w + kv_ada_b, 2, axis=-1)
            kv, fg = _norm_proj(x2, sh, sc, kv_norm, kv_w[:, :2 * D], kv_w[:, 2 * D:], S)
            log_f = jax.nn.log_sigmoid(fg[:, :FOX_HEADS] + kv_b_f)
            f_cum = jnp.cumsum(log_f.reshape(B, S, FOX_HEADS), axis=1).reshape(n, FOX_HEADS)
    return x2.reshape(B, S, D)
```

```python
import functools
import math

import jax
import jax.numpy as jnp
from jax import lax
from jax.experimental import pallas as pl
from jax.experimental.pallas import tpu as pltpu

F32 = jnp.float32
BF16 = jnp.bfloat16

D_MODEL = 1024
DEPTH = 2
N_A_LAYERS = DEPTH // 2
EPS = 1e-6
N_ADA = 6

MLSTM_HEADS = 4
MLSTM_QK_DIM = D_MODEL // (2 * MLSTM_HEADS)
MLSTM_V_DIM = D_MODEL // MLSTM_HEADS
MLSTM_CHUNK = 64
GATE_SOFTCAP = 15.0
MLSTM_QK_W = MLSTM_HEADS * MLSTM_QK_DIM
MLSTM_V_W = MLSTM_HEADS * MLSTM_V_DIM

FOX_HEADS = 8
FOX_HEAD_DIM = D_MODEL // FOX_HEADS

PEER_HEADS = 8
PEER_KEYS = 128
PEER_EXPERTS = PEER_KEYS * PEER_KEYS
PEER_QUERY_DIM = 256
PEER_HALF = PEER_QUERY_DIM // 2
PEER_TOPK = 16

SUBLANES = 8
LANES = 128
ROW_WORDS = SUBLANES * LANES
assert ROW_WORDS == D_MODEL
PEER_SLOTS = PEER_HEADS * PEER_TOPK
HALF_EXPERTS = PEER_EXPERTS // 2
SUBROWS = 2 * SUBLANES
GATHER_ROWS = PEER_SLOTS * SUBROWS
DOWN_SLAB_ROWS = SUBLANES // 2
DOWN_ROWS = PEER_SLOTS * SUBLANES
TOKEN_GROUP = 2 * SUBLANES
PEER_TOKEN_TILE = 128
PEER_VMEM_LIMIT = 48 * 1024 * 1024


def _pack_expert_table(t):
    lo = lax.bitcast_convert_type(t[:HALF_EXPERTS].astype(BF16), jnp.uint16).astype(jnp.uint32)
    hi = lax.bitcast_convert_type(t[HALF_EXPERTS:].astype(BF16), jnp.uint16).astype(jnp.uint32)
    w = lo | (hi << 16)
    return lax.bitcast_convert_type(w, jnp.int32).reshape(HALF_EXPERTS, SUBLANES, LANES)


def _pack_down_table(t):
    bits = lax.bitcast_convert_type(t.astype(BF16), jnp.uint16).astype(jnp.uint32)
    bits = bits.reshape(t.shape[0], 2, DOWN_SLAB_ROWS, LANES)
    w = bits[:, 0] | (bits[:, 1] << 16)
    return lax.bitcast_convert_type(w, jnp.int32).reshape(t.shape[0] * DOWN_SLAB_ROWS, LANES)


def _slot_expand_matrix():
    slot = lax.broadcasted_iota(jnp.int32, (PEER_SLOTS, GATHER_ROWS), 0)
    sub = lax.broadcasted_iota(jnp.int32, (PEER_SLOTS, GATHER_ROWS), 1)
    return (sub // SUBROWS == slot).astype(BF16)


def _diag_mask():
    r = lax.broadcasted_iota(jnp.int32, (SUBLANES, GATHER_ROWS), 0)
    sub = lax.broadcasted_iota(jnp.int32, (SUBLANES, GATHER_ROWS), 1)
    return (sub % SUBROWS) // 2 == r


def _down_diag_mask():
    j = lax.broadcasted_iota(jnp.int32, (SUBLANES, DOWN_ROWS), 0)
    sub = lax.broadcasted_iota(jnp.int32, (SUBLANES, DOWN_ROWS), 1)
    return sub % SUBLANES == j


def _down_select_matrix():
    sub = lax.broadcasted_iota(jnp.int32, (DOWN_ROWS, PEER_SLOTS), 0)
    pos = lax.broadcasted_iota(jnp.int32, (DOWN_ROWS, PEER_SLOTS), 1)
    return (sub // SUBLANES == pos).astype(F32)


def _half_mask(hs, e):
    hsx = jnp.dot(hs.astype(BF16), e, preferred_element_type=F32)
    par = (lax.broadcasted_iota(jnp.int32, hsx.shape, 1) % 2).astype(F32)
    return hsx == par


def _gather_token(idx_ref, tab_ref, gb_ref, t):
    for s in range(PEER_SLOTS // 2):
        w = idx_ref[t, s]
        a = pl.multiple_of(w & 0xFFFF, SUBLANES)
        b = pl.multiple_of(lax.shift_right_logical(w, 16), SUBLANES)
        gb_ref[pl.ds(2 * s * SUBLANES, SUBLANES), :] = tab_ref[pl.ds(a, SUBLANES), :]
        gb_ref[pl.ds((2 * s + 1) * SUBLANES, SUBLANES), :] = tab_ref[pl.ds(b, SUBLANES), :]
    return pltpu.bitcast(gb_ref[...], BF16)


def _gather_token_down(idx_ref, tab_ref, gb_ref, t):
    for s in range(PEER_SLOTS // 2):
        w = idx_ref[t, s]
        a = pl.multiple_of(w & 0xFFFF, DOWN_SLAB_ROWS)
        b = pl.multiple_of(lax.shift_right_logical(w, 16), DOWN_SLAB_ROWS)
        gb_ref[pl.ds(s * SUBLANES, SUBLANES), :] = jnp.concatenate(
            [tab_ref[pl.ds(a, DOWN_SLAB_ROWS), :], tab_ref[pl.ds(b, DOWN_SLAB_ROWS), :]], axis=0)
    return pltpu.bitcast(gb_ref[...], BF16)


def _split_hi_lo(a):
    hi = a.astype(BF16)
    lo = (a - hi.astype(F32)).astype(BF16)
    return hi, lo


def _down_tokens(g, first, last, idx_ref, xs_ref, tab_ref, gb_ref):
    diag = _down_diag_mask()
    rows = []
    for i in range(first, last):
        t = g * TOKEN_GROUP + i
        gath = _gather_token_down(idx_ref, tab_ref, gb_ref, t)
        y = lax.dot_general(xs_ref[t], gath, (((1,), (1,)), ((), ())),
                            preferred_element_type=F32)
        y8 = y[:SUBLANES] + y[SUBLANES:]
        rows.append(jnp.sum(jnp.where(diag, y8, 0.0), axis=0, keepdims=True))
    return rows


def _store_z(g, rows, z_ref):
    z_ref[pl.ds(pl.multiple_of(g * TOKEN_GROUP, TOKEN_GROUP), TOKEN_GROUP), :] = jnp.concatenate(rows, axis=0)


def _down_epilogue(z_ref, sel_ref, gate_ref, w_ref):
    act = jnp.dot(z_ref[...], sel_ref[...], precision=lax.Precision.HIGHEST, preferred_element_type=F32)
    gelu = 0.5 * act * (1.0 + lax.erf(act * (1.0 / math.sqrt(2.0))))
    w_ref[...] = gate_ref[...] * gelu


def _peer_up_kernel(idx_ref, w_ref, hs_ref, tab_ref, e_ref, y_ref, gb_ref, ahi_ref, alo_ref):
    tt = w_ref.shape[0]
    diag = _diag_mask()
    e = e_ref[...]
    hm = _half_mask(hs_ref[...], e)
    w_hi, w_lo = _split_hi_lo(w_ref[...])
    ahi_ref[...] = jnp.where(hm, jnp.dot(w_hi, e, preferred_element_type=F32), 0.0)
    alo_ref[...] = jnp.where(hm, jnp.dot(w_lo, e, preferred_element_type=F32), 0.0)

    def group(g, carry):
        base = pl.multiple_of(g * TOKEN_GROUP, TOKEN_GROUP)
        a_hi = ahi_ref[pl.ds(base, TOKEN_GROUP), :]
        a_lo = alo_ref[pl.ds(base, TOKEN_GROUP), :]
        for i in range(TOKEN_GROUP):
            t = g * TOKEN_GROUP + i
            gath = _gather_token(idx_ref, tab_ref, gb_ref, t)
            lhs = jnp.concatenate(
                [jnp.where(diag, a_hi[i:i + 1, :], 0.0), jnp.where(diag, a_lo[i:i + 1, :], 0.0)],
                axis=0).astype(BF16)
            out = jnp.dot(lhs, gath, preferred_element_type=F32)
            y_ref[t] = out[:SUBLANES] + out[SUBLANES:]
        return carry

    lax.fori_loop(0, tt // TOKEN_GROUP, group, 0)


def _peer_up(rows_up, w, hs, tab_v, *, token_tile=PEER_TOKEN_TILE, interpret=False):
    n = w.shape[0]
    tt = token_tile
    assert n % tt == 0 and tt % TOKEN_GROUP == 0
    tok = lambda i: (i, 0)
    slot_spec = pl.BlockSpec((tt, PEER_SLOTS), tok)
    y = pl.pallas_call(
        _peer_up_kernel,
        grid=(n // tt,),
        in_specs=[pl.BlockSpec((tt, PEER_SLOTS // 2), tok, memory_space=pltpu.SMEM), slot_spec, slot_spec,
                  pl.BlockSpec(memory_space=pltpu.VMEM),
                  pl.BlockSpec((PEER_SLOTS, GATHER_ROWS), lambda i: (0, 0))],
        out_specs=pl.BlockSpec((tt, SUBLANES, LANES), lambda i: (i, 0, 0)),
        out_shape=jax.ShapeDtypeStruct((n, SUBLANES, LANES), F32),
        scratch_shapes=[pltpu.VMEM((PEER_SLOTS * SUBLANES, LANES), jnp.int32),
                        pltpu.VMEM((tt, GATHER_ROWS), F32),
                        pltpu.VMEM((tt, GATHER_ROWS), F32)],
        compiler_params=pltpu.CompilerParams(dimension_semantics=("arbitrary",),
                                             vmem_limit_bytes=PEER_VMEM_LIMIT),
        interpret=interpret,
        name="peer_up",
    )(rows_up, w, hs, tab_v.reshape(HALF_EXPERTS * SUBLANES, LANES), _slot_expand_matrix())
    return y.reshape(n, D_MODEL)


def _softcap(z):
    return GATE_SOFTCAP * jnp.tanh(z / GATE_SOFTCAP)


def _dot3(a_hi, a_lo, b_hi, b_lo, dims):
    dot = functools.partial(lax.dot_general, dimension_numbers=(dims, ((), ())), preferred_element_type=F32)
    return dot(a_hi, b_hi) + (dot(a_hi, b_lo) + dot(a_lo, b_hi))


def _top_rows(s, k):
    nrows = s.shape[0]
    row = lax.broadcasted_iota(jnp.int32, s.shape, 0)
    vals, rows = [], []
    for _ in range(k):
        m = jnp.max(s, axis=0, keepdims=True)
        r = jnp.min(jnp.where(s == m, row, nrows), axis=0, keepdims=True)
        vals.append(m)
        rows.append(r)
        s = jnp.where(row == r, -jnp.inf, s)
    return jnp.concatenate(vals, axis=0), jnp.concatenate(rows, axis=0)


def _tree(op, xs):
    while len(xs) > 1:
        xs = [op(xs[i], xs[i + 1]) if i + 1 < len(xs) else xs[i] for i in range(0, len(xs), 2)]
    return xs[0]


def _product_candidates(s0, s1, i0, i1):
    k = PEER_TOPK
    t = s0.shape[1]
    sub = lax.broadcasted_iota(jnp.int32, (SUBLANES, t), 0)
    blocks = []
    for half in range(k // SUBLANES):
        b = sub + half * SUBLANES
        lo = half * SUBLANES
        blocks.append((s0[0:1] + s1[lo:lo + SUBLANES], b, i0[0:1] * PEER_KEYS + i1[lo:lo + SUBLANES]))
    for a in range(1, SUBLANES):
        valid = (a + 1) * (sub + 1) <= k
        blocks.append((jnp.where(valid, s0[a:a + 1] + s1[0:SUBLANES], -jnp.inf), a * k + sub,
                       i0[a:a + 1] * PEER_KEYS + i1[0:SUBLANES]))
    a = sub + SUBLANES
    blocks.append((s0[SUBLANES:k] + s1[0:1], a * k, i0[SUBLANES:k] * PEER_KEYS + i1[0:1]))
    return blocks


def _route_prologue(x_ref, shift_ref, scale_ref, gain_ref, xs_ref, xh_ref, xl_ref):
    x = x_ref[...]
    xn = x * lax.rsqrt(jnp.mean(x * x, axis=-1, keepdims=True) + EPS) * gain_ref[...]
    xn = xn * (1.0 + scale_ref[0]) + shift_ref[0]
    x_hi, x_lo = _split_hi_lo(xn)
    for j in range(SUBLANES):
        c = (j % 2) * DOWN_SLAB_ROWS + j // 2
        xs_ref[:, j * LANES:(j + 1) * LANES] = x_hi[:, c * LANES:(c + 1) * LANES]
        xs_ref[:, D_MODEL + j * LANES:D_MODEL + (j + 1) * LANES] = x_lo[:, c * LANES:(c + 1) * LANES]
    xh_ref[...] = x_hi
    xl_ref[...] = x_lo


def _route_query(h, xh_ref, xl_ref, wqh_ref, wql_ref, q_ref):
    q_ref[...] = _dot3(xh_ref[...], xl_ref[...], wqh_ref[h], wql_ref[h], ((1,), (0,)))


def _route_sub_keys(p, skh_ref, skl_ref, q_ref, tv_ref, ti_ref):
    q_hi, q_lo = _split_hi_lo(q_ref[:, p * PEER_HALF:(p + 1) * PEER_HALF])
    s = _dot3(skh_ref[p], skl_ref[p], q_hi, q_lo, ((1,), (1,)))
    tv_ref[p], ti_ref[p] = _top_rows(s, PEER_TOPK)


def _route_head(h, tv_ref, ti_ref, bs_ref, be_ref):
    k = PEER_TOPK
    blocks = _product_candidates(tv_ref[0], tv_ref[1], ti_ref[0], ti_ref[1])
    sums = [b[0] for b in blocks]
    best_s, best_e = [], []
    for _ in range(k):
        m = jnp.max(_tree(jnp.maximum, sums), axis=0, keepdims=True)
        pos = jnp.min(_tree(jnp.minimum, [jnp.where(c == m, b[1], k * k) for c, b in zip(sums, blocks)]),
                      axis=0, keepdims=True)
        hit = [b[1] == pos for b in blocks]
        e = jnp.max(_tree(jnp.maximum, [jnp.where(hh, b[2], -1) for hh, b in zip(hit, blocks)]),
                    axis=0, keepdims=True)
        sums = [jnp.where(hh, -jnp.inf, c) for hh, c in zip(hit, sums)]
        best_s.append(m)
        best_e.append(e)
    bs = jnp.concatenate(best_s, axis=0)
    ex = jnp.exp(bs - bs[0:1])
    off = pl.multiple_of(h * k, k)
    bs_ref[pl.ds(off, k), :] = ex / jnp.sum(ex, axis=0, keepdims=True)
    be_ref[pl.ds(off, k), :] = jnp.concatenate(best_e, axis=0).astype(F32)


def _route_epilogue(bs_ref, be_ref, pos_ref, rows_down_ref, rows_up_ref, hs_ref, gate_ref):
    tt = gate_ref.shape[0]
    half = PEER_SLOTS // 2

    def gather_order(a):
        for c in range(tt // LANES):
            cols = slice(c * LANES, (c + 1) * LANES)
            tile_ref = pos_ref.at[c]
            tile_ref[pl.ds(0, half, stride=2), :] = a[:half, cols]
            tile_ref[pl.ds(1, half, stride=2), :] = a[half:, cols]
        return jnp.concatenate([pos_ref[c].T for c in range(tt // LANES)], axis=0)

    ids = be_ref[...]
    gate_ref[...] = gather_order(bs_ref[...])
    hs_ref[...] = (gather_order(ids).astype(jnp.int32) // HALF_EXPERTS).astype(F32)
    lo = ids.T[:, :half].astype(jnp.int32)
    hi = jnp.concatenate([ids[half:], ids[half:]], axis=0).T[:, :half].astype(jnp.int32)
    up_off = lambda e: (e % HALF_EXPERTS) * SUBLANES
    rows_up_ref[...] = up_off(lo) | (up_off(hi) << 16)
    rows_down_ref[...] = (lo * DOWN_SLAB_ROWS) | ((hi * DOWN_SLAB_ROWS) << 16)


N_ROUTE_IN, N_DOWN_IN, N_ROUTE_OUT, N_ROUTE_SCRATCH = 8, 5, 5, 8


def _route_down_kernel(*refs, do_route, do_down):
    refs = list(refs)
    take = lambda count: [refs.pop(0) for _ in range(count)]
    route_in = take(N_ROUTE_IN) if do_route else None
    down_in = take(N_DOWN_IN) if do_down else None
    route_out = take(N_ROUTE_OUT) if do_route else None
    w_ref = take(1)[0] if do_down else None
    route_scratch = take(N_ROUTE_SCRATCH) if do_route else None
    down_scratch = take(2) if do_down else None

    if do_route:
        x_ref, shift_ref, scale_ref, gain_ref, wqh_ref, wql_ref, skh_ref, skl_ref = route_in
        rows_down_ref, rows_up_ref, hs_ref, gate_ref, xs_ref = route_out
        xh_ref, xl_ref, q_ref, tv_ref, ti_ref, bs_ref, be_ref, pos_ref = route_scratch
        _route_prologue(x_ref, shift_ref, scale_ref, gain_ref, xs_ref, xh_ref, xl_ref)
    if do_down:
        idx_ref, xs_in_ref, gate_in_ref, tab_ref, sel_ref = down_in
        gb_ref, z_ref = down_scratch

    quarter = TOKEN_GROUP // 4

    def step(h, carry):
        rows = []
        if do_route:
            _route_query(h, xh_ref, xl_ref, wqh_ref, wql_ref, q_ref)
        for part in range(4):
            if do_down:
                rows += _down_tokens(h, part * quarter, (part + 1) * quarter, idx_ref, xs_in_ref, tab_ref, gb_ref)
            if do_route and part < 2:
                _route_sub_keys(part, skh_ref, skl_ref, q_ref, tv_ref, ti_ref)
            if do_route and part == 2:
                _route_head(h, tv_ref, ti_ref, bs_ref, be_ref)
        if do_down:
            _store_z(h, rows, z_ref)
        return carry

    lax.fori_loop(0, PEER_HEADS, step, 0)
    if do_route:
        _route_epilogue(bs_ref, be_ref, pos_ref, rows_down_ref, rows_up_ref, hs_ref, gate_ref)
    if do_down:
        _down_epilogue(z_ref, sel_ref, gate_in_ref, w_ref)


ROUTE_DOWN_VMEM_LIMIT = 56 * 1024 * 1024


def _route_down(route_args, down_args, seq_len, *, interpret=False):
    tt = PEER_TOKEN_TILE
    assert tt // TOKEN_GROUP == PEER_HEADS and seq_len % tt == 0
    tiles = seq_len // tt
    tok = lambda i: (i, 0)
    whole = pl.BlockSpec(memory_space=pltpu.VMEM)
    slot_spec = pl.BlockSpec((tt, PEER_SLOTS), tok)
    pair_spec = pl.BlockSpec((tt, PEER_SLOTS // 2), tok)
    slot_shape = jax.ShapeDtypeStruct((seq_len, PEER_SLOTS), F32)
    pair_shape = jax.ShapeDtypeStruct((seq_len, PEER_SLOTS // 2), jnp.int32)
    in_specs, args, out_specs, out_shape, scratch = [], [], [], [], []
    if route_args is not None:
        x2, b, shift, scale, gain, wq_hi, wq_lo, sk_hi, sk_lo = route_args
        batch = shift.shape[0]
        per_seq = pl.BlockSpec((1, 1, D_MODEL), lambda i: (b, 0, 0))
        in_specs += [pl.BlockSpec((tt, D_MODEL), lambda i: (b * tiles + i, 0)), per_seq, per_seq,
                     pl.BlockSpec((1, D_MODEL), lambda i: (0, 0)), whole, whole, whole, whole]
        args += [x2, shift.reshape(batch, 1, D_MODEL), scale.reshape(batch, 1, D_MODEL),
                 gain.reshape(1, D_MODEL), wq_hi, wq_lo, sk_hi, sk_lo]
    if down_args is not None:
        rows_down, xs, gate, tab_u = down_args
        in_specs += [pl.BlockSpec((tt, PEER_SLOTS // 2), tok, memory_space=pltpu.SMEM),
                     pl.BlockSpec((tt, SUBROWS, LANES), lambda i: (i, 0, 0)), slot_spec, whole,
                     pl.BlockSpec((DOWN_ROWS, PEER_SLOTS), lambda i: (0, 0))]
        args += [rows_down, xs.reshape(seq_len, SUBROWS, LANES), gate, tab_u, _down_select_matrix()]
    if route_args is not None:
        out_specs += [pair_spec, pair_spec, slot_spec, slot_spec, pl.BlockSpec((tt, 2 * D_MODEL), tok)]
        out_shape += [pair_shape, pair_shape, slot_shape, slot_shape,
                      jax.ShapeDtypeStruct((seq_len, 2 * D_MODEL), BF16)]
    if down_args is not None:
        out_specs.append(slot_spec)
        out_shape.append(slot_shape)
    if route_args is not None:
        scratch += [pltpu.VMEM((tt, D_MODEL), BF16),
                    pltpu.VMEM((tt, D_MODEL), BF16),
                    pltpu.VMEM((tt, PEER_QUERY_DIM), F32),
                    pltpu.VMEM((2, PEER_TOPK, tt), F32),
                    pltpu.VMEM((2, PEER_TOPK, tt), jnp.int32),
                    pltpu.VMEM((PEER_SLOTS, tt), F32),
                    pltpu.VMEM((PEER_SLOTS, tt), F32),
                    pltpu.VMEM((tt // LANES, PEER_SLOTS, LANES), F32)]
    if down_args is not None:
        scratch += [pltpu.VMEM((PEER_SLOTS // 2 * SUBLANES, LANES), jnp.int32),
                    pltpu.VMEM((tt, DOWN_ROWS), F32)]
    outs = pl.pallas_call(
        functools.partial(_route_down_kernel, do_route=route_args is not None, do_down=down_args is not None),
        grid=(tiles,), in_specs=in_specs, out_specs=out_specs, out_shape=out_shape, scratch_shapes=scratch,
        compiler_params=pltpu.CompilerParams(dimension_semantics=("arbitrary",),
                                             vmem_limit_bytes=ROUTE_DOWN_VMEM_LIMIT),
        interpret=interpret, name="peer_route_down",
    )(*args)
    routed = tuple(outs[:N_ROUTE_OUT]) if route_args is not None else None
    w = outs[-1] if down_args is not None else None
    return routed, w


def _peer(x, gain, shift, scale, wq, subkeys, tab_u, tab_v):
    B, S, D = x.shape
    x2 = x.reshape(B * S, D)
    per_head = lambda w: w.reshape(D, PEER_HEADS, PEER_QUERY_DIM).transpose(1, 0, 2)
    wq_hi, wq_lo = _split_hi_lo(per_head(wq))
    sk_hi, sk_lo = _split_hi_lo(subkeys)
    routed, rows_up, hs, ws = None, [], [], []
    for b in range(B + 1):
        route_args = (x2, b, shift, scale, gain, wq_hi, wq_lo, sk_hi, sk_lo) if b < B else None
        down_args = (routed[0], routed[4], routed[3], tab_u) if routed is not None else None
        routed, w = _route_down(route_args, down_args, S)
        if w is not None:
            ws.append(w)
        if routed is not None:
            rows_up.append(routed[1])
            hs.append(routed[2])
    y = _peer_up(jnp.concatenate(rows_up), jnp.concatenate(ws), jnp.concatenate(hs), tab_v)
    return y.reshape(B, S, D)


PROJ_TOKEN_TILE = 512
PROJ_VMEM_LIMIT = 48 * 1024 * 1024


def _ada_norm_tile(x, gain, shift, scale):
    xn = x * lax.rsqrt(jnp.mean(x * x, axis=-1, keepdims=True) + EPS) * gain
    return xn * (1.0 + scale) + shift


def _norm_proj_kernel(x_ref, shift_ref, scale_ref, gain_ref, w_ref, *rest, with_gate):
    x_hi, x_lo = _split_hi_lo(_ada_norm_tile(x_ref[...], gain_ref[...], shift_ref[0], scale_ref[0]))
    if with_gate:
        wgh_ref, wgl_ref, main_ref, gate_ref = rest
        gate_ref[...] = _dot3(x_hi, x_lo, wgh_ref[...], wgl_ref[...], ((1,), (0,)))
    else:
        (main_ref,) = rest
    main_ref[...] = jnp.dot(x_hi, w_ref[...], preferred_element_type=F32)


def _norm_proj(x2, shift, scale, gain, w_main, w_gate, seq_len, *, token_tile=PROJ_TOKEN_TILE, interpret=False):
    n = x2.shape[0]
    tt = token_tile
    assert seq_len % tt == 0 and n % seq_len == 0
    tiles_per_seq = seq_len // tt
    batch = n // seq_len
    m = w_main.shape[1]
    tok = lambda i: (i, 0)
    per_seq = pl.BlockSpec((1, 1, D_MODEL), lambda i: (i // tiles_per_seq, 0, 0))
    whole = pl.BlockSpec(memory_space=pltpu.VMEM)
    in_specs = [pl.BlockSpec((tt, D_MODEL), tok), per_seq, per_seq,
                pl.BlockSpec((1, D_MODEL), lambda i: (0, 0)), whole]
    args = [x2, shift.reshape(batch, 1, D_MODEL), scale.reshape(batch, 1, D_MODEL), gain.reshape(1, D_MODEL),
            w_main.astype(BF16)]
    out_specs = [pl.BlockSpec((tt, m), tok)]
    out_shape = [jax.ShapeDtypeStruct((n, m), F32)]
    if w_gate is not None:
        wg_hi, wg_lo = _split_hi_lo(jnp.pad(w_gate, ((0, 0), (0, LANES - w_gate.shape[1]))))
        in_specs += [whole, whole]
        args += [wg_hi, wg_lo]
        out_specs.append(pl.BlockSpec((tt, LANES), tok))
        out_shape.append(jax.ShapeDtypeStruct((n, LANES), F32))
    outs = pl.pallas_call(
        functools.partial(_norm_proj_kernel, with_gate=w_gate is not None),
        grid=(n // tt,), in_specs=in_specs, out_specs=out_specs, out_shape=out_shape,
        compiler_params=pltpu.CompilerParams(dimension_semantics=("arbitrary",),
                                             vmem_limit_bytes=PROJ_VMEM_LIMIT),
        interpret=interpret, name="norm_proj",
    )(*args)
    return (outs[0], outs[1]) if w_gate is not None else (outs[0], None)


def _out_proj_kernel(h_ref, og_ref, x_ref, g_ref, hg_ref, w_ref, o_ref, *, head_dim):
    h = h_ref[...]
    if head_dim is not None:
        parts = []
        for j in range(D_MODEL // head_dim):
            hb = h[:, j * head_dim:(j + 1) * head_dim]
            parts.append(hb * lax.rsqrt(jnp.mean(hb * hb, axis=-1, keepdims=True) + EPS))
        h = jnp.concatenate(parts, axis=1) * hg_ref[...]
    a = jax.nn.sigmoid(og_ref[...]) * h
    y = jnp.dot(a.astype(BF16), w_ref[...], preferred_element_type=F32)
    o_ref[...] = x_ref[...] + g_ref[0] * y


def _out_proj(h2, p, og_block, x2, g, h_gain, w_out, seq_len, head_dim, *, token_tile=PROJ_TOKEN_TILE,
              interpret=False):
    n = x2.shape[0]
    tt = token_tile
    tiles_per_seq = seq_len // tt
    batch = n // seq_len
    tok = lambda i: (i, 0)
    row = pl.BlockSpec((tt, D_MODEL), tok)
    gain = jnp.ones((1, D_MODEL), F32) if h_gain is None else h_gain.reshape(1, D_MODEL)
    return pl.pallas_call(
        functools.partial(_out_proj_kernel, head_dim=head_dim),
        grid=(n // tt,),
        in_specs=[row, pl.BlockSpec((tt, D_MODEL), lambda i: (i, og_block)), row,
                  pl.BlockSpec((1, 1, D_MODEL), lambda i: (i // tiles_per_seq, 0, 0)),
                  pl.BlockSpec((1, D_MODEL), lambda i: (0, 0)),
                  pl.BlockSpec(memory_space=pltpu.VMEM)],
        out_specs=row,
        out_shape=jax.ShapeDtypeStruct((n, D_MODEL), F32),
        compiler_params=pltpu.CompilerParams(dimension_semantics=("arbitrary",),
                                             vmem_limit_bytes=PROJ_VMEM_LIMIT),
        interpret=interpret, name="out_proj",
    )(h2, p, x2, g.reshape(batch, 1, D_MODEL), gain, w_out.astype(BF16))


MLSTM_STATE_W = MLSTM_V_DIM + LANES


def _mlstm_kernel(q_ref, k_ref, v_ref, gc_ref, gr_ref, h_ref, cn_ref, m_ref):
    H, DQK, DV, L = MLSTM_HEADS, MLSTM_QK_DIM, MLSTM_V_DIM, MLSTM_CHUNK

    @pl.when(pl.program_id(1) == 0)
    def _():
        cn_ref[...] = jnp.zeros_like(cn_ref)
        m_ref[...] = jnp.zeros_like(m_ref)

    tril = lax.broadcasted_iota(jnp.int32, (L, L), 1) <= lax.broadcasted_iota(jnp.int32, (L, L), 0)
    gc = gc_ref[...]
    gr = gr_ref[0]
    ones = jnp.ones((L, LANES), F32)
    for h in range(H):
        ig_col, b_col = gc[:, h:h + 1], gc[:, H + h:H + h + 1]
        ig_row, b_row = gr[h:h + 1, :], gr[H + h:H + h + 1, :]
        m_prev = m_ref[h:h + 1, 0:1]
        d_log = jnp.where(tril, b_col - b_row + ig_row, -jnp.inf)
        inter = b_col + m_prev
        m_t = jnp.maximum(inter, jnp.max(d_log, axis=1, keepdims=True))
        w = jnp.exp(d_log - m_t)
        a_inter = jnp.exp(inter - m_t)
        qh = (q_ref[:, h * DQK:(h + 1) * DQK] * (DQK ** -0.5)).astype(BF16)
        kf = k_ref[:, h * DQK:(h + 1) * DQK]
        vh = v_ref[:, h * DV:(h + 1) * DV]
        s = lax.dot_general(qh, kf.astype(BF16), (((1,), (1,)), ((), ())), preferred_element_type=F32) * w
        cn = cn_ref[h]
        qc = jnp.dot(qh, cn.astype(BF16), preferred_element_type=F32)
        num = a_inter * qc[:, :DV] + jnp.dot(s.astype(BF16), vh.astype(BF16), preferred_element_type=F32)
        den = a_inter * qc[:, DV:DV + 1] + jnp.sum(s, axis=1, keepdims=True)
        h_ref[:, h * DV:(h + 1) * DV] = num / jnp.maximum(jnp.abs(den), jnp.exp(-m_t))
        b_last = b_col[L - 1:L, :]
        g_col = b_last - b_col + ig_col
        m_new = jnp.maximum(b_last + m_prev, jnp.max(g_col, axis=0, keepdims=True))
        w_s = jnp.exp(g_col - m_new)
        decay = jnp.exp(b_last + m_prev - m_new)
        kw = (kf * w_s).astype(BF16)
        vaug = jnp.concatenate([vh, ones], axis=1).astype(BF16)
        cn_ref[h] = decay * cn + lax.dot_general(kw, vaug, (((0,), (0,)), ((), ())), preferred_element_type=F32)
        m_ref[h:h + 1, :] = jnp.broadcast_to(m_new, (1, LANES))


def _mlstm_scan(p, gates, b_i, b_f, batch, seq_len, *, interpret=False):
    H, L = MLSTM_HEADS, MLSTM_CHUNK
    n = p.shape[0]
    nc = seq_len // L
    ig = _softcap(gates[:, :H] + b_i)
    lf = jax.nn.log_sigmoid(_softcap(gates[:, H:2 * H] + b_f))
    b = jnp.cumsum(lf.reshape(n // L, L, H), axis=1).reshape(n, H)
    gc = jnp.concatenate([ig, b], axis=1)
    gr = gc.reshape(n // L, L, 2 * H).transpose(0, 2, 1)
    chunk = lambda bi, c: bi * nc + c
    return pl.pallas_call(
        _mlstm_kernel,
        grid=(batch, nc),
        in_specs=[pl.BlockSpec((L, MLSTM_QK_W), lambda bi, c: (chunk(bi, c), 0)),
                  pl.BlockSpec((L, MLSTM_QK_W), lambda bi, c: (chunk(bi, c), 1)),
                  pl.BlockSpec((L, MLSTM_V_W), lambda bi, c: (chunk(bi, c), 1)),
                  pl.BlockSpec((L, 2 * H), lambda bi, c: (chunk(bi, c), 0)),
                  pl.BlockSpec((1, 2 * H, L), lambda bi, c: (chunk(bi, c), 0, 0))],
        out_specs=pl.BlockSpec((L, D_MODEL), lambda bi, c: (chunk(bi, c), 0)),
        out_shape=jax.ShapeDtypeStruct((n, D_MODEL), F32),
        scratch_shapes=[pltpu.VMEM((H, MLSTM_QK_DIM, MLSTM_STATE_W), F32),
                        pltpu.VMEM((SUBLANES, LANES), F32)],
        compiler_params=pltpu.CompilerParams(dimension_semantics=("arbitrary", "arbitrary")),
        interpret=interpret, name="mlstm_scan",
    )(p, p, p, gc, gr)


FOX_BLOCK = 512
FOX_QUERY_PART = 256
FOX_SUM_ROWS = 16
FOX_VMEM_LIMIT = 48 * 1024 * 1024


def _bias_columns(f, query_side):
    hi = f.astype(BF16).astype(F32)
    r1 = f - hi
    lo = r1.astype(BF16).astype(F32)
    lo2 = r1 - lo
    lane = lax.broadcasted_iota(jnp.int32, (f.shape[0], LANES), 1)
    if query_side:
        vals = jnp.where(lane == 0, hi, jnp.where(lane == 1, lo, jnp.where(lane == 2, lo2,
                         jnp.where(lane < 6, 1.0, 0.0))))
    else:
        vals = jnp.where(lane < 3, 1.0, jnp.where(lane == 3, -hi, jnp.where(lane == 4, -lo,
                         jnp.where(lane == 5, -lo2, 0.0))))
    return vals.astype(BF16)


def _head_column(fc, h):
    lane = lax.broadcasted_iota(jnp.int32, fc.shape, 1)
    return jnp.sum(jnp.where(lane == h, fc, 0.0), axis=1, keepdims=True)


def _rms_rows(t, gain):
    return t * lax.rsqrt(jnp.mean(t * t, axis=-1, keepdims=True) + EPS) * gain


def _fox_attn_kernel(q_ref, k_ref, v_ref, fq_ref, fk_ref, qg_ref, kg_ref, o_ref,
                     ka_ref, vt_ref, qa_ref, st_cur_ref, st_next_ref, *state_refs):
    i = pl.program_id(1)
    h = pl.program_id(0) % FOX_HEADS
    tq = q_ref.shape[0]
    nk = k_ref.shape[0] // tq
    hd = FOX_HEAD_DIM
    part = FOX_QUERY_PART
    n_parts = tq // part
    m_refs, acc_refs = state_refs[:n_parts], state_refs[n_parts:]

    @pl.when(i == 0)
    def _():
        def prep(c, carry):
            r = pl.ds(pl.multiple_of(c * tq, tq), tq)
            ka_ref[r, :hd] = _rms_rows(k_ref[r, :], kg_ref[...]).astype(BF16)
            ka_ref[r, hd:] = _bias_columns(_head_column(fk_ref[r, :], h), False)
            vt_ref[c, :hd, :] = v_ref[r, :].T.astype(BF16)
            vt_ref[c, hd:, :] = jnp.ones((FOX_SUM_ROWS, tq), BF16)
            return carry
        lax.fori_loop(0, nk, prep, 0)

    qn = _rms_rows(q_ref[...], qg_ref[...]) * (hd ** -0.5)
    qa_ref[:, :hd] = qn.astype(BF16)
    qa_ref[:, hd:] = _bias_columns(_head_column(fq_ref[...], h), True)
    for m_ref, acc_ref in zip(m_refs, acc_refs):
        m_ref[...] = jnp.full_like(m_ref, -jnp.inf)
        acc_ref[...] = jnp.zeros_like(acc_ref)

    def scores(j, st_ref):
        kblk = ka_ref[pl.ds(pl.multiple_of(j * tq, tq), tq), :]
        for c in range(n_parts):
            st_ref[c] = lax.dot_general(kblk, qa_ref[c * part:(c + 1) * part, :],
                                        (((1,), (1,)), ((), ())), preferred_element_type=F32)

    def softmax_pv(j, masked):
        vtblk = vt_ref[j]
        pts, alphas = [], []
        for c, m_ref in enumerate(m_refs):
            st = st_cur_ref[c]
            if masked:
                key = lax.broadcasted_iota(jnp.int32, st.shape, 0)
                qry = lax.broadcasted_iota(jnp.int32, st.shape, 1) + c * part
                st = jnp.where(key <= qry, st, -jnp.inf)
            m_prev = m_ref[0:1, :]
            m_new = jnp.maximum(m_prev, jnp.max(st, axis=0, keepdims=True))
            alphas.append(jnp.exp(m_prev - m_new))
            pts.append(jnp.exp(st - m_new).astype(BF16))
            m_ref[0:1, :] = m_new
        for pt, alpha, acc_ref in zip(pts, alphas, acc_refs):
            acc_ref[...] = alpha * acc_ref[...] + jnp.dot(vtblk, pt, preferred_element_type=F32)

    def body(j, carry):
        scores(j + 1, st_next_ref)
        softmax_pv(j, False)
        st_cur_ref[...] = st_next_ref[...]
        return carry

    scores(0, st_cur_ref)
    lax.fori_loop(0, i, body, 0)
    softmax_pv(i, True)
    for c, acc_ref in enumerate(acc_refs):
        acc = acc_ref[...]
        o_ref[c * part:(c + 1) * part, :] = (acc[:hd] / acc[hd:hd + 1]).T


def _fox_attention(pq, kv, f_cum, q_gain, k_gain, batch, seq_len, *, block=FOX_BLOCK, interpret=False):
    n = pq.shape[0]
    hd, nh = FOX_HEAD_DIM, FOX_HEADS
    nq = seq_len // block
    return pl.pallas_call(
        _fox_attn_kernel,
        grid=(batch * nh, nq),
        in_specs=[pl.BlockSpec((block, hd), lambda bh, i: ((bh // nh) * nq + i, bh % nh)),
                  pl.BlockSpec((seq_len, hd), lambda bh, i: (bh // nh, bh % nh)),
                  pl.BlockSpec((seq_len, hd), lambda bh, i: (bh // nh, nh + bh % nh)),
                  pl.BlockSpec((block, nh), lambda bh, i: ((bh // nh) * nq + i, 0)),
                  pl.BlockSpec((seq_len, nh), lambda bh, i: (bh // nh, 0)),
                  pl.BlockSpec((1, hd), lambda bh, i: (0, 0)),
                  pl.BlockSpec((1, hd), lambda bh, i: (0, 0))],
        out_specs=pl.BlockSpec((block, hd), lambda bh, i: ((bh // nh) * nq + i, bh % nh)),
        out_shape=jax.ShapeDtypeStruct((n, D_MODEL), F32),
        scratch_shapes=[pltpu.VMEM((seq_len, 2 * hd), BF16),
                        pltpu.VMEM((nq, hd + FOX_SUM_ROWS, block), BF16),
                        pltpu.VMEM((block, 2 * hd), BF16),
                        *[pltpu.VMEM((block // FOX_QUERY_PART, block, FOX_QUERY_PART), F32)] * 2,
                        *[pltpu.VMEM((SUBLANES, FOX_QUERY_PART), F32)] * (block // FOX_QUERY_PART),
                        *[pltpu.VMEM((hd + FOX_SUM_ROWS, FOX_QUERY_PART), F32)] * (block // FOX_QUERY_PART)],
        compiler_params=pltpu.CompilerParams(dimension_semantics=("arbitrary", "arbitrary"),
                                             vmem_limit_bytes=FOX_VMEM_LIMIT),
        interpret=interpret, name="fox_attention",
    )(pq, kv, kv, f_cum, f_cum, q_gain.reshape(1, hd), k_gain.reshape(1, hd))


def kernel(x, c, ada_w, ada_b, mix_norm, ffn_norm, a_w_in, a_b_i, a_b_f, a_h_norm, a_w_out,
           kv_ada_w, kv_ada_b, kv_norm, kv_w, kv_b_f, kv_k_norm, b_w_qo, b_q_norm, b_w_out,
           peer_wq, peer_subkeys, peer_u, peer_v):
    B, S, D = x.shape
    n = B * S
    cs = jax.nn.silu(c)
    x2 = x.reshape(n, D)
    kv = f_cum = None
    for l in range(DEPTH):
        mod = cs @ ada_w[l] + ada_b[l]
        sh1, sc1, g1, sh2, sc2, g2 = jnp.split(mod, N_ADA, axis=-1)
        if l < N_A_LAYERS:
            split = 2 * MLSTM_QK_W + MLSTM_V_W + D_MODEL
            p, gates = _norm_proj(x2, sh1, sc1, mix_norm[l], a_w_in[l][:, :split], a_w_in[l][:, split:], S)
            h = _mlstm_scan(p, gates, a_b_i[l], a_b_f[l], B, S)
            x2 = _out_proj(h, p, 2, x2, g1, a_h_norm[l], a_w_out[l], S, MLSTM_V_DIM)
        else:
            j = l - N_A_LAYERS
            pq, _ = _norm_proj(x2, sh1, sc1, mix_norm[l], b_w_qo[j], None, S)
            att = _fox_attention(pq, kv, f_cum, b_q_norm[j], kv_k_norm, B, S)
            x2 = _out_proj(att, pq, 1, x2, g1, None, b_w_out[j], S, None)
        y = _peer(x2.reshape(B, S, D), ffn_norm[l], sh2, sc2, peer_wq[l], peer_subkeys[l],
                  _pack_down_table(peer_u[l]), _pack_expert_table(peer_v[l]))
        x2 = x2 + (g2[:, None, :] * y).reshape(n, D)
        if l == N_A_LAYERS - 1:
            sh, sc = jnp.split(cs @ kv_ada_w + kv_ada_b, 2, axis=-1)
            kv, fg = _norm_proj(x2, sh, sc, kv_norm, kv_w[:, :2 * D], kv_w[:, 2 * D:], S)
            log_f = jax.nn.log_sigmoid(fg[:, :FOX_HEADS] + kv_b_f)
            f_cum = jnp.cumsum(log_f.reshape(B, S, FOX_HEADS), axis=1).reshape(n, FOX_HEADS)
    return x2.reshape(B, S, D)
```

```python
import functools
import math

import jax
import jax.numpy as jnp
from jax import lax
from jax.experimental import pallas as pl
from jax.experimental.pallas import tpu as pltpu

F32 = jnp.float32
BF16 = jnp.bfloat16

D_MODEL = 1024
DEPTH = 2
N_A_LAYERS = DEPTH // 2
EPS = 1e-6
N_ADA = 6

MLSTM_HEADS = 4
MLSTM_QK_DIM = D_MODEL // (2 * MLSTM_HEADS)
MLSTM_V_DIM = D_MODEL // MLSTM_HEADS
MLSTM_CHUNK = 64
GATE_SOFTCAP = 15.0
MLSTM_QK_W = MLSTM_HEADS * MLSTM_QK_DIM
MLSTM_V_W = MLSTM_HEADS * MLSTM_V_DIM

FOX_HEADS = 8
FOX_HEAD_DIM = D_MODEL // FOX_HEADS

PEER_HEADS = 8
PEER_KEYS = 128
PEER_EXPERTS = PEER_KEYS * PEER_KEYS
PEER_QUERY_DIM = 256
PEER_HALF = PEER_QUERY_DIM // 2
PEER_TOPK = 16

SUBLANES = 8
LANES = 128
ROW_WORDS = SUBLANES * LANES
assert ROW_WORDS == D_MODEL
PEER_SLOTS = PEER_HEADS * PEER_TOPK
HALF_EXPERTS = PEER_EXPERTS // 2
SUBROWS = 2 * SUBLANES
GATHER_ROWS = PEER_SLOTS * SUBROWS
DOWN_SLAB_ROWS = SUBLANES // 2
DOWN_ROWS = PEER_SLOTS * SUBLANES
TOKEN_GROUP = 2 * SUBLANES
PEER_TOKEN_TILE = 128
PEER_VMEM_LIMIT = 48 * 1024 * 1024


def _pack_expert_table(t):
    lo = lax.bitcast_convert_type(t[:HALF_EXPERTS].astype(BF16), jnp.uint16).astype(jnp.uint32)
    hi = lax.bitcast_convert_type(t[HALF_EXPERTS:].astype(BF16), jnp.uint16).astype(jnp.uint32)
    w = lo | (hi << 16)
    return lax.bitcast_convert_type(w, jnp.int32).reshape(HALF_EXPERTS, SUBLANES, LANES)


def _pack_down_table(t):
    bits = lax.bitcast_convert_type(t.astype(BF16), jnp.uint16).astype(jnp.uint32)
    bits = bits.reshape(t.shape[0], 2, DOWN_SLAB_ROWS, LANES)
    w = bits[:, 0] | (bits[:, 1] << 16)
    return lax.bitcast_convert_type(w, jnp.int32).reshape(t.shape[0] * DOWN_SLAB_ROWS, LANES)


def _slot_expand_matrix():
    slot = lax.broadcasted_iota(jnp.int32, (PEER_SLOTS, GATHER_ROWS), 0)
    sub = lax.broadcasted_iota(jnp.int32, (PEER_SLOTS, GATHER_ROWS), 1)
    return (sub // SUBROWS == slot).astype(BF16)


def _diag_mask():
    r = lax.broadcasted_iota(jnp.int32, (SUBLANES, GATHER_ROWS), 0)
    sub = lax.broadcasted_iota(jnp.int32, (SUBLANES, GATHER_ROWS), 1)
    return (sub % SUBROWS) // 2 == r


def _down_diag_mask():
    j = lax.broadcasted_iota(jnp.int32, (SUBLANES, DOWN_ROWS), 0)
    sub = lax.broadcasted_iota(jnp.int32, (SUBLANES, DOWN_ROWS), 1)
    return sub % SUBLANES == j


def _down_select_matrix():
    sub = lax.broadcasted_iota(jnp.int32, (DOWN_ROWS, PEER_SLOTS), 0)
    pos = lax.broadcasted_iota(jnp.int32, (DOWN_ROWS, PEER_SLOTS), 1)
    return (sub // SUBLANES == pos).astype(F32)


def _half_mask(hs, e):
    hsx = jnp.dot(hs.astype(BF16), e, preferred_element_type=F32)
    par = (lax.broadcasted_iota(jnp.int32, hsx.shape, 1) % 2).astype(F32)
    return hsx == par


def _gather_token(idx_ref, tab_ref, gb_ref, t):
    for s in range(PEER_SLOTS // 2):
        w = idx_ref[t, s]
        a = pl.multiple_of(w & 0xFFFF, SUBLANES)
        b = pl.multiple_of(lax.shift_right_logical(w, 16), SUBLANES)
        gb_ref[pl.ds(2 * s * SUBLANES, SUBLANES), :] = tab_ref[pl.ds(a, SUBLANES), :]
        gb_ref[pl.ds((2 * s + 1) * SUBLANES, SUBLANES), :] = tab_ref[pl.ds(b, SUBLANES), :]
    return pltpu.bitcast(gb_ref[...], BF16)


def _gather_token_down(idx_ref, tab_ref, gb_ref, t):
    for s in range(PEER_SLOTS // 2):
        w = idx_ref[t, s]
        a = pl.multiple_of(w & 0xFFFF, DOWN_SLAB_ROWS)
        b = pl.multiple_of(lax.shift_right_logical(w, 16), DOWN_SLAB_ROWS)
        gb_ref[pl.ds(s * SUBLANES, SUBLANES), :] = jnp.concatenate(
            [tab_ref[pl.ds(a, DOWN_SLAB_ROWS), :], tab_ref[pl.ds(b, DOWN_SLAB_ROWS), :]], axis=0)
    return pltpu.bitcast(gb_ref[...], BF16)


def _split_hi_lo(a):
    hi = a.astype(BF16)
    lo = (a - hi.astype(F32)).astype(BF16)
    return hi, lo


def _down_tokens(g, first, last, idx_ref, xs_ref, tab_ref, gb_ref):
    diag = _down_diag_mask()
    rows = []
    for i in range(first, last):
        t = g * TOKEN_GROUP + i
        gath = _gather_token_down(idx_ref, tab_ref, gb_ref, t)
        y = lax.dot_general(xs_ref[t], gath, (((1,), (1,)), ((), ())),
                            preferred_element_type=F32)
        y8 = y[:SUBLANES] + y[SUBLANES:]
        rows.append(jnp.sum(jnp.where(diag, y8, 0.0), axis=0, keepdims=True))
    return rows


def _store_z(g, rows, z_ref):
    z_ref[pl.ds(pl.multiple_of(g * TOKEN_GROUP, TOKEN_GROUP), TOKEN_GROUP), :] = jnp.concatenate(rows, axis=0)


def _down_epilogue(z_ref, sel_ref, gate_ref, w_ref):
    act = jnp.dot(z_ref[...], sel_ref[...], precision=lax.Precision.HIGHEST, preferred_element_type=F32)
    gelu = 0.5 * act * (1.0 + lax.erf(act * (1.0 / math.sqrt(2.0))))
    w_ref[...] = gate_ref[...] * gelu


def _peer_up_kernel(idx_ref, w_ref, hs_ref, x_ref, g_ref, tab_ref, e_ref, y_ref, gb_ref, ahi_ref, alo_ref):
    tt = w_ref.shape[0]
    diag = _diag_mask()
    e = e_ref[...]
    hm = _half_mask(hs_ref[...], e)
    w_hi, w_lo = _split_hi_lo(w_ref[...])
    ahi_ref[...] = jnp.where(hm, jnp.dot(w_hi, e, preferred_element_type=F32), 0.0)
    alo_ref[...] = jnp.where(hm, jnp.dot(w_lo, e, preferred_element_type=F32), 0.0)

    def group(g, carry):
        base = pl.multiple_of(g * TOKEN_GROUP, TOKEN_GROUP)
        a_hi = ahi_ref[pl.ds(base, TOKEN_GROUP), :]
        a_lo = alo_ref[pl.ds(base, TOKEN_GROUP), :]
        outs = []
        for i in range(TOKEN_GROUP):
            t = g * TOKEN_GROUP + i
            gath = _gather_token(idx_ref, tab_ref, gb_ref, t)
            lhs = jnp.concatenate(
                [jnp.where(diag, a_hi[i:i + 1, :], 0.0), jnp.where(diag, a_lo[i:i + 1, :], 0.0)],
                axis=0).astype(BF16)
            out = jnp.dot(lhs, gath, preferred_element_type=F32)
            outs.append(out[:SUBLANES] + out[SUBLANES:])
        rows = pl.ds(base, TOKEN_GROUP)
        for r in range(SUBLANES):
            cols = slice(r * LANES, (r + 1) * LANES)
            chunk = jnp.concatenate([o[r:r + 1, :] for o in outs], axis=0)
            y_ref[rows, cols] = x_ref[rows, cols] + g_ref[0][:, cols] * chunk
        return carry

    lax.fori_loop(0, tt // TOKEN_GROUP, group, 0)


def _peer_up(rows_up, w, hs, x2, g, tab_v, seq_len, *, token_tile=PEER_TOKEN_TILE, interpret=False):
    n = w.shape[0]
    tt = token_tile
    assert n % tt == 0 and tt % TOKEN_GROUP == 0 and seq_len % tt == 0
    tiles_per_seq = seq_len // tt
    batch = n // seq_len
    tok = lambda i: (i, 0)
    slot_spec = pl.BlockSpec((tt, PEER_SLOTS), tok)
    row_spec = pl.BlockSpec((tt, D_MODEL), tok)
    return pl.pallas_call(
        _peer_up_kernel,
        grid=(n // tt,),
        in_specs=[pl.BlockSpec((tt, PEER_SLOTS // 2), tok, memory_space=pltpu.SMEM), slot_spec, slot_spec,
                  row_spec, pl.BlockSpec((1, 1, D_MODEL), lambda i: (i // tiles_per_seq, 0, 0)),
                  pl.BlockSpec(memory_space=pltpu.VMEM),
                  pl.BlockSpec((PEER_SLOTS, GATHER_ROWS), lambda i: (0, 0))],
        out_specs=row_spec,
        out_shape=jax.ShapeDtypeStruct((n, D_MODEL), F32),
        scratch_shapes=[pltpu.VMEM((PEER_SLOTS * SUBLANES, LANES), jnp.int32),
                        pltpu.VMEM((tt, GATHER_ROWS), F32),
                        pltpu.VMEM((tt, GATHER_ROWS), F32)],
        compiler_params=pltpu.CompilerParams(dimension_semantics=("arbitrary",),
                                             vmem_limit_bytes=PEER_VMEM_LIMIT),
        interpret=interpret,
        name="peer_up",
    )(rows_up, w, hs, x2, g.reshape(batch, 1, D_MODEL), tab_v.reshape(HALF_EXPERTS * SUBLANES, LANES),
      _slot_expand_matrix())


def _softcap(z):
    return GATE_SOFTCAP * jnp.tanh(z / GATE_SOFTCAP)


def _dot3(a_hi, a_lo, b_hi, b_lo, dims):
    dot = functools.partial(lax.dot_general, dimension_numbers=(dims, ((), ())), preferred_element_type=F32)
    return dot(a_hi, b_hi) + (dot(a_hi, b_lo) + dot(a_lo, b_hi))


def _top_rows(s, k):
    nrows = s.shape[0]
    row = lax.broadcasted_iota(jnp.int32, s.shape, 0)
    vals, rows = [], []
    for _ in range(k):
        m = jnp.max(s, axis=0, keepdims=True)
        r = jnp.min(jnp.where(s == m, row, nrows), axis=0, keepdims=True)
        vals.append(m)
        rows.append(r)
        s = jnp.where(row == r, -jnp.inf, s)
    return jnp.concatenate(vals, axis=0), jnp.concatenate(rows, axis=0)


def _tree(op, xs):
    while len(xs) > 1:
        xs = [op(xs[i], xs[i + 1]) if i + 1 < len(xs) else xs[i] for i in range(0, len(xs), 2)]
    return xs[0]


def _product_candidates(s0, s1, i0, i1):
    k = PEER_TOPK
    t = s0.shape[1]
    sub = lax.broadcasted_iota(jnp.int32, (SUBLANES, t), 0)
    blocks = []
    for half in range(k // SUBLANES):
        b = sub + half * SUBLANES
        lo = half * SUBLANES
        blocks.append((s0[0:1] + s1[lo:lo + SUBLANES], b, i0[0:1] * PEER_KEYS + i1[lo:lo + SUBLANES]))
    for a in range(1, SUBLANES):
        valid = (a + 1) * (sub + 1) <= k
        blocks.append((jnp.where(valid, s0[a:a + 1] + s1[0:SUBLANES], -jnp.inf), a * k + sub,
                       i0[a:a + 1] * PEER_KEYS + i1[0:SUBLANES]))
    a = sub + SUBLANES
    blocks.append((s0[SUBLANES:k] + s1[0:1], a * k, i0[SUBLANES:k] * PEER_KEYS + i1[0:1]))
    return blocks


def _route_prologue(x_ref, shift_ref, scale_ref, gain_ref, xs_ref, xh_ref, xl_ref):
    x = x_ref[...]
    xn = x * lax.rsqrt(jnp.mean(x * x, axis=-1, keepdims=True) + EPS) * gain_ref[...]
    xn = xn * (1.0 + scale_ref[0]) + shift_ref[0]
    x_hi, x_lo = _split_hi_lo(xn)
    for j in range(SUBLANES):
        c = (j % 2) * DOWN_SLAB_ROWS + j // 2
        xs_ref[:, j * LANES:(j + 1) * LANES] = x_hi[:, c * LANES:(c + 1) * LANES]
        xs_ref[:, D_MODEL + j * LANES:D_MODEL + (j + 1) * LANES] = x_lo[:, c * LANES:(c + 1) * LANES]
    xh_ref[...] = x_hi
    xl_ref[...] = x_lo


def _route_query(h, xh_ref, xl_ref, wqh_ref, wql_ref, q_ref):
    q_ref[...] = _dot3(xh_ref[...], xl_ref[...], wqh_ref[h], wql_ref[h], ((1,), (0,)))


def _route_sub_keys(p, skh_ref, skl_ref, q_ref, tv_ref, ti_ref):
    q_hi, q_lo = _split_hi_lo(q_ref[:, p * PEER_HALF:(p + 1) * PEER_HALF])
    s = _dot3(skh_ref[p], skl_ref[p], q_hi, q_lo, ((1,), (1,)))
    tv_ref[p], ti_ref[p] = _top_rows(s, PEER_TOPK)


def _route_head(h, tv_ref, ti_ref, bs_ref, be_ref):
    k = PEER_TOPK
    blocks = _product_candidates(tv_ref[0], tv_ref[1], ti_ref[0], ti_ref[1])
    sums = [b[0] for b in blocks]
    best_s, best_e = [], []
    for _ in range(k):
        m = jnp.max(_tree(jnp.maximum, sums), axis=0, keepdims=True)
        pos = jnp.min(_tree(jnp.minimum, [jnp.where(c == m, b[1], k * k) for c, b in zip(sums, blocks)]),
                      axis=0, keepdims=True)
        hit = [b[1] == pos for b in blocks]
        e = jnp.max(_tree(jnp.maximum, [jnp.where(hh, b[2], -1) for hh, b in zip(hit, blocks)]),
                    axis=0, keepdims=True)
        sums = [jnp.where(hh, -jnp.inf, c) for hh, c in zip(hit, sums)]
        best_s.append(m)
        best_e.append(e)
    bs = jnp.concatenate(best_s, axis=0)
    ex = jnp.exp(bs - bs[0:1])
    off = pl.multiple_of(h * k, k)
    bs_ref[pl.ds(off, k), :] = ex / jnp.sum(ex, axis=0, keepdims=True)
    be_ref[pl.ds(off, k), :] = jnp.concatenate(best_e, axis=0).astype(F32)


def _route_epilogue(bs_ref, be_ref, pos_ref, rows_down_ref, rows_up_ref, hs_ref, gate_ref):
    tt = gate_ref.shape[0]
    half = PEER_SLOTS // 2

    def gather_order(a):
        for c in range(tt // LANES):
            cols = slice(c * LANES, (c + 1) * LANES)
            tile_ref = pos_ref.at[c]
            tile_ref[pl.ds(0, half, stride=2), :] = a[:half, cols]
            tile_ref[pl.ds(1, half, stride=2), :] = a[half:, cols]
        return jnp.concatenate([pos_ref[c].T for c in range(tt // LANES)], axis=0)

    ids = be_ref[...]
    gate_ref[...] = gather_order(bs_ref[...])
    hs_ref[...] = (gather_order(ids).astype(jnp.int32) // HALF_EXPERTS).astype(F32)
    lo = ids.T[:, :half].astype(jnp.int32)
    hi = jnp.concatenate([ids[half:], ids[half:]], axis=0).T[:, :half].astype(jnp.int32)
    up_off = lambda e: (e % HALF_EXPERTS) * SUBLANES
    rows_up_ref[...] = up_off(lo) | (up_off(hi) << 16)
    rows_down_ref[...] = (lo * DOWN_SLAB_ROWS) | ((hi * DOWN_SLAB_ROWS) << 16)


N_ROUTE_IN, N_DOWN_IN, N_ROUTE_OUT, N_ROUTE_SCRATCH = 8, 5, 5, 8


def _route_down_kernel(*refs, do_route, do_down):
    refs = list(refs)
    take = lambda count: [refs.pop(0) for _ in range(count)]
    route_in = take(N_ROUTE_IN) if do_route else None
    down_in = take(N_DOWN_IN) if do_down else None
    route_out = take(N_ROUTE_OUT) if do_route else None
    w_ref = take(1)[0] if do_down else None
    route_scratch = take(N_ROUTE_SCRATCH) if do_route else None
    down_scratch = take(2) if do_down else None

    if do_route:
        x_ref, shift_ref, scale_ref, gain_ref, wqh_ref, wql_ref, skh_ref, skl_ref = route_in
        rows_down_ref, rows_up_ref, hs_ref, gate_ref, xs_ref = route_out
        xh_ref, xl_ref, q_ref, tv_ref, ti_ref, bs_ref, be_ref, pos_ref = route_scratch
        _route_prologue(x_ref, shift_ref, scale_ref, gain_ref, xs_ref, xh_ref, xl_ref)
    if do_down:
        idx_ref, xs_in_ref, gate_in_ref, tab_ref, sel_ref = down_in
        gb_ref, z_ref = down_scratch

    quarter = TOKEN_GROUP // 4

    def step(h, carry):
        rows = []
        if do_route:
            _route_query(h, xh_ref, xl_ref, wqh_ref, wql_ref, q_ref)
        for part in range(4):
            if do_down:
                rows += _down_tokens(h, part * quarter, (part + 1) * quarter, idx_ref, xs_in_ref, tab_ref, gb_ref)
            if do_route and part < 2:
                _route_sub_keys(part, skh_ref, skl_ref, q_ref, tv_ref, ti_ref)
            if do_route and part == 2:
                _route_head(h, tv_ref, ti_ref, bs_ref, be_ref)
        if do_down:
            _store_z(h, rows, z_ref)
        return carry

    lax.fori_loop(0, PEER_HEADS, step, 0)
    if do_route:
        _route_epilogue(bs_ref, be_ref, pos_ref, rows_down_ref, rows_up_ref, hs_ref, gate_ref)
    if do_down:
        _down_epilogue(z_ref, sel_ref, gate_in_ref, w_ref)


ROUTE_DOWN_VMEM_LIMIT = 56 * 1024 * 1024


def _route_down(route_args, down_args, seq_len, *, interpret=False):
    tt = PEER_TOKEN_TILE
    assert tt // TOKEN_GROUP == PEER_HEADS and seq_len % tt == 0
    tiles = seq_len // tt
    tok = lambda i: (i, 0)
    whole = pl.BlockSpec(memory_space=pltpu.VMEM)
    slot_spec = pl.BlockSpec((tt, PEER_SLOTS), tok)
    pair_spec = pl.BlockSpec((tt, PEER_SLOTS // 2), tok)
    slot_shape = jax.ShapeDtypeStruct((seq_len, PEER_SLOTS), F32)
    pair_shape = jax.ShapeDtypeStruct((seq_len, PEER_SLOTS // 2), jnp.int32)
    in_specs, args, out_specs, out_shape, scratch = [], [], [], [], []
    if route_args is not None:
        x2, b, shift, scale, gain, wq_hi, wq_lo, sk_hi, sk_lo = route_args
        batch = shift.shape[0]
        per_seq = pl.BlockSpec((1, 1, D_MODEL), lambda i: (b, 0, 0))
        in_specs += [pl.BlockSpec((tt, D_MODEL), lambda i: (b * tiles + i, 0)), per_seq, per_seq,
                     pl.BlockSpec((1, D_MODEL), lambda i: (0, 0)), whole, whole, whole, whole]
        args += [x2, shift.reshape(batch, 1, D_MODEL), scale.reshape(batch, 1, D_MODEL),
                 gain.reshape(1, D_MODEL), wq_hi, wq_lo, sk_hi, sk_lo]
    if down_args is not None:
        rows_down, xs, gate, tab_u = down_args
        in_specs += [pl.BlockSpec((tt, PEER_SLOTS // 2), tok, memory_space=pltpu.SMEM),
                     pl.BlockSpec((tt, SUBROWS, LANES), lambda i: (i, 0, 0)), slot_spec, whole,
                     pl.BlockSpec((DOWN_ROWS, PEER_SLOTS), lambda i: (0, 0))]
        args += [rows_down, xs.reshape(seq_len, SUBROWS, LANES), gate, tab_u, _down_select_matrix()]
    if route_args is not None:
        out_specs += [pair_spec, pair_spec, slot_spec, slot_spec, pl.BlockSpec((tt, 2 * D_MODEL), tok)]
        out_shape += [pair_shape, pair_shape, slot_shape, slot_shape,
                      jax.ShapeDtypeStruct((seq_len, 2 * D_MODEL), BF16)]
    if down_args is not None:
        out_specs.append(slot_spec)
        out_shape.append(slot_shape)
    if route_args is not None:
        scratch += [pltpu.VMEM((tt, D_MODEL), BF16),
                    pltpu.VMEM((tt, D_MODEL), BF16),
                    pltpu.VMEM((tt, PEER_QUERY_DIM), F32),
                    pltpu.VMEM((2, PEER_TOPK, tt), F32),
                    pltpu.VMEM((2, PEER_TOPK, tt), jnp.int32),
                    pltpu.VMEM((PEER_SLOTS, tt), F32),
                    pltpu.VMEM((PEER_SLOTS, tt), F32),
                    pltpu.VMEM((tt // LANES, PEER_SLOTS, LANES), F32)]
    if down_args is not None:
        scratch += [pltpu.VMEM((PEER_SLOTS // 2 * SUBLANES, LANES), jnp.int32),
                    pltpu.VMEM((tt, DOWN_ROWS), F32)]
    outs = pl.pallas_call(
        functools.partial(_route_down_kernel, do_route=route_args is not None, do_down=down_args is not None),
        grid=(tiles,), in_specs=in_specs, out_specs=out_specs, out_shape=out_shape, scratch_shapes=scratch,
        compiler_params=pltpu.CompilerParams(dimension_semantics=("arbitrary",),
                                             vmem_limit_bytes=ROUTE_DOWN_VMEM_LIMIT),
        interpret=interpret, name="peer_route_down",
    )(*args)
    routed = tuple(outs[:N_ROUTE_OUT]) if route_args is not None else None
    w = outs[-1] if down_args is not None else None
    return routed, w


def _peer(x, gain, shift, scale, res_gate, wq, subkeys, tab_u, tab_v):
    B, S, D = x.shape
    x2 = x.reshape(B * S, D)
    per_head = lambda w: w.reshape(D, PEER_HEADS, PEER_QUERY_DIM).transpose(1, 0, 2)
    wq_hi, wq_lo = _split_hi_lo(per_head(wq))
    sk_hi, sk_lo = _split_hi_lo(subkeys)
    routed, rows_up, hs, ws = None, [], [], []
    for b in range(B + 1):
        route_args = (x2, b, shift, scale, gain, wq_hi, wq_lo, sk_hi, sk_lo) if b < B else None
        down_args = (routed[0], routed[4], routed[3], tab_u) if routed is not None else None
        routed, w = _route_down(route_args, down_args, S)
        if w is not None:
            ws.append(w)
        if routed is not None:
            rows_up.append(routed[1])
            hs.append(routed[2])
    y = _peer_up(jnp.concatenate(rows_up), jnp.concatenate(ws), jnp.concatenate(hs), x2, res_gate, tab_v, S)
    return y.reshape(B, S, D)


PROJ_TOKEN_TILE = 512
PROJ_VMEM_LIMIT = 48 * 1024 * 1024


def _ada_norm_tile(x, gain, shift, scale):
    xn = x * lax.rsqrt(jnp.mean(x * x, axis=-1, keepdims=True) + EPS) * gain
    return xn * (1.0 + scale) + shift


def _norm_proj_kernel(x_ref, shift_ref, scale_ref, gain_ref, w_ref, *rest, with_gate):
    x_hi, x_lo = _split_hi_lo(_ada_norm_tile(x_ref[...], gain_ref[...], shift_ref[0], scale_ref[0]))
    if with_gate:
        wgh_ref, wgl_ref, main_ref, gate_ref = rest
        gate_ref[...] = _dot3(x_hi, x_lo, wgh_ref[...], wgl_ref[...], ((1,), (0,)))
    else:
        (main_ref,) = rest
    main_ref[...] = jnp.dot(x_hi, w_ref[...], preferred_element_type=F32)


def _norm_proj(x2, shift, scale, gain, w_main, w_gate, seq_len, *, token_tile=PROJ_TOKEN_TILE, interpret=False):
    n = x2.shape[0]
    tt = token_tile
    assert seq_len % tt == 0 and n % seq_len == 0
    tiles_per_seq = seq_len // tt
    batch = n // seq_len
    m = w_main.shape[1]
    tok = lambda i: (i, 0)
    per_seq = pl.BlockSpec((1, 1, D_MODEL), lambda i: (i // tiles_per_seq, 0, 0))
    whole = pl.BlockSpec(memory_space=pltpu.VMEM)
    in_specs = [pl.BlockSpec((tt, D_MODEL), tok), per_seq, per_seq,
                pl.BlockSpec((1, D_MODEL), lambda i: (0, 0)), whole]
    args = [x2, shift.reshape(batch, 1, D_MODEL), scale.reshape(batch, 1, D_MODEL), gain.reshape(1, D_MODEL),
            w_main.astype(BF16)]
    out_specs = [pl.BlockSpec((tt, m), tok)]
    out_shape = [jax.ShapeDtypeStruct((n, m), F32)]
    if w_gate is not None:
        wg_hi, wg_lo = _split_hi_lo(jnp.pad(w_gate, ((0, 0), (0, LANES - w_gate.shape[1]))))
        in_specs += [whole, whole]
        args += [wg_hi, wg_lo]
        out_specs.append(pl.BlockSpec((tt, LANES), tok))
        out_shape.append(jax.ShapeDtypeStruct((n, LANES), F32))
    outs = pl.pallas_call(
        functools.partial(_norm_proj_kernel, with_gate=w_gate is not None),
        grid=(n // tt,), in_specs=in_specs, out_specs=out_specs, out_shape=out_shape,
        compiler_params=pltpu.CompilerParams(dimension_semantics=("arbitrary",),
                                             vmem_limit_bytes=PROJ_VMEM_LIMIT),
        interpret=interpret, name="norm_proj",
    )(*args)
    return (outs[0], outs[1]) if w_gate is not None else (outs[0], None)


def _out_proj_kernel(h_ref, og_ref, x_ref, g_ref, hg_ref, w_ref, o_ref, *, head_dim):
    h = h_ref[...]
    if head_dim is not None:
        parts = []
        for j in range(D_MODEL // head_dim):
            hb = h[:, j * head_dim:(j + 1) * head_dim]
            parts.append(hb * lax.rsqrt(jnp.mean(hb * hb, axis=-1, keepdims=True) + EPS))
        h = jnp.concatenate(parts, axis=1) * hg_ref[...]
    a = jax.nn.sigmoid(og_ref[...]) * h
    y = jnp.dot(a.astype(BF16), w_ref[...], preferred_element_type=F32)
    o_ref[...] = x_ref[...] + g_ref[0] * y


def _out_proj(h2, p, og_block, x2, g, h_gain, w_out, seq_len, head_dim, *, token_tile=PROJ_TOKEN_TILE,
              interpret=False):
    n = x2.shape[0]
    tt = token_tile
    tiles_per_seq = seq_len // tt
    batch = n // seq_len
    tok = lambda i: (i, 0)
    row = pl.BlockSpec((tt, D_MODEL), tok)
    gain = jnp.ones((1, D_MODEL), F32) if h_gain is None else h_gain.reshape(1, D_MODEL)
    return pl.pallas_call(
        functools.partial(_out_proj_kernel, head_dim=head_dim),
        grid=(n // tt,),
        in_specs=[row, pl.BlockSpec((tt, D_MODEL), lambda i: (i, og_block)), row,
                  pl.BlockSpec((1, 1, D_MODEL), lambda i: (i // tiles_per_seq, 0, 0)),
                  pl.BlockSpec((1, D_MODEL), lambda i: (0, 0)),
                  pl.BlockSpec(memory_space=pltpu.VMEM)],
        out_specs=row,
        out_shape=jax.ShapeDtypeStruct((n, D_MODEL), F32),
        compiler_params=pltpu.CompilerParams(dimension_semantics=("arbitrary",),
                                             vmem_limit_bytes=PROJ_VMEM_LIMIT),
        interpret=interpret, name="out_proj",
    )(h2, p, x2, g.reshape(batch, 1, D_MODEL), gain, w_out.astype(BF16))


MLSTM_STATE_W = MLSTM_V_DIM + LANES


def _mlstm_kernel(q_ref, k_ref, v_ref, gc_ref, gr_ref, h_ref, cn_ref, m_ref):
    H, DQK, DV, L = MLSTM_HEADS, MLSTM_QK_DIM, MLSTM_V_DIM, MLSTM_CHUNK

    @pl.when(pl.program_id(1) == 0)
    def _():
        cn_ref[...] = jnp.zeros_like(cn_ref)
        m_ref[...] = jnp.zeros_like(m_ref)

    tril = lax.broadcasted_iota(jnp.int32, (L, L), 1) <= lax.broadcasted_iota(jnp.int32, (L, L), 0)
    gc = gc_ref[...]
    gr = gr_ref[0]
    ones = jnp.ones((L, LANES), F32)
    for h in range(H):
        ig_col, b_col = gc[:, h:h + 1], gc[:, H + h:H + h + 1]
        ig_row, b_row = gr[h:h + 1, :], gr[H + h:H + h + 1, :]
        m_prev = m_ref[h:h + 1, 0:1]
        d_log = jnp.where(tril, b_col - b_row + ig_row, -jnp.inf)
        inter = b_col + m_prev
        m_t = jnp.maximum(inter, jnp.max(d_log, axis=1, keepdims=True))
        w = jnp.exp(d_log - m_t)
        a_inter = jnp.exp(inter - m_t)
        qh = (q_ref[:, h * DQK:(h + 1) * DQK] * (DQK ** -0.5)).astype(BF16)
        kf = k_ref[:, h * DQK:(h + 1) * DQK]
        vh = v_ref[:, h * DV:(h + 1) * DV]
        s = lax.dot_general(qh, kf.astype(BF16), (((1,), (1,)), ((), ())), preferred_element_type=F32) * w
        cn = cn_ref[h]
        qc = jnp.dot(qh, cn.astype(BF16), preferred_element_type=F32)
        num = a_inter * qc[:, :DV] + jnp.dot(s.astype(BF16), vh.astype(BF16), preferred_element_type=F32)
        den = a_inter * qc[:, DV:DV + 1] + jnp.sum(s, axis=1, keepdims=True)
        h_ref[:, h * DV:(h + 1) * DV] = num / jnp.maximum(jnp.abs(den), jnp.exp(-m_t))
        b_last = b_col[L - 1:L, :]
        g_col = b_last - b_col + ig_col
        m_new = jnp.maximum(b_last + m_prev, jnp.max(g_col, axis=0, keepdims=True))
        w_s = jnp.exp(g_col - m_new)
        decay = jnp.exp(b_last + m_prev - m_new)
        kw = (kf * w_s).astype(BF16)
        vaug = jnp.concatenate([vh, ones], axis=1).astype(BF16)
        cn_ref[h] = decay * cn + lax.dot_general(kw, vaug, (((0,), (0,)), ((), ())), preferred_element_type=F32)
        m_ref[h:h + 1, :] = jnp.broadcast_to(m_new, (1, LANES))


def _mlstm_scan(p, gates, b_i, b_f, batch, seq_len, *, interpret=False):
    H, L = MLSTM_HEADS, MLSTM_CHUNK
    n = p.shape[0]
    nc = seq_len // L
    ig = _softcap(gates[:, :H] + b_i)
    lf = jax.nn.log_sigmoid(_softcap(gates[:, H:2 * H] + b_f))
    b = jnp.cumsum(lf.reshape(n // L, L, H), axis=1).reshape(n, H)
    gc = jnp.concatenate([ig, b], axis=1)
    gr = gc.reshape(n // L, L, 2 * H).transpose(0, 2, 1)
    chunk = lambda bi, c: bi * nc + c
    return pl.pallas_call(
        _mlstm_kernel,
        grid=(batch, nc),
        in_specs=[pl.BlockSpec((L, MLSTM_QK_W), lambda bi, c: (chunk(bi, c), 0)),
                  pl.BlockSpec((L, MLSTM_QK_W), lambda bi, c: (chunk(bi, c), 1)),
                  pl.BlockSpec((L, MLSTM_V_W), lambda bi, c: (chunk(bi, c), 1)),
                  pl.BlockSpec((L, 2 * H), lambda bi, c: (chunk(bi, c), 0)),
                  pl.BlockSpec((1, 2 * H, L), lambda bi, c: (chunk(bi, c), 0, 0))],
        out_specs=pl.BlockSpec((L, D_MODEL), lambda bi, c: (chunk(bi, c), 0)),
        out_shape=jax.ShapeDtypeStruct((n, D_MODEL), F32),
        scratch_shapes=[pltpu.VMEM((H, MLSTM_QK_DIM, MLSTM_STATE_W), F32),
                        pltpu.VMEM((SUBLANES, LANES), F32)],
        compiler_params=pltpu.CompilerParams(dimension_semantics=("arbitrary", "arbitrary")),
        interpret=interpret, name="mlstm_scan",
    )(p, p, p, gc, gr)


FOX_BLOCK = 512
FOX_QUERY_PART = 256
FOX_SUM_ROWS = 16
FOX_VMEM_LIMIT = 48 * 1024 * 1024


def _bias_columns(f, query_side):
    hi = f.astype(BF16).astype(F32)
    r1 = f - hi
    lo = r1.astype(BF16).astype(F32)
    lo2 = r1 - lo
    lane = lax.broadcasted_iota(jnp.int32, (f.shape[0], LANES), 1)
    if query_side:
        vals = jnp.where(lane == 0, hi, jnp.where(lane == 1, lo, jnp.where(lane == 2, lo2,
                         jnp.where(lane < 6, 1.0, 0.0))))
    else:
        vals = jnp.where(lane < 3, 1.0, jnp.where(lane == 3, -hi, jnp.where(lane == 4, -lo,
                         jnp.where(lane == 5, -lo2, 0.0))))
    return vals.astype(BF16)


def _head_column(fc, h):
    lane = lax.broadcasted_iota(jnp.int32, fc.shape, 1)
    return jnp.sum(jnp.where(lane == h, fc, 0.0), axis=1, keepdims=True)


def _rms_rows(t, gain):
    return t * lax.rsqrt(jnp.mean(t * t, axis=-1, keepdims=True) + EPS) * gain


def _fox_attn_kernel(q_ref, k_ref, v_ref, fq_ref, fk_ref, qg_ref, kg_ref, o_ref,
                     ka_ref, vt_ref, qa_ref, st_cur_ref, st_next_ref, *state_refs):
    i = pl.program_id(1)
    h = pl.program_id(0) % FOX_HEADS
    tq = q_ref.shape[0]
    nk = k_ref.shape[0] // tq
    hd = FOX_HEAD_DIM
    part = FOX_QUERY_PART
    n_parts = tq // part
    m_refs, acc_refs = state_refs[:n_parts], state_refs[n_parts:]

    @pl.when(i == 0)
    def _():
        def prep(c, carry):
            r = pl.ds(pl.multiple_of(c * tq, tq), tq)
            ka_ref[r, :hd] = _rms_rows(k_ref[r, :], kg_ref[...]).astype(BF16)
            ka_ref[r, hd:] = _bias_columns(_head_column(fk_ref[r, :], h), False)
            vt_ref[c, :hd, :] = v_ref[r, :].T.astype(BF16)
            vt_ref[c, hd:, :] = jnp.ones((FOX_SUM_ROWS, tq), BF16)
            return carry
        lax.fori_loop(0, nk, prep, 0)

    qn = _rms_rows(q_ref[...], qg_ref[...]) * (hd ** -0.5)
    qa_ref[:, :hd] = qn.astype(BF16)
    qa_ref[:, hd:] = _bias_columns(_head_column(fq_ref[...], h), True)
    for m_ref, acc_ref in zip(m_refs, acc_refs):
        m_ref[...] = jnp.full_like(m_ref, -jnp.inf)
        acc_ref[...] = jnp.zeros_like(acc_ref)

    def scores(j, st_ref):
        kblk = ka_ref[pl.ds(pl.multiple_of(j * tq, tq), tq), :]
        for c in range(n_parts):
            st_ref[c] = lax.dot_general(kblk, qa_ref[c * part:(c + 1) * part, :],
                                        (((1,), (1,)), ((), ())), preferred_element_type=F32)

    def softmax_pv(j, masked):
        vtblk = vt_ref[j]
        pts, alphas = [], []
        for c, m_ref in enumerate(m_refs):
            st = st_cur_ref[c]
            if masked:
                key = lax.broadcasted_iota(jnp.int32, st.shape, 0)
                qry = lax.broadcasted_iota(jnp.int32, st.shape, 1) + c * part
                st = jnp.where(key <= qry, st, -jnp.inf)
            m_prev = m_ref[0:1, :]
            m_new = jnp.maximum(m_prev, jnp.max(st, axis=0, keepdims=True))
            alphas.append(jnp.exp(m_prev - m_new))
            pts.append(jnp.exp(st - m_new).astype(BF16))
            m_ref[0:1, :] = m_new
        for pt, alpha, acc_ref in zip(pts, alphas, acc_refs):
            acc_ref[...] = alpha * acc_ref[...] + jnp.dot(vtblk, pt, preferred_element_type=F32)

    def body(j, carry):
        scores(j + 1, st_next_ref)
        softmax_pv(j, False)
        st_cur_ref[...] = st_next_ref[...]
        return carry

    scores(0, st_cur_ref)
    lax.fori_loop(0, i, body, 0)
    softmax_pv(i, True)
    for c, acc_ref in enumerate(acc_refs):
        acc = acc_ref[...]
        o_ref[c * part:(c + 1) * part, :] = (acc[:hd] / acc[hd:hd + 1]).T


def _fox_attention(pq, kv, f_cum, q_gain, k_gain, batch, seq_len, *, block=FOX_BLOCK, interpret=False):
    n = pq.shape[0]
    hd, nh = FOX_HEAD_DIM, FOX_HEADS
    nq = seq_len // block
    return pl.pallas_call(
        _fox_attn_kernel,
        grid=(batch * nh, nq),
        in_specs=[pl.BlockSpec((block, hd), lambda bh, i: ((bh // nh) * nq + i, bh % nh)),
                  pl.BlockSpec((seq_len, hd), lambda bh, i: (bh // nh, bh % nh)),
                  pl.BlockSpec((seq_len, hd), lambda bh, i: (bh // nh, nh + bh % nh)),
                  pl.BlockSpec((block, nh), lambda bh, i: ((bh // nh) * nq + i, 0)),
                  pl.BlockSpec((seq_len, nh), lambda bh, i: (bh // nh, 0)),
                  pl.BlockSpec((1, hd), lambda bh, i: (0, 0)),
                  pl.BlockSpec((1, hd), lambda bh, i: (0, 0))],
        out_specs=pl.BlockSpec((block, hd), lambda bh, i: ((bh // nh) * nq + i, bh % nh)),
        out_shape=jax.ShapeDtypeStruct((n, D_MODEL), F32),
        scratch_shapes=[pltpu.VMEM((seq_len, 2 * hd), BF16),
                        pltpu.VMEM((nq, hd + FOX_SUM_ROWS, block), BF16),
                        pltpu.VMEM((block, 2 * hd), BF16),
                        *[pltpu.VMEM((block // FOX_QUERY_PART, block, FOX_QUERY_PART), F32)] * 2,
                        *[pltpu.VMEM((SUBLANES, FOX_QUERY_PART), F32)] * (block // FOX_QUERY_PART),
                        *[pltpu.VMEM((hd + FOX_SUM_ROWS, FOX_QUERY_PART), F32)] * (block // FOX_QUERY_PART)],
        compiler_params=pltpu.CompilerParams(dimension_semantics=("arbitrary", "arbitrary"),
                                             vmem_limit_bytes=FOX_VMEM_LIMIT),
        interpret=interpret, name="fox_attention",
    )(pq, kv, kv, f_cum, f_cum, q_gain.reshape(1, hd), k_gain.reshape(1, hd))


def kernel(x, c, ada_w, ada_b, mix_norm, ffn_norm, a_w_in, a_b_i, a_b_f, a_h_norm, a_w_out,
           kv_ada_w, kv_ada_b, kv_norm, kv_w, kv_b_f, kv_k_norm, b_w_qo, b_q_norm, b_w_out,
           peer_wq, peer_subkeys, peer_u, peer_v):
    B, S, D = x.shape
    n = B * S
    cs = jax.nn.silu(c)
    x2 = x.reshape(n, D)
    kv = f_cum = None
    for l in range(DEPTH):
        mod = cs @ ada_w[l] + ada_b[l]
        sh1, sc1, g1, sh2, sc2, g2 = jnp.split(mod, N_ADA, axis=-1)
        if l < N_A_LAYERS:
            split = 2 * MLSTM_QK_W + MLSTM_V_W + D_MODEL
            p, gates = _norm_proj(x2, sh1, sc1, mix_norm[l], a_w_in[l][:, :split], a_w_in[l][:, split:], S)
            h = _mlstm_scan(p, gates, a_b_i[l], a_b_f[l], B, S)
            x2 = _out_proj(h, p, 2, x2, g1, a_h_norm[l], a_w_out[l], S, MLSTM_V_DIM)
        else:
            j = l - N_A_LAYERS
            pq, _ = _norm_proj(x2, sh1, sc1, mix_norm[l], b_w_qo[j], None, S)
            att = _fox_attention(pq, kv, f_cum, b_q_norm[j], kv_k_norm, B, S)
            x2 = _out_proj(att, pq, 1, x2, g1, None, b_w_out[j], S, None)
        x2 = _peer(x2.reshape(B, S, D), ffn_norm[l], sh2, sc2, g2, peer_wq[l], peer_subkeys[l],
                   _pack_down_table(peer_u[l]), _pack_expert_table(peer_v[l])).reshape(n, D)
        if l == N_A_LAYERS - 1:
            sh, sc = jnp.split(cs @ kv_ada_w + kv_ada_b, 2, axis=-1)
            kv, fg = _norm_proj(x2, sh, sc, kv_norm, kv_w[:, :2 * D], kv_w[:, 2 * D:], S)
            log_f = jax.nn.log_sigmoid(fg[:, :FOX_HEADS] + kv_b_f)
            f_cum = jnp.cumsum(log_f.reshape(B, S, FOX_HEADS), axis=1).reshape(n, FOX_HEADS)
    return x2.reshape(B, S, D)
```

```python
import functools
import math

import jax
import jax.numpy as jnp
from jax import lax
from jax.experimental import pallas as pl
from jax.experimental.pallas import tpu as pltpu

F32 = jnp.float32
BF16 = jnp.bfloat16

D_MODEL = 1024
DEPTH = 2
N_A_LAYERS = DEPTH // 2
EPS = 1e-6
N_ADA = 6

MLSTM_HEADS = 4
MLSTM_QK_DIM = D_MODEL // (2 * MLSTM_HEADS)
MLSTM_V_DIM = D_MODEL // MLSTM_HEADS
MLSTM_CHUNK = 64
GATE_SOFTCAP = 15.0
MLSTM_QK_W = MLSTM_HEADS * MLSTM_QK_DIM
MLSTM_V_W = MLSTM_HEADS * MLSTM_V_DIM

FOX_HEADS = 8
FOX_HEAD_DIM = D_MODEL // FOX_HEADS

PEER_HEADS = 8
PEER_KEYS = 128
PEER_EXPERTS = PEER_KEYS * PEER_KEYS
PEER_QUERY_DIM = 256
PEER_HALF = PEER_QUERY_DIM // 2
PEER_TOPK = 16

SUBLANES = 8
LANES = 128
ROW_WORDS = SUBLANES * LANES
assert ROW_WORDS == D_MODEL
PEER_SLOTS = PEER_HEADS * PEER_TOPK
HALF_EXPERTS = PEER_EXPERTS // 2
SUBROWS = 2 * SUBLANES
GATHER_ROWS = PEER_SLOTS * SUBROWS
DOWN_SLAB_ROWS = SUBLANES // 2
DOWN_ROWS = PEER_SLOTS * SUBLANES
TOKEN_GROUP = 2 * SUBLANES
PEER_TOKEN_TILE = 128
PEER_VMEM_LIMIT = 48 * 1024 * 1024


def _pack_expert_table(t):
    lo = lax.bitcast_convert_type(t[:HALF_EXPERTS].astype(BF16), jnp.uint16).astype(jnp.uint32)
    hi = lax.bitcast_convert_type(t[HALF_EXPERTS:].astype(BF16), jnp.uint16).astype(jnp.uint32)
    w = lo | (hi << 16)
    return lax.bitcast_convert_type(w, jnp.int32).reshape(HALF_EXPERTS, SUBLANES, LANES)


def _pack_down_table(t):
    bits = lax.bitcast_convert_type(t.astype(BF16), jnp.uint16).astype(jnp.uint32)
    bits = bits.reshape(t.shape[0], 2, DOWN_SLAB_ROWS, LANES)
    w = bits[:, 0] | (bits[:, 1] << 16)
    return lax.bitcast_convert_type(w, jnp.int32).reshape(t.shape[0] * DOWN_SLAB_ROWS, LANES)


def _slot_expand_matrix():
    slot = lax.broadcasted_iota(jnp.int32, (PEER_SLOTS, GATHER_ROWS), 0)
    sub = lax.broadcasted_iota(jnp.int32, (PEER_SLOTS, GATHER_ROWS), 1)
    return (sub // SUBROWS == slot).astype(BF16)


def _diag_mask():
    r = lax.broadcasted_iota(jnp.int32, (SUBLANES, GATHER_ROWS), 0)
    sub = lax.broadcasted_iota(jnp.int32, (SUBLANES, GATHER_ROWS), 1)
    return (sub % SUBROWS) // 2 == r


def _down_diag_mask():
    j = lax.broadcasted_iota(jnp.int32, (SUBLANES, DOWN_ROWS), 0)
    sub = lax.broadcasted_iota(jnp.int32, (SUBLANES, DOWN_ROWS), 1)
    return sub % SUBLANES == j


def _down_select_matrix():
    sub = lax.broadcasted_iota(jnp.int32, (DOWN_ROWS, PEER_SLOTS), 0)
    pos = lax.broadcasted_iota(jnp.int32, (DOWN_ROWS, PEER_SLOTS), 1)
    return (sub // SUBLANES == pos).astype(F32)


def _half_mask(hs, e):
    hsx = jnp.dot(hs.astype(BF16), e, preferred_element_type=F32)
    par = (lax.broadcasted_iota(jnp.int32, hsx.shape, 1) % 2).astype(F32)
    return hsx == par


def _gather_token(idx_ref, tab_ref, gb_ref, t):
    for s in range(PEER_SLOTS // 2):
        w = idx_ref[t, s]
        a = pl.multiple_of(w & 0xFFFF, SUBLANES)
        b = pl.multiple_of(lax.shift_right_logical(w, 16), SUBLANES)
        gb_ref[pl.ds(2 * s * SUBLANES, SUBLANES), :] = tab_ref[pl.ds(a, SUBLANES), :]
        gb_ref[pl.ds((2 * s + 1) * SUBLANES, SUBLANES), :] = tab_ref[pl.ds(b, SUBLANES), :]
    return pltpu.bitcast(gb_ref[...], BF16)


def _gather_token_down(idx_ref, tab_ref, gb_ref, t):
    for s in range(PEER_SLOTS // 2):
        w = idx_ref[t, s]
        a = pl.multiple_of(w & 0xFFFF, DOWN_SLAB_ROWS)
        b = pl.multiple_of(lax.shift_right_logical(w, 16), DOWN_SLAB_ROWS)
        gb_ref[pl.ds(s * SUBLANES, SUBLANES), :] = jnp.concatenate(
            [tab_ref[pl.ds(a, DOWN_SLAB_ROWS), :], tab_ref[pl.ds(b, DOWN_SLAB_ROWS), :]], axis=0)
    return pltpu.bitcast(gb_ref[...], BF16)


def _split_hi_lo(a):
    hi = a.astype(BF16)
    lo = (a - hi.astype(F32)).astype(BF16)
    return hi, lo


def _down_tokens(g, first, last, idx_ref, xs_ref, tab_ref, gb_ref):
    diag = _down_diag_mask()
    rows = []
    for i in range(first, last):
        t = g * TOKEN_GROUP + i
        gath = _gather_token_down(idx_ref, tab_ref, gb_ref, t)
        y = lax.dot_general(xs_ref[t], gath, (((1,), (1,)), ((), ())),
                            preferred_element_type=F32)
        y8 = y[:SUBLANES] + y[SUBLANES:]
        rows.append(jnp.sum(jnp.where(diag, y8, 0.0), axis=0, keepdims=True))
    return rows


def _store_z(g, rows, z_ref):
    z_ref[pl.ds(pl.multiple_of(g * TOKEN_GROUP, TOKEN_GROUP), TOKEN_GROUP), :] = jnp.concatenate(rows, axis=0)


def _down_epilogue(z_ref, sel_ref, gate_ref, w_ref):
    act = jnp.dot(z_ref[...], sel_ref[...], precision=lax.Precision.HIGHEST, preferred_element_type=F32)
    gelu = 0.5 * act * (1.0 + lax.erf(act * (1.0 / math.sqrt(2.0))))
    w_ref[...] = gate_ref[...] * gelu


def _peer_up_kernel(idx_ref, w_ref, hs_ref, x_ref, g_ref, tab_ref, e_ref, y_ref, gb_ref, ahi_ref, alo_ref):
    tt = w_ref.shape[0]
    diag = _diag_mask()
    e = e_ref[...]
    hm = _half_mask(hs_ref[...], e)
    w_hi, w_lo = _split_hi_lo(w_ref[...])
    ahi_ref[...] = jnp.where(hm, jnp.dot(w_hi, e, preferred_element_type=F32), 0.0)
    alo_ref[...] = jnp.where(hm, jnp.dot(w_lo, e, preferred_element_type=F32), 0.0)

    def group(g, carry):
        base = pl.multiple_of(g * TOKEN_GROUP, TOKEN_GROUP)
        a_hi = ahi_ref[pl.ds(base, TOKEN_GROUP), :]
        a_lo = alo_ref[pl.ds(base, TOKEN_GROUP), :]
        outs = []
        for i in range(TOKEN_GROUP):
            t = g * TOKEN_GROUP + i
            gath = _gather_token(idx_ref, tab_ref, gb_ref, t)
            lhs = jnp.concatenate(
                [jnp.where(diag, a_hi[i:i + 1, :], 0.0), jnp.where(diag, a_lo[i:i + 1, :], 0.0)],
                axis=0).astype(BF16)
            out = jnp.dot(lhs, gath, preferred_element_type=F32)
            outs.append(out[:SUBLANES] + out[SUBLANES:])
        rows = pl.ds(base, TOKEN_GROUP)
        for r in range(SUBLANES):
            cols = slice(r * LANES, (r + 1) * LANES)
            chunk = jnp.concatenate([o[r:r + 1, :] for o in outs], axis=0)
            y_ref[rows, cols] = x_ref[rows, cols] + g_ref[0][:, cols] * chunk
        return carry

    lax.fori_loop(0, tt // TOKEN_GROUP, group, 0)


def _peer_up(rows_up, w, hs, x2, g, tab_v, seq_len, *, token_tile=PEER_TOKEN_TILE, interpret=False):
    n = w.shape[0]
    tt = token_tile
    assert n % tt == 0 and tt % TOKEN_GROUP == 0 and seq_len % tt == 0
    tiles_per_seq = seq_len // tt
    batch = n // seq_len
    tok = lambda i: (i, 0)
    slot_spec = pl.BlockSpec((tt, PEER_SLOTS), tok)
    row_spec = pl.BlockSpec((tt, D_MODEL), tok)
    return pl.pallas_call(
        _peer_up_kernel,
        grid=(n // tt,),
        in_specs=[pl.BlockSpec((tt, PEER_SLOTS // 2), tok, memory_space=pltpu.SMEM), slot_spec, slot_spec,
                  row_spec, pl.BlockSpec((1, 1, D_MODEL), lambda i: (i // tiles_per_seq, 0, 0)),
                  pl.BlockSpec(memory_space=pltpu.VMEM),
                  pl.BlockSpec((PEER_SLOTS, GATHER_ROWS), lambda i: (0, 0))],
        out_specs=row_spec,
        out_shape=jax.ShapeDtypeStruct((n, D_MODEL), F32),
        scratch_shapes=[pltpu.VMEM((PEER_SLOTS * SUBLANES, LANES), jnp.int32),
                        pltpu.VMEM((tt, GATHER_ROWS), F32),
                        pltpu.VMEM((tt, GATHER_ROWS), F32)],
        compiler_params=pltpu.CompilerParams(dimension_semantics=("arbitrary",),
                                             vmem_limit_bytes=PEER_VMEM_LIMIT),
        interpret=interpret,
        name="peer_up",
    )(rows_up, w, hs, x2, g.reshape(batch, 1, D_MODEL), tab_v.reshape(HALF_EXPERTS * SUBLANES, LANES),
      _slot_expand_matrix())


def _softcap(z):
    return GATE_SOFTCAP * jnp.tanh(z / GATE_SOFTCAP)


def _dot3(a_hi, a_lo, b_hi, b_lo, dims):
    dot = functools.partial(lax.dot_general, dimension_numbers=(dims, ((), ())), preferred_element_type=F32)
    return dot(a_hi, b_hi) + (dot(a_hi, b_lo) + dot(a_lo, b_hi))


def _top_rows(s, k):
    nrows = s.shape[0]
    row = lax.broadcasted_iota(jnp.int32, s.shape, 0)
    vals, rows = [], []
    for _ in range(k):
        m = jnp.max(s, axis=0, keepdims=True)
        r = jnp.min(jnp.where(s == m, row, nrows), axis=0, keepdims=True)
        vals.append(m)
        rows.append(r)
        s = jnp.where(row == r, -jnp.inf, s)
    return jnp.concatenate(vals, axis=0), jnp.concatenate(rows, axis=0)


def _tree(op, xs):
    while len(xs) > 1:
        xs = [op(xs[i], xs[i + 1]) if i + 1 < len(xs) else xs[i] for i in range(0, len(xs), 2)]
    return xs[0]


def _product_candidates(s0, s1, i0, i1):
    k = PEER_TOPK
    t = s0.shape[1]
    sub = lax.broadcasted_iota(jnp.int32, (SUBLANES, t), 0)
    blocks = []
    for half in range(k // SUBLANES):
        b = sub + half * SUBLANES
        lo = half * SUBLANES
        blocks.append((s0[0:1] + s1[lo:lo + SUBLANES], b, i0[0:1] * PEER_KEYS + i1[lo:lo + SUBLANES]))
    for a in range(1, SUBLANES):
        valid = (a + 1) * (sub + 1) <= k
        blocks.append((jnp.where(valid, s0[a:a + 1] + s1[0:SUBLANES], -jnp.inf), a * k + sub,
                       i0[a:a + 1] * PEER_KEYS + i1[0:SUBLANES]))
    a = sub + SUBLANES
    blocks.append((s0[SUBLANES:k] + s1[0:1], a * k, i0[SUBLANES:k] * PEER_KEYS + i1[0:1]))
    return blocks


def _route_prologue(x_ref, shift_ref, scale_ref, gain_ref, xs_ref, xh_ref, xl_ref):
    x = x_ref[...]
    xn = x * lax.rsqrt(jnp.mean(x * x, axis=-1, keepdims=True) + EPS) * gain_ref[...]
    xn = xn * (1.0 + scale_ref[0]) + shift_ref[0]
    x_hi, x_lo = _split_hi_lo(xn)
    for j in range(SUBLANES):
        c = (j % 2) * DOWN_SLAB_ROWS + j // 2
        xs_ref[:, j * LANES:(j + 1) * LANES] = x_hi[:, c * LANES:(c + 1) * LANES]
        xs_ref[:, D_MODEL + j * LANES:D_MODEL + (j + 1) * LANES] = x_lo[:, c * LANES:(c + 1) * LANES]
    xh_ref[...] = x_hi
    xl_ref[...] = x_lo


def _route_query(h, xh_ref, xl_ref, wqh_ref, wql_ref, q_ref):
    q_ref[...] = _dot3(xh_ref[...], xl_ref[...], wqh_ref[h], wql_ref[h], ((1,), (0,)))


def _route_sub_keys(p, skh_ref, skl_ref, q_ref, tv_ref, ti_ref):
    q_hi, q_lo = _split_hi_lo(q_ref[:, p * PEER_HALF:(p + 1) * PEER_HALF])
    s = _dot3(skh_ref[p], skl_ref[p], q_hi, q_lo, ((1,), (1,)))
    tv_ref[p], ti_ref[p] = _top_rows(s, PEER_TOPK)


def _route_head(h, tv_ref, ti_ref, bs_ref, be_ref):
    k = PEER_TOPK
    blocks = _product_candidates(tv_ref[0], tv_ref[1], ti_ref[0], ti_ref[1])
    sums = [b[0] for b in blocks]
    best_s, best_e = [], []
    for _ in range(k):
        m = jnp.max(_tree(jnp.maximum, sums), axis=0, keepdims=True)
        pos = jnp.min(_tree(jnp.minimum, [jnp.where(c == m, b[1], k * k) for c, b in zip(sums, blocks)]),
                      axis=0, keepdims=True)
        hit = [b[1] == pos for b in blocks]
        e = jnp.max(_tree(jnp.maximum, [jnp.where(hh, b[2], -1) for hh, b in zip(hit, blocks)]),
                    axis=0, keepdims=True)
        sums = [jnp.where(hh, -jnp.inf, c) for hh, c in zip(hit, sums)]
        best_s.append(m)
        best_e.append(e)
    bs = jnp.concatenate(best_s, axis=0)
    ex = jnp.exp(bs - bs[0:1])
    off = pl.multiple_of(h * k, k)
    bs_ref[pl.ds(off, k), :] = ex / jnp.sum(ex, axis=0, keepdims=True)
    be_ref[pl.ds(off, k), :] = jnp.concatenate(best_e, axis=0).astype(F32)


def _route_epilogue(bs_ref, be_ref, pos_ref, rows_down_ref, rows_up_ref, hs_ref, gate_ref):
    tt = gate_ref.shape[0]
    half = PEER_SLOTS // 2

    def gather_order(a):
        for c in range(tt // LANES):
            cols = slice(c * LANES, (c + 1) * LANES)
            tile_ref = pos_ref.at[c]
            tile_ref[pl.ds(0, half, stride=2), :] = a[:half, cols]
            tile_ref[pl.ds(1, half, stride=2), :] = a[half:, cols]
        return jnp.concatenate([pos_ref[c].T for c in range(tt // LANES)], axis=0)

    ids = be_ref[...]
    gate_ref[...] = gather_order(bs_ref[...])
    hs_ref[...] = (gather_order(ids).astype(jnp.int32) // HALF_EXPERTS).astype(F32)
    lo = ids.T[:, :half].astype(jnp.int32)
    hi = jnp.concatenate([ids[half:], ids[half:]], axis=0).T[:, :half].astype(jnp.int32)
    up_off = lambda e: (e % HALF_EXPERTS) * SUBLANES
    rows_up_ref[...] = up_off(lo) | (up_off(hi) << 16)
    rows_down_ref[...] = (lo * DOWN_SLAB_ROWS) | ((hi * DOWN_SLAB_ROWS) << 16)


N_ROUTE_IN, N_DOWN_IN, N_ROUTE_OUT, N_ROUTE_SCRATCH = 8, 5, 5, 8


def _route_down_kernel(*refs, do_route, do_down):
    refs = list(refs)
    take = lambda count: [refs.pop(0) for _ in range(count)]
    route_in = take(N_ROUTE_IN) if do_route else None
    down_in = take(N_DOWN_IN) if do_down else None
    route_out = take(N_ROUTE_OUT) if do_route else None
    w_ref = take(1)[0] if do_down else None
    route_scratch = take(N_ROUTE_SCRATCH) if do_route else None
    down_scratch = take(2) if do_down else None

    if do_route:
        x_ref, shift_ref, scale_ref, gain_ref, wqh_ref, wql_ref, skh_ref, skl_ref = route_in
        rows_down_ref, rows_up_ref, hs_ref, gate_ref, xs_ref = route_out
        xh_ref, xl_ref, q_ref, tv_ref, ti_ref, bs_ref, be_ref, pos_ref = route_scratch
        _route_prologue(x_ref, shift_ref, scale_ref, gain_ref, xs_ref, xh_ref, xl_ref)
    if do_down:
        idx_ref, xs_in_ref, gate_in_ref, tab_ref, sel_ref = down_in
        gb_ref, z_ref = down_scratch

    quarter = TOKEN_GROUP // 4

    def step(h, carry):
        rows = []
        if do_route:
            _route_query(h, xh_ref, xl_ref, wqh_ref, wql_ref, q_ref)
        for part in range(4):
            if do_down:
                rows += _down_tokens(h, part * quarter, (part + 1) * quarter, idx_ref, xs_in_ref, tab_ref, gb_ref)
            if do_route and part < 2:
                _route_sub_keys(part, skh_ref, skl_ref, q_ref, tv_ref, ti_ref)
            if do_route and part == 2:
                _route_head(h, tv_ref, ti_ref, bs_ref, be_ref)
        if do_down:
            _store_z(h, rows, z_ref)
        return carry

    lax.fori_loop(0, PEER_HEADS, step, 0)
    if do_route:
        _route_epilogue(bs_ref, be_ref, pos_ref, rows_down_ref, rows_up_ref, hs_ref, gate_ref)
    if do_down:
        _down_epilogue(z_ref, sel_ref, gate_in_ref, w_ref)


ROUTE_DOWN_VMEM_LIMIT = 56 * 1024 * 1024


def _route_down(route_args, down_args, seq_len, *, interpret=False):
    tt = PEER_TOKEN_TILE
    assert tt // TOKEN_GROUP == PEER_HEADS and seq_len % tt == 0
    tiles = seq_len // tt
    tok = lambda i: (i, 0)
    whole = pl.BlockSpec(memory_space=pltpu.VMEM)
    slot_spec = pl.BlockSpec((tt, PEER_SLOTS), tok)
    pair_spec = pl.BlockSpec((tt, PEER_SLOTS // 2), tok)
    slot_shape = jax.ShapeDtypeStruct((seq_len, PEER_SLOTS), F32)
    pair_shape = jax.ShapeDtypeStruct((seq_len, PEER_SLOTS // 2), jnp.int32)
    in_specs, args, out_specs, out_shape, scratch = [], [], [], [], []
    if route_args is not None:
        x2, b, shift, scale, gain, wq_hi, wq_lo, sk_hi, sk_lo = route_args
        batch = shift.shape[0]
        per_seq = pl.BlockSpec((1, 1, D_MODEL), lambda i: (b, 0, 0))
        in_specs += [pl.BlockSpec((tt, D_MODEL), lambda i: (b * tiles + i, 0)), per_seq, per_seq,
                     pl.BlockSpec((1, D_MODEL), lambda i: (0, 0)), whole, whole, whole, whole]
        args += [x2, shift.reshape(batch, 1, D_MODEL), scale.reshape(batch, 1, D_MODEL),
                 gain.reshape(1, D_MODEL), wq_hi, wq_lo, sk_hi, sk_lo]
    if down_args is not None:
        rows_down, xs, gate, tab_u = down_args
        in_specs += [pl.BlockSpec((tt, PEER_SLOTS // 2), tok, memory_space=pltpu.SMEM),
                     pl.BlockSpec((tt, SUBROWS, LANES), lambda i: (i, 0, 0)), slot_spec, whole,
                     pl.BlockSpec((DOWN_ROWS, PEER_SLOTS), lambda i: (0, 0))]
        args += [rows_down, xs.reshape(seq_len, SUBROWS, LANES), gate, tab_u, _down_select_matrix()]
    if route_args is not None:
        out_specs += [pair_spec, pair_spec, slot_spec, slot_spec, pl.BlockSpec((tt, 2 * D_MODEL), tok)]
        out_shape += [pair_shape, pair_shape, slot_shape, slot_shape,
                      jax.ShapeDtypeStruct((seq_len, 2 * D_MODEL), BF16)]
    if down_args is not None:
        out_specs.append(slot_spec)
        out_shape.append(slot_shape)
    if route_args is not None:
        scratch += [pltpu.VMEM((tt, D_MODEL), BF16),
                    pltpu.VMEM((tt, D_MODEL), BF16),
                    pltpu.VMEM((tt, PEER_QUERY_DIM), F32),
                    pltpu.VMEM((2, PEER_TOPK, tt), F32),
                    pltpu.VMEM((2, PEER_TOPK, tt), jnp.int32),
                    pltpu.VMEM((PEER_SLOTS, tt), F32),
                    pltpu.VMEM((PEER_SLOTS, tt), F32),
                    pltpu.VMEM((tt // LANES, PEER_SLOTS, LANES), F32)]
    if down_args is not None:
        scratch += [pltpu.VMEM((PEER_SLOTS // 2 * SUBLANES, LANES), jnp.int32),
                    pltpu.VMEM((tt, DOWN_ROWS), F32)]
    outs = pl.pallas_call(
        functools.partial(_route_down_kernel, do_route=route_args is not None, do_down=down_args is not None),
        grid=(tiles,), in_specs=in_specs, out_specs=out_specs, out_shape=out_shape, scratch_shapes=scratch,
        compiler_params=pltpu.CompilerParams(dimension_semantics=("arbitrary",),
                                             vmem_limit_bytes=ROUTE_DOWN_VMEM_LIMIT),
        interpret=interpret, name="peer_route_down",
    )(*args)
    routed = tuple(outs[:N_ROUTE_OUT]) if route_args is not None else None
    w = outs[-1] if down_args is not None else None
    return routed, w


def _peer(x, gain, shift, scale, res_gate, wq, subkeys, tab_u, tab_v):
    B, S, D = x.shape
    x2 = x.reshape(B * S, D)
    per_head = lambda w: w.reshape(D, PEER_HEADS, PEER_QUERY_DIM).transpose(1, 0, 2)
    wq_hi, wq_lo = _split_hi_lo(per_head(wq))
    sk_hi, sk_lo = _split_hi_lo(subkeys)
    routed, rows_up, hs, ws = None, [], [], []
    for b in range(B + 1):
        route_args = (x2, b, shift, scale, gain, wq_hi, wq_lo, sk_hi, sk_lo) if b < B else None
        down_args = (routed[0], routed[4], routed[3], tab_u) if routed is not None else None
        routed, w = _route_down(route_args, down_args, S)
        if w is not None:
            ws.append(w)
        if routed is not None:
            rows_up.append(routed[1])
            hs.append(routed[2])
    y = _peer_up(jnp.concatenate(rows_up), jnp.concatenate(ws), jnp.concatenate(hs), x2, res_gate, tab_v, S)
    return y.reshape(B, S, D)


PROJ_TOKEN_TILE = 512
PROJ_VMEM_LIMIT = 48 * 1024 * 1024


def _ada_norm_tile(x, gain, shift, scale):
    xn = x * lax.rsqrt(jnp.mean(x * x, axis=-1, keepdims=True) + EPS) * gain
    return xn * (1.0 + scale) + shift


def _norm_proj_kernel(x_ref, shift_ref, scale_ref, gain_ref, w_ref, *rest, with_gate):
    x_hi, x_lo = _split_hi_lo(_ada_norm_tile(x_ref[...], gain_ref[...], shift_ref[0], scale_ref[0]))
    if with_gate:
        wgh_ref, wgl_ref, main_ref, gate_ref = rest
        gate_ref[...] = _dot3(x_hi, x_lo, wgh_ref[...], wgl_ref[...], ((1,), (0,)))
    else:
        (main_ref,) = rest
    main_ref[...] = jnp.dot(x_hi, w_ref[...], preferred_element_type=F32)


def _norm_proj(x2, shift, scale, gain, w_main, w_gate, seq_len, *, token_tile=PROJ_TOKEN_TILE, interpret=False):
    n = x2.shape[0]
    tt = token_tile
    assert seq_len % tt == 0 and n % seq_len == 0
    tiles_per_seq = seq_len // tt
    batch = n // seq_len
    m = w_main.shape[1]
    tok = lambda i: (i, 0)
    per_seq = pl.BlockSpec((1, 1, D_MODEL), lambda i: (i // tiles_per_seq, 0, 0))
    whole = pl.BlockSpec(memory_space=pltpu.VMEM)
    in_specs = [pl.BlockSpec((tt, D_MODEL), tok), per_seq, per_seq,
                pl.BlockSpec((1, D_MODEL), lambda i: (0, 0)), whole]
    args = [x2, shift.reshape(batch, 1, D_MODEL), scale.reshape(batch, 1, D_MODEL), gain.reshape(1, D_MODEL),
            w_main.astype(BF16)]
    out_specs = [pl.BlockSpec((tt, m), tok)]
    out_shape = [jax.ShapeDtypeStruct((n, m), F32)]
    if w_gate is not None:
        wg_hi, wg_lo = _split_hi_lo(jnp.pad(w_gate, ((0, 0), (0, LANES - w_gate.shape[1]))))
        in_specs += [whole, whole]
        args += [wg_hi, wg_lo]
        out_specs.append(pl.BlockSpec((tt, LANES), tok))
        out_shape.append(jax.ShapeDtypeStruct((n, LANES), F32))
    outs = pl.pallas_call(
        functools.partial(_norm_proj_kernel, with_gate=w_gate is not None),
        grid=(n // tt,), in_specs=in_specs, out_specs=out_specs, out_shape=out_shape,
        compiler_params=pltpu.CompilerParams(dimension_semantics=("arbitrary",),
                                             vmem_limit_bytes=PROJ_VMEM_LIMIT),
        interpret=interpret, name="norm_proj",
    )(*args)
    return (outs[0], outs[1]) if w_gate is not None else (outs[0], None)


def _out_proj_kernel(h_ref, og_ref, x_ref, g_ref, hg_ref, w_ref, o_ref, *, head_dim):
    h = h_ref[...]
    if head_dim is not None:
        parts = []
        for j in range(D_MODEL // head_dim):
            hb = h[:, j * head_dim:(j + 1) * head_dim]
            parts.append(hb * lax.rsqrt(jnp.mean(hb * hb, axis=-1, keepdims=True) + EPS))
        h = jnp.concatenate(parts, axis=1) * hg_ref[...]
    a = jax.nn.sigmoid(og_ref[...]) * h
    y = jnp.dot(a.astype(BF16), w_ref[...], preferred_element_type=F32)
    o_ref[...] = x_ref[...] + g_ref[0] * y


def _out_proj(h2, p, og_block, x2, g, h_gain, w_out, seq_len, head_dim, *, token_tile=PROJ_TOKEN_TILE,
              interpret=False):
    n = x2.shape[0]
    tt = token_tile
    tiles_per_seq = seq_len // tt
    batch = n // seq_len
    tok = lambda i: (i, 0)
    row = pl.BlockSpec((tt, D_MODEL), tok)
    gain = jnp.ones((1, D_MODEL), F32) if h_gain is None else h_gain.reshape(1, D_MODEL)
    return pl.pallas_call(
        functools.partial(_out_proj_kernel, head_dim=head_dim),
        grid=(n // tt,),
        in_specs=[row, pl.BlockSpec((tt, D_MODEL), lambda i: (i, og_block)), row,
                  pl.BlockSpec((1, 1, D_MODEL), lambda i: (i // tiles_per_seq, 0, 0)),
                  pl.BlockSpec((1, D_MODEL), lambda i: (0, 0)),
                  pl.BlockSpec(memory_space=pltpu.VMEM)],
        out_specs=row,
        out_shape=jax.ShapeDtypeStruct((n, D_MODEL), F32),
        compiler_params=pltpu.CompilerParams(dimension_semantics=("arbitrary",),
                                             vmem_limit_bytes=PROJ_VMEM_LIMIT),
        interpret=interpret, name="out_proj",
    )(h2, p, x2, g.reshape(batch, 1, D_MODEL), gain, w_out.astype(BF16))


MLSTM_STATE_W = MLSTM_V_DIM + LANES


def _mlstm_kernel(q_ref, k_ref, v_ref, gc_ref, gr_ref, h_ref, cn_ref, m_ref):
    H, DQK, DV, L = MLSTM_HEADS, MLSTM_QK_DIM, MLSTM_V_DIM, MLSTM_CHUNK

    @pl.when(pl.program_id(1) == 0)
    def _():
        cn_ref[...] = jnp.zeros_like(cn_ref)
        m_ref[...] = jnp.zeros_like(m_ref)

    tril = lax.broadcasted_iota(jnp.int32, (L, L), 1) <= lax.broadcasted_iota(jnp.int32, (L, L), 0)
    gc = gc_ref[...]
    gr = gr_ref[0]
    ones = jnp.ones((L, LANES), F32)
    for h in range(H):
        ig_col, b_col = gc[:, h:h + 1], gc[:, H + h:H + h + 1]
        ig_row, b_row = gr[h:h + 1, :], gr[H + h:H + h + 1, :]
        m_prev = m_ref[h:h + 1, 0:1]
        d_log = jnp.where(tril, b_col - b_row + ig_row, -jnp.inf)
        inter = b_col + m_prev
        m_t = jnp.maximum(inter, jnp.max(d_log, axis=1, keepdims=True))
        w = jnp.exp(d_log - m_t)
        a_inter = jnp.exp(inter - m_t)
        qh = (q_ref[:, h * DQK:(h + 1) * DQK] * (DQK ** -0.5)).astype(BF16)
        kf = k_ref[:, h * DQK:(h + 1) * DQK]
        vh = v_ref[:, h * DV:(h + 1) * DV]
        s = lax.dot_general(qh, kf.astype(BF16), (((1,), (1,)), ((), ())), preferred_element_type=F32) * w
        cn = cn_ref[h]
        qc = jnp.dot(qh, cn.astype(BF16), preferred_element_type=F32)
        num = a_inter * qc[:, :DV] + jnp.dot(s.astype(BF16), vh.astype(BF16), preferred_element_type=F32)
        den = a_inter * qc[:, DV:DV + 1] + jnp.sum(s, axis=1, keepdims=True)
        h_ref[:, h * DV:(h + 1) * DV] = num / jnp.maximum(jnp.abs(den), jnp.exp(-m_t))
        b_last = b_col[L - 1:L, :]
        g_col = b_last - b_col + ig_col
        m_new = jnp.maximum(b_last + m_prev, jnp.max(g_col, axis=0, keepdims=True))
        w_s = jnp.exp(g_col - m_new)
        decay = jnp.exp(b_last + m_prev - m_new)
        kw = (kf * w_s).astype(BF16)
        vaug = jnp.concatenate([vh, ones], axis=1).astype(BF16)
        cn_ref[h] = decay * cn + lax.dot_general(kw, vaug, (((0,), (0,)), ((), ())), preferred_element_type=F32)
        m_ref[h:h + 1, :] = jnp.broadcast_to(m_new, (1, LANES))


def _mlstm_scan(p, gates, b_i, b_f, batch, seq_len, *, interpret=False):
    H, L = MLSTM_HEADS, MLSTM_CHUNK
    n = p.shape[0]
    nc = seq_len // L
    ig = _softcap(gates[:, :H] + b_i)
    lf = jax.nn.log_sigmoid(_softcap(gates[:, H:2 * H] + b_f))
    b = jnp.cumsum(lf.reshape(n // L, L, H), axis=1).reshape(n, H)
    gc = jnp.concatenate([ig, b], axis=1)
    gr = gc.reshape(n // L, L, 2 * H).transpose(0, 2, 1)
    chunk = lambda bi, c: bi * nc + c
    return pl.pallas_call(
        _mlstm_kernel,
        grid=(batch, nc),
        in_specs=[pl.BlockSpec((L, MLSTM_QK_W), lambda bi, c: (chunk(bi, c), 0)),
                  pl.BlockSpec((L, MLSTM_QK_W), lambda bi, c: (chunk(bi, c), 1)),
                  pl.BlockSpec((L, MLSTM_V_W), lambda bi, c: (chunk(bi, c), 1)),
                  pl.BlockSpec((L, 2 * H), lambda bi, c: (chunk(bi, c), 0)),
                  pl.BlockSpec((1, 2 * H, L), lambda bi, c: (chunk(bi, c), 0, 0))],
        out_specs=pl.BlockSpec((L, D_MODEL), lambda bi, c: (chunk(bi, c), 0)),
        out_shape=jax.ShapeDtypeStruct((n, D_MODEL), F32),
        scratch_shapes=[pltpu.VMEM((H, MLSTM_QK_DIM, MLSTM_STATE_W), F32),
                        pltpu.VMEM((SUBLANES, LANES), F32)],
        compiler_params=pltpu.CompilerParams(dimension_semantics=("arbitrary", "arbitrary")),
        interpret=interpret, name="mlstm_scan",
    )(p, p, p, gc, gr)


FOX_BLOCK = 1024
FOX_QUERY_PART = 256
FOX_SUM_ROWS = 16
FOX_VMEM_LIMIT = 48 * 1024 * 1024


def _bias_columns(f, query_side):
    hi = f.astype(BF16).astype(F32)
    r1 = f - hi
    lo = r1.astype(BF16).astype(F32)
    lo2 = r1 - lo
    lane = lax.broadcasted_iota(jnp.int32, (f.shape[0], LANES), 1)
    if query_side:
        vals = jnp.where(lane == 0, hi, jnp.where(lane == 1, lo, jnp.where(lane == 2, lo2,
                         jnp.where(lane < 6, 1.0, 0.0))))
    else:
        vals = jnp.where(lane < 3, 1.0, jnp.where(lane == 3, -hi, jnp.where(lane == 4, -lo,
                         jnp.where(lane == 5, -lo2, 0.0))))
    return vals.astype(BF16)


def _head_column(fc, h):
    lane = lax.broadcasted_iota(jnp.int32, fc.shape, 1)
    return jnp.sum(jnp.where(lane == h, fc, 0.0), axis=1, keepdims=True)


def _rms_rows(t, gain):
    return t * lax.rsqrt(jnp.mean(t * t, axis=-1, keepdims=True) + EPS) * gain


def _fox_attn_kernel(q_ref, k_ref, v_ref, fq_ref, fk_ref, qg_ref, kg_ref, o_ref,
                     ka_ref, vt_ref, qa_ref, st_cur_ref, st_next_ref, *state_refs):
    i = pl.program_id(1)
    h = pl.program_id(0) % FOX_HEADS
    tq = q_ref.shape[0]
    nk = k_ref.shape[0] // tq
    hd = FOX_HEAD_DIM
    part = FOX_QUERY_PART
    n_parts = tq // part
    m_refs, acc_refs = state_refs[:n_parts], state_refs[n_parts:]

    @pl.when(i == 0)
    def _():
        def prep(c, carry):
            r = pl.ds(pl.multiple_of(c * tq, tq), tq)
            ka_ref[r, :hd] = _rms_rows(k_ref[r, :], kg_ref[...]).astype(BF16)
            ka_ref[r, hd:] = _bias_columns(_head_column(fk_ref[r, :], h), False)
            vt_ref[c, :hd, :] = v_ref[r, :].T.astype(BF16)
            vt_ref[c, hd:, :] = jnp.ones((FOX_SUM_ROWS, tq), BF16)
            return carry
        lax.fori_loop(0, nk, prep, 0)

    qn = _rms_rows(q_ref[...], qg_ref[...]) * (hd ** -0.5)
    qa_ref[:, :hd] = qn.astype(BF16)
    qa_ref[:, hd:] = _bias_columns(_head_column(fq_ref[...], h), True)
    for m_ref, acc_ref in zip(m_refs, acc_refs):
        m_ref[...] = jnp.full_like(m_ref, -jnp.inf)
        acc_ref[...] = jnp.zeros_like(acc_ref)

    def scores(j, st_ref):
        kblk = ka_ref[pl.ds(pl.multiple_of(j * tq, tq), tq), :]
        for c in range(n_parts):
            st_ref[c] = lax.dot_general(kblk, qa_ref[c * part:(c + 1) * part, :],
                                        (((1,), (1,)), ((), ())), preferred_element_type=F32)

    def softmax_pv(j, masked):
        vtblk = vt_ref[j]
        pts, alphas = [], []
        for c, m_ref in enumerate(m_refs):
            st = st_cur_ref[c]
            if masked:
                key = lax.broadcasted_iota(jnp.int32, st.shape, 0)
                qry = lax.broadcasted_iota(jnp.int32, st.shape, 1) + c * part
                st = jnp.where(key <= qry, st, -jnp.inf)
            m_prev = m_ref[0:1, :]
            m_new = jnp.maximum(m_prev, jnp.max(st, axis=0, keepdims=True))
            alphas.append(jnp.exp(m_prev - m_new))
            pts.append(jnp.exp(st - m_new).astype(BF16))
            m_ref[0:1, :] = m_new
        for pt, alpha, acc_ref in zip(pts, alphas, acc_refs):
            acc_ref[...] = alpha * acc_ref[...] + jnp.dot(vtblk, pt, preferred_element_type=F32)

    def body(j, carry):
        scores(j + 1, st_next_ref)
        softmax_pv(j, False)
        st_cur_ref[...] = st_next_ref[...]
        return carry

    scores(0, st_cur_ref)
    lax.fori_loop(0, i, body, 0)
    softmax_pv(i, True)
    for c, acc_ref in enumerate(acc_refs):
        acc = acc_ref[...]
        o_ref[c * part:(c + 1) * part, :] = (acc[:hd] / acc[hd:hd + 1]).T


def _fox_attention(pq, kv, f_cum, q_gain, k_gain, batch, seq_len, *, block=FOX_BLOCK, interpret=False):
    n = pq.shape[0]
    hd, nh = FOX_HEAD_DIM, FOX_HEADS
    nq = seq_len // block
    return pl.pallas_call(
        _fox_attn_kernel,
        grid=(batch * nh, nq),
        in_specs=[pl.BlockSpec((block, hd), lambda bh, i: ((bh // nh) * nq + i, bh % nh)),
                  pl.BlockSpec((seq_len, hd), lambda bh, i: (bh // nh, bh % nh)),
                  pl.BlockSpec((seq_len, hd), lambda bh, i: (bh // nh, nh + bh % nh)),
                  pl.BlockSpec((block, nh), lambda bh, i: ((bh // nh) * nq + i, 0)),
                  pl.BlockSpec((seq_len, nh), lambda bh, i: (bh // nh, 0)),
                  pl.BlockSpec((1, hd), lambda bh, i: (0, 0)),
                  pl.BlockSpec((1, hd), lambda bh, i: (0, 0))],
        out_specs=pl.BlockSpec((block, hd), lambda bh, i: ((bh // nh) * nq + i, bh % nh)),
        out_shape=jax.ShapeDtypeStruct((n, D_MODEL), F32),
        scratch_shapes=[pltpu.VMEM((seq_len, 2 * hd), BF16),
                        pltpu.VMEM((nq, hd + FOX_SUM_ROWS, block), BF16),
                        pltpu.VMEM((block, 2 * hd), BF16),
                        *[pltpu.VMEM((block // FOX_QUERY_PART, block, FOX_QUERY_PART), F32)] * 2,
                        *[pltpu.VMEM((SUBLANES, FOX_QUERY_PART), F32)] * (block // FOX_QUERY_PART),
                        *[pltpu.VMEM((hd + FOX_SUM_ROWS, FOX_QUERY_PART), F32)] * (block // FOX_QUERY_PART)],
        compiler_params=pltpu.CompilerParams(dimension_semantics=("arbitrary", "arbitrary"),
                                             vmem_limit_bytes=FOX_VMEM_LIMIT),
        interpret=interpret, name="fox_attention",
    )(pq, kv, kv, f_cum, f_cum, q_gain.reshape(1, hd), k_gain.reshape(1, hd))


def kernel(x, c, ada_w, ada_b, mix_norm, ffn_norm, a_w_in, a_b_i, a_b_f, a_h_norm, a_w_out,
           kv_ada_w, kv_ada_b, kv_norm, kv_w, kv_b_f, kv_k_norm, b_w_qo, b_q_norm, b_w_out,
           peer_wq, peer_subkeys, peer_u, peer_v):
    B, S, D = x.shape
    n = B * S
    cs = jax.nn.silu(c)
    x2 = x.reshape(n, D)
    kv = f_cum = None
    for l in range(DEPTH):
        mod = cs @ ada_w[l] + ada_b[l]
        sh1, sc1, g1, sh2, sc2, g2 = jnp.split(mod, N_ADA, axis=-1)
        if l < N_A_LAYERS:
            split = 2 * MLSTM_QK_W + MLSTM_V_W + D_MODEL
            p, gates = _norm_proj(x2, sh1, sc1, mix_norm[l], a_w_in[l][:, :split], a_w_in[l][:, split:], S)
            h = _mlstm_scan(p, gates, a_b_i[l], a_b_f[l], B, S)
            x2 = _out_proj(h, p, 2, x2, g1, a_h_norm[l], a_w_out[l], S, MLSTM_V_DIM)
        else:
            j = l - N_A_LAYERS
            pq, _ = _norm_proj(x2, sh1, sc1, mix_norm[l], b_w_qo[j], None, S)
            att = _fox_attention(pq, kv, f_cum, b_q_norm[j], kv_k_norm, B, S)
            x2 = _out_proj(att, pq, 1, x2, g1, None, b_w_out[j], S, None)
        x2 = _peer(x2.reshape(B, S, D), ffn_norm[l], sh2, sc2, g2, peer_wq[l], peer_subkeys[l],
                   _pack_down_table(peer_u[l]), _pack_expert_table(peer_v[l])).reshape(n, D)
        if l == N_A_LAYERS - 1:
            sh, sc = jnp.split(cs @ kv_ada_w + kv_ada_b, 2, axis=-1)
            kv, fg = _norm_proj(x2, sh, sc, kv_norm, kv_w[:, :2 * D], kv_w[:, 2 * D:], S)
            log_f = jax.nn.log_sigmoid(fg[:, :FOX_HEADS] + kv_b_f)
            f_cum = jnp.cumsum(log_f.reshape(B, S, FOX_HEADS), axis=1).reshape(n, FOX_HEADS)
    return x2.reshape(B, S, D)
```

```python
import functools
import math

import jax
import jax.numpy as jnp
from jax import lax
from jax.experimental import pallas as pl
from jax.experimental.pallas import tpu as pltpu

F32 = jnp.float32
BF16 = jnp.bfloat16

D_MODEL = 1024
DEPTH = 2
N_A_LAYERS = DEPTH // 2
EPS = 1e-6
N_ADA = 6

MLSTM_HEADS = 4
MLSTM_QK_DIM = D_MODEL // (2 * MLSTM_HEADS)
MLSTM_V_DIM = D_MODEL // MLSTM_HEADS
MLSTM_CHUNK = 64
GATE_SOFTCAP = 15.0
MLSTM_QK_W = MLSTM_HEADS * MLSTM_QK_DIM
MLSTM_V_W = MLSTM_HEADS * MLSTM_V_DIM

FOX_HEADS = 8
FOX_HEAD_DIM = D_MODEL // FOX_HEADS

PEER_HEADS = 8
PEER_KEYS = 128
PEER_EXPERTS = PEER_KEYS * PEER_KEYS
PEER_QUERY_DIM = 256
PEER_HALF = PEER_QUERY_DIM // 2
PEER_TOPK = 16

SUBLANES = 8
LANES = 128
ROW_WORDS = SUBLANES * LANES
assert ROW_WORDS == D_MODEL
PEER_SLOTS = PEER_HEADS * PEER_TOPK
HALF_EXPERTS = PEER_EXPERTS // 2
SUBROWS = 2 * SUBLANES
GATHER_ROWS = PEER_SLOTS * SUBROWS
DOWN_SLAB_ROWS = SUBLANES // 2
DOWN_ROWS = PEER_SLOTS * SUBLANES
TOKEN_GROUP = 2 * SUBLANES
PEER_TOKEN_TILE = 128
PEER_VMEM_LIMIT = 48 * 1024 * 1024


def _pack_expert_table(t):
    lo = lax.bitcast_convert_type(t[:HALF_EXPERTS].astype(BF16), jnp.uint16).astype(jnp.uint32)
    hi = lax.bitcast_convert_type(t[HALF_EXPERTS:].astype(BF16), jnp.uint16).astype(jnp.uint32)
    w = lo | (hi << 16)
    return lax.bitcast_convert_type(w, jnp.int32).reshape(HALF_EXPERTS, SUBLANES, LANES)


def _pack_down_table(t):
    bits = lax.bitcast_convert_type(t.astype(BF16), jnp.uint16).astype(jnp.uint32)
    bits = bits.reshape(t.shape[0], 2, DOWN_SLAB_ROWS, LANES)
    w = bits[:, 0] | (bits[:, 1] << 16)
    return lax.bitcast_convert_type(w, jnp.int32).reshape(t.shape[0] * DOWN_SLAB_ROWS, LANES)


def _slot_expand_matrix():
    slot = lax.broadcasted_iota(jnp.int32, (PEER_SLOTS, GATHER_ROWS), 0)
    sub = lax.broadcasted_iota(jnp.int32, (PEER_SLOTS, GATHER_ROWS), 1)
    return (sub // SUBROWS == slot).astype(BF16)


def _diag_mask():
    r = lax.broadcasted_iota(jnp.int32, (SUBLANES, GATHER_ROWS), 0)
    sub = lax.broadcasted_iota(jnp.int32, (SUBLANES, GATHER_ROWS), 1)
    return (sub % SUBROWS) // 2 == r


def _down_diag_mask():
    j = lax.broadcasted_iota(jnp.int32, (SUBLANES, DOWN_ROWS), 0)
    sub = lax.broadcasted_iota(jnp.int32, (SUBLANES, DOWN_ROWS), 1)
    return sub % SUBLANES == j


def _down_select_matrix():
    sub = lax.broadcasted_iota(jnp.int32, (DOWN_ROWS, PEER_SLOTS), 0)
    pos = lax.broadcasted_iota(jnp.int32, (DOWN_ROWS, PEER_SLOTS), 1)
    return (sub // SUBLANES == pos).astype(F32)


def _half_mask(hs, e):
    hsx = jnp.dot(hs.astype(BF16), e, preferred_element_type=F32)
    par = (lax.broadcasted_iota(jnp.int32, hsx.shape, 1) % 2).astype(F32)
    return hsx == par


def _gather_token(idx_ref, tab_ref, gb_ref, t):
    for s in range(PEER_SLOTS // 2):
        w = idx_ref[t, s]
        a = pl.multiple_of(w & 0xFFFF, SUBLANES)
        b = pl.multiple_of(lax.shift_right_logical(w, 16), SUBLANES)
        gb_ref[pl.ds(2 * s * SUBLANES, SUBLANES), :] = tab_ref[pl.ds(a, SUBLANES), :]
        gb_ref[pl.ds((2 * s + 1) * SUBLANES, SUBLANES), :] = tab_ref[pl.ds(b, SUBLANES), :]
    return pltpu.bitcast(gb_ref[...], BF16)


def _gather_token_down(idx_ref, tab_ref, gb_ref, t):
    for s in range(PEER_SLOTS // 2):
        w = idx_ref[t, s]
        a = pl.multiple_of(w & 0xFFFF, DOWN_SLAB_ROWS)
        b = pl.multiple_of(lax.shift_right_logical(w, 16), DOWN_SLAB_ROWS)
        gb_ref[pl.ds(s * SUBLANES, SUBLANES), :] = jnp.concatenate(
            [tab_ref[pl.ds(a, DOWN_SLAB_ROWS), :], tab_ref[pl.ds(b, DOWN_SLAB_ROWS), :]], axis=0)
    return pltpu.bitcast(gb_ref[...], BF16)


def _split_hi_lo(a):
    hi = a.astype(BF16)
    lo = (a - hi.astype(F32)).astype(BF16)
    return hi, lo


def _down_tokens(g, first, last, idx_ref, xs_ref, tab_ref, gb_ref):
    diag = _down_diag_mask()
    rows = []
    for i in range(first, last):
        t = g * TOKEN_GROUP + i
        gath = _gather_token_down(idx_ref, tab_ref, gb_ref, t)
        y = lax.dot_general(xs_ref[t], gath, (((1,), (1,)), ((), ())),
                            preferred_element_type=F32)
        y8 = y[:SUBLANES] + y[SUBLANES:]
        rows.append(jnp.sum(jnp.where(diag, y8, 0.0), axis=0, keepdims=True))
    return rows


def _store_z(g, rows, z_ref):
    z_ref[pl.ds(pl.multiple_of(g * TOKEN_GROUP, TOKEN_GROUP), TOKEN_GROUP), :] = jnp.concatenate(rows, axis=0)


def _down_epilogue(z_ref, sel_ref, gate_ref, w_ref):
    act = jnp.dot(z_ref[...], sel_ref[...], precision=lax.Precision.HIGHEST, preferred_element_type=F32)
    gelu = 0.5 * act * (1.0 + lax.erf(act * (1.0 / math.sqrt(2.0))))
    w_ref[...] = gate_ref[...] * gelu


def _peer_up_kernel(idx_ref, w_ref, hs_ref, x_ref, g_ref, tab_ref, e_ref, y_ref, gb_ref, ahi_ref, alo_ref):
    tt = w_ref.shape[0]
    diag = _diag_mask()
    e = e_ref[...]
    hm = _half_mask(hs_ref[...], e)
    w_hi, w_lo = _split_hi_lo(w_ref[...])
    ahi_ref[...] = jnp.where(hm, jnp.dot(w_hi, e, preferred_element_type=F32), 0.0)
    alo_ref[...] = jnp.where(hm, jnp.dot(w_lo, e, preferred_element_type=F32), 0.0)

    def group(g, carry):
        base = pl.multiple_of(g * TOKEN_GROUP, TOKEN_GROUP)
        a_hi = ahi_ref[pl.ds(base, TOKEN_GROUP), :]
        a_lo = alo_ref[pl.ds(base, TOKEN_GROUP), :]
        outs = []
        for i in range(TOKEN_GROUP):
            t = g * TOKEN_GROUP + i
            gath = _gather_token(idx_ref, tab_ref, gb_ref, t)
            lhs = jnp.concatenate(
                [jnp.where(diag, a_hi[i:i + 1, :], 0.0), jnp.where(diag, a_lo[i:i + 1, :], 0.0)],
                axis=0).astype(BF16)
            out = jnp.dot(lhs, gath, preferred_element_type=F32)
            outs.append(out[:SUBLANES] + out[SUBLANES:])
        rows = pl.ds(base, TOKEN_GROUP)
        for r in range(SUBLANES):
            cols = slice(r * LANES, (r + 1) * LANES)
            chunk = jnp.concatenate([o[r:r + 1, :] for o in outs], axis=0)
            y_ref[rows, cols] = x_ref[rows, cols] + g_ref[0][:, cols] * chunk
        return carry

    lax.fori_loop(0, tt // TOKEN_GROUP, group, 0)


def _peer_up(rows_up, w, hs, x2, g, tab_v, seq_len, *, token_tile=PEER_TOKEN_TILE, interpret=False):
    n = w.shape[0]
    tt = token_tile
    assert n % tt == 0 and tt % TOKEN_GROUP == 0 and seq_len % tt == 0
    tiles_per_seq = seq_len // tt
    batch = n // seq_len
    tok = lambda i: (i, 0)
    slot_spec = pl.BlockSpec((tt, PEER_SLOTS), tok)
    row_spec = pl.BlockSpec((tt, D_MODEL), tok)
    return pl.pallas_call(
        _peer_up_kernel,
        grid=(n // tt,),
        in_specs=[pl.BlockSpec((tt, PEER_SLOTS // 2), tok, memory_space=pltpu.SMEM), slot_spec, slot_spec,
                  row_spec, pl.BlockSpec((1, 1, D_MODEL), lambda i: (i // tiles_per_seq, 0, 0)),
                  pl.BlockSpec(memory_space=pltpu.VMEM),
                  pl.BlockSpec((PEER_SLOTS, GATHER_ROWS), lambda i: (0, 0))],
        out_specs=row_spec,
        out_shape=jax.ShapeDtypeStruct((n, D_MODEL), F32),
        scratch_shapes=[pltpu.VMEM((PEER_SLOTS * SUBLANES, LANES), jnp.int32),
                        pltpu.VMEM((tt, GATHER_ROWS), F32),
                        pltpu.VMEM((tt, GATHER_ROWS), F32)],
        compiler_params=pltpu.CompilerParams(dimension_semantics=("arbitrary",),
                                             vmem_limit_bytes=PEER_VMEM_LIMIT),
        interpret=interpret,
        name="peer_up",
    )(rows_up, w, hs, x2, g.reshape(batch, 1, D_MODEL), tab_v.reshape(HALF_EXPERTS * SUBLANES, LANES),
      _slot_expand_matrix())


def _softcap(z):
    return GATE_SOFTCAP * jnp.tanh(z / GATE_SOFTCAP)


def _dot3(a_hi, a_lo, b_hi, b_lo, dims):
    dot = functools.partial(lax.dot_general, dimension_numbers=(dims, ((), ())), preferred_element_type=F32)
    return dot(a_hi, b_hi) + (dot(a_hi, b_lo) + dot(a_lo, b_hi))


def _top_rows(s, k):
    nrows = s.shape[0]
    row = lax.broadcasted_iota(jnp.int32, s.shape, 0)
    vals, rows = [], []
    for _ in range(k):
        m = jnp.max(s, axis=0, keepdims=True)
        r = jnp.min(jnp.where(s == m, row, nrows), axis=0, keepdims=True)
        vals.append(m)
        rows.append(r)
        s = jnp.where(row == r, -jnp.inf, s)
    return jnp.concatenate(vals, axis=0), jnp.concatenate(rows, axis=0)


def _tree(op, xs):
    while len(xs) > 1:
        xs = [op(xs[i], xs[i + 1]) if i + 1 < len(xs) else xs[i] for i in range(0, len(xs), 2)]
    return xs[0]


def _product_candidates(s0, s1, i0, i1):
    k = PEER_TOPK
    t = s0.shape[1]
    sub = lax.broadcasted_iota(jnp.int32, (SUBLANES, t), 0)
    blocks = []
    for half in range(k // SUBLANES):
        b = sub + half * SUBLANES
        lo = half * SUBLANES
        blocks.append((s0[0:1] + s1[lo:lo + SUBLANES], b, i0[0:1] * PEER_KEYS + i1[lo:lo + SUBLANES]))
    for a in range(1, SUBLANES):
        valid = (a + 1) * (sub + 1) <= k
        blocks.append((jnp.where(valid, s0[a:a + 1] + s1[0:SUBLANES], -jnp.inf), a * k + sub,
                       i0[a:a + 1] * PEER_KEYS + i1[0:SUBLANES]))
    a = sub + SUBLANES
    blocks.append((s0[SUBLANES:k] + s1[0:1], a * k, i0[SUBLANES:k] * PEER_KEYS + i1[0:1]))
    return blocks


def _route_prologue(x_ref, shift_ref, scale_ref, gain_ref, xs_ref, xh_ref, xl_ref):
    x = x_ref[...]
    xn = x * lax.rsqrt(jnp.mean(x * x, axis=-1, keepdims=True) + EPS) * gain_ref[...]
    xn = xn * (1.0 + scale_ref[0]) + shift_ref[0]
    x_hi, x_lo = _split_hi_lo(xn)
    for j in range(SUBLANES):
        c = (j % 2) * DOWN_SLAB_ROWS + j // 2
        xs_ref[:, j * LANES:(j + 1) * LANES] = x_hi[:, c * LANES:(c + 1) * LANES]
        xs_ref[:, D_MODEL + j * LANES:D_MODEL + (j + 1) * LANES] = x_lo[:, c * LANES:(c + 1) * LANES]
    xh_ref[...] = x_hi
    xl_ref[...] = x_lo


def _route_query(h, xh_ref, xl_ref, wqh_ref, wql_ref, q_ref):
    q_ref[...] = _dot3(xh_ref[...], xl_ref[...], wqh_ref[h], wql_ref[h], ((1,), (0,)))


def _route_sub_keys(p, skh_ref, skl_ref, q_ref, tv_ref, ti_ref):
    q_hi, q_lo = _split_hi_lo(q_ref[:, p * PEER_HALF:(p + 1) * PEER_HALF])
    s = _dot3(skh_ref[p], skl_ref[p], q_hi, q_lo, ((1,), (1,)))
    tv_ref[p], ti_ref[p] = _top_rows(s, PEER_TOPK)


def _route_head(h, tv_ref, ti_ref, bs_ref, be_ref):
    k = PEER_TOPK
    blocks = _product_candidates(tv_ref[0], tv_ref[1], ti_ref[0], ti_ref[1])
    sums = [b[0] for b in blocks]
    best_s, best_e = [], []
    for _ in range(k):
        m = jnp.max(_tree(jnp.maximum, sums), axis=0, keepdims=True)
        pos = jnp.min(_tree(jnp.minimum, [jnp.where(c == m, b[1], k * k) for c, b in zip(sums, blocks)]),
                      axis=0, keepdims=True)
        hit = [b[1] == pos for b in blocks]
        e = jnp.max(_tree(jnp.maximum, [jnp.where(hh, b[2], -1) for hh, b in zip(hit, blocks)]),
                    axis=0, keepdims=True)
        sums = [jnp.where(hh, -jnp.inf, c) for hh, c in zip(hit, sums)]
        best_s.append(m)
        best_e.append(e)
    bs = jnp.concatenate(best_s, axis=0)
    ex = jnp.exp(bs - bs[0:1])
    off = pl.multiple_of(h * k, k)
    bs_ref[pl.ds(off, k), :] = ex / jnp.sum(ex, axis=0, keepdims=True)
    be_ref[pl.ds(off, k), :] = jnp.concatenate(best_e, axis=0).astype(F32)


def _route_epilogue(bs_ref, be_ref, pos_ref, rows_down_ref, rows_up_ref, hs_ref, gate_ref):
    tt = gate_ref.shape[0]
    half = PEER_SLOTS // 2

    def gather_order(a):
        for c in range(tt // LANES):
            cols = slice(c * LANES, (c + 1) * LANES)
            tile_ref = pos_ref.at[c]
            tile_ref[pl.ds(0, half, stride=2), :] = a[:half, cols]
            tile_ref[pl.ds(1, half, stride=2), :] = a[half:, cols]
        return jnp.concatenate([pos_ref[c].T for c in range(tt // LANES)], axis=0)

    ids = be_ref[...]
    gate_ref[...] = gather_order(bs_ref[...])
    hs_ref[...] = (gather_order(ids).astype(jnp.int32) // HALF_EXPERTS).astype(F32)
    lo = ids.T[:, :half].astype(jnp.int32)
    hi = jnp.concatenate([ids[half:], ids[half:]], axis=0).T[:, :half].astype(jnp.int32)
    up_off = lambda e: (e % HALF_EXPERTS) * SUBLANES
    rows_up_ref[...] = up_off(lo) | (up_off(hi) << 16)
    rows_down_ref[...] = (lo * DOWN_SLAB_ROWS) | ((hi * DOWN_SLAB_ROWS) << 16)


N_ROUTE_IN, N_DOWN_IN, N_ROUTE_OUT, N_ROUTE_SCRATCH = 8, 5, 5, 8


def _route_down_kernel(*refs, do_route, do_down):
    refs = list(refs)
    take = lambda count: [refs.pop(0) for _ in range(count)]
    route_in = take(N_ROUTE_IN) if do_route else None
    down_in = take(N_DOWN_IN) if do_down else None
    route_out = take(N_ROUTE_OUT) if do_route else None
    w_ref = take(1)[0] if do_down else None
    route_scratch = take(N_ROUTE_SCRATCH) if do_route else None
    down_scratch = take(2) if do_down else None

    if do_route:
        x_ref, shift_ref, scale_ref, gain_ref, wqh_ref, wql_ref, skh_ref, skl_ref = route_in
        rows_down_ref, rows_up_ref, hs_ref, gate_ref, xs_ref = route_out
        xh_ref, xl_ref, q_ref, tv_ref, ti_ref, bs_ref, be_ref, pos_ref = route_scratch
        _route_prologue(x_ref, shift_ref, scale_ref, gain_ref, xs_ref, xh_ref, xl_ref)
    if do_down:
        idx_ref, xs_in_ref, gate_in_ref, tab_ref, sel_ref = down_in
        gb_ref, z_ref = down_scratch

    quarter = TOKEN_GROUP // 4

    def step(h, carry):
        rows = []
        if do_route:
            _route_query(h, xh_ref, xl_ref, wqh_ref, wql_ref, q_ref)
        for part in range(4):
            if do_down:
                rows += _down_tokens(h, part * quarter, (part + 1) * quarter, idx_ref, xs_in_ref, tab_ref, gb_ref)
            if do_route and part < 2:
                _route_sub_keys(part, skh_ref, skl_ref, q_ref, tv_ref, ti_ref)
            if do_route and part == 2:
                _route_head(h, tv_ref, ti_ref, bs_ref, be_ref)
        if do_down:
            _store_z(h, rows, z_ref)
        return carry

    lax.fori_loop(0, PEER_HEADS, step, 0)
    if do_route:
        _route_epilogue(bs_ref, be_ref, pos_ref, rows_down_ref, rows_up_ref, hs_ref, gate_ref)
    if do_down:
        _down_epilogue(z_ref, sel_ref, gate_in_ref, w_ref)


ROUTE_DOWN_VMEM_LIMIT = 56 * 1024 * 1024
PEER_CHUNKS_PER_SEQ = 2


def _route_down(route_args, down_args, seq_len, *, interpret=False):
    tt = PEER_TOKEN_TILE
    assert tt // TOKEN_GROUP == PEER_HEADS and seq_len % tt == 0
    tiles = seq_len // tt
    tok = lambda i: (i, 0)
    whole = pl.BlockSpec(memory_space=pltpu.VMEM)
    slot_spec = pl.BlockSpec((tt, PEER_SLOTS), tok)
    pair_spec = pl.BlockSpec((tt, PEER_SLOTS // 2), tok)
    slot_shape = jax.ShapeDtypeStruct((seq_len, PEER_SLOTS), F32)
    pair_shape = jax.ShapeDtypeStruct((seq_len, PEER_SLOTS // 2), jnp.int32)
    in_specs, args, out_specs, out_shape, scratch = [], [], [], [], []
    if route_args is not None:
        x2, chunk, seq, shift, scale, gain, wq_hi, wq_lo, sk_hi, sk_lo = route_args
        batch = shift.shape[0]
        per_seq = pl.BlockSpec((1, 1, D_MODEL), lambda i: (seq, 0, 0))
        in_specs += [pl.BlockSpec((tt, D_MODEL), lambda i: (chunk * tiles + i, 0)), per_seq, per_seq,
                     pl.BlockSpec((1, D_MODEL), lambda i: (0, 0)), whole, whole, whole, whole]
        args += [x2, shift.reshape(batch, 1, D_MODEL), scale.reshape(batch, 1, D_MODEL),
                 gain.reshape(1, D_MODEL), wq_hi, wq_lo, sk_hi, sk_lo]
    if down_args is not None:
        rows_down, xs, gate, tab_u = down_args
        in_specs += [pl.BlockSpec((tt, PEER_SLOTS // 2), tok, memory_space=pltpu.SMEM),
                     pl.BlockSpec((tt, SUBROWS, LANES), lambda i: (i, 0, 0)), slot_spec, whole,
                     pl.BlockSpec((DOWN_ROWS, PEER_SLOTS), lambda i: (0, 0))]
        args += [rows_down, xs.reshape(seq_len, SUBROWS, LANES), gate, tab_u, _down_select_matrix()]
    if route_args is not None:
        out_specs += [pair_spec, pair_spec, slot_spec, slot_spec, pl.BlockSpec((tt, 2 * D_MODEL), tok)]
        out_shape += [pair_shape, pair_shape, slot_shape, slot_shape,
                      jax.ShapeDtypeStruct((seq_len, 2 * D_MODEL), BF16)]
    if down_args is not None:
        out_specs.append(slot_spec)
        out_shape.append(slot_shape)
    if route_args is not None:
        scratch += [pltpu.VMEM((tt, D_MODEL), BF16),
                    pltpu.VMEM((tt, D_MODEL), BF16),
                    pltpu.VMEM((tt, PEER_QUERY_DIM), F32),
                    pltpu.VMEM((2, PEER_TOPK, tt), F32),
                    pltpu.VMEM((2, PEER_TOPK, tt), jnp.int32),
                    pltpu.VMEM((PEER_SLOTS, tt), F32),
                    pltpu.VMEM((PEER_SLOTS, tt), F32),
                    pltpu.VMEM((tt // LANES, PEER_SLOTS, LANES), F32)]
    if down_args is not None:
        scratch += [pltpu.VMEM((PEER_SLOTS // 2 * SUBLANES, LANES), jnp.int32),
                    pltpu.VMEM((tt, DOWN_ROWS), F32)]
    outs = pl.pallas_call(
        functools.partial(_route_down_kernel, do_route=route_args is not None, do_down=down_args is not None),
        grid=(tiles,), in_specs=in_specs, out_specs=out_specs, out_shape=out_shape, scratch_shapes=scratch,
        compiler_params=pltpu.CompilerParams(dimension_semantics=("arbitrary",),
                                             vmem_limit_bytes=ROUTE_DOWN_VMEM_LIMIT),
        interpret=interpret, name="peer_route_down",
    )(*args)
    routed = tuple(outs[:N_ROUTE_OUT]) if route_args is not None else None
    w = outs[-1] if down_args is not None else None
    return routed, w


def _peer(x, gain, shift, scale, res_gate, wq, subkeys, tab_u, tab_v):
    B, S, D = x.shape
    x2 = x.reshape(B * S, D)
    per_head = lambda w: w.reshape(D, PEER_HEADS, PEER_QUERY_DIM).transpose(1, 0, 2)
    wq_hi, wq_lo = _split_hi_lo(per_head(wq))
    sk_hi, sk_lo = _split_hi_lo(subkeys)
    routed, rows_up, hs, ws = None, [], [], []
    chunk_len = S // PEER_CHUNKS_PER_SEQ
    n_chunks = B * PEER_CHUNKS_PER_SEQ
    for c in range(n_chunks + 1):
        route_args = ((x2, c, c // PEER_CHUNKS_PER_SEQ, shift, scale, gain, wq_hi, wq_lo, sk_hi, sk_lo)
                      if c < n_chunks else None)
        down_args = (routed[0], routed[4], routed[3], tab_u) if routed is not None else None
        routed, w = _route_down(route_args, down_args, chunk_len)
        if w is not None:
            ws.append(w)
        if routed is not None:
            rows_up.append(routed[1])
            hs.append(routed[2])
    y = _peer_up(jnp.concatenate(rows_up), jnp.concatenate(ws), jnp.concatenate(hs), x2, res_gate, tab_v, S)
    return y.reshape(B, S, D)


PROJ_TOKEN_TILE = 512
PROJ_VMEM_LIMIT = 48 * 1024 * 1024


def _ada_norm_tile(x, gain, shift, scale):
    xn = x * lax.rsqrt(jnp.mean(x * x, axis=-1, keepdims=True) + EPS) * gain
    return xn * (1.0 + scale) + shift


def _norm_proj_kernel(x_ref, shift_ref, scale_ref, gain_ref, w_ref, *rest, with_gate):
    x_hi, x_lo = _split_hi_lo(_ada_norm_tile(x_ref[...], gain_ref[...], shift_ref[0], scale_ref[0]))
    if with_gate:
        wgh_ref, wgl_ref, main_ref, gate_ref = rest
        gate_ref[...] = _dot3(x_hi, x_lo, wgh_ref[...], wgl_ref[...], ((1,), (0,)))
    else:
        (main_ref,) = rest
    main_ref[...] = jnp.dot(x_hi, w_ref[...], preferred_element_type=F32)


def _norm_proj(x2, shift, scale, gain, w_main, w_gate, seq_len, *, token_tile=PROJ_TOKEN_TILE, interpret=False):
    n = x2.shape[0]
    tt = token_tile
    assert seq_len % tt == 0 and n % seq_len == 0
    tiles_per_seq = seq_len // tt
    batch = n // seq_len
    m = w_main.shape[1]
    tok = lambda i: (i, 0)
    per_seq = pl.BlockSpec((1, 1, D_MODEL), lambda i: (i // tiles_per_seq, 0, 0))
    whole = pl.BlockSpec(memory_space=pltpu.VMEM)
    in_specs = [pl.BlockSpec((tt, D_MODEL), tok), per_seq, per_seq,
                pl.BlockSpec((1, D_MODEL), lambda i: (0, 0)), whole]
    args = [x2, shift.reshape(batch, 1, D_MODEL), scale.reshape(batch, 1, D_MODEL), gain.reshape(1, D_MODEL),
            w_main.astype(BF16)]
    out_specs = [pl.BlockSpec((tt, m), tok)]
    out_shape = [jax.ShapeDtypeStruct((n, m), F32)]
    if w_gate is not None:
        wg_hi, wg_lo = _split_hi_lo(jnp.pad(w_gate, ((0, 0), (0, LANES - w_gate.shape[1]))))
        in_specs += [whole, whole]
        args += [wg_hi, wg_lo]
        out_specs.append(pl.BlockSpec((tt, LANES), tok))
        out_shape.append(jax.ShapeDtypeStruct((n, LANES), F32))
    outs = pl.pallas_call(
        functools.partial(_norm_proj_kernel, with_gate=w_gate is not None),
        grid=(n // tt,), in_specs=in_specs, out_specs=out_specs, out_shape=out_shape,
        compiler_params=pltpu.CompilerParams(dimension_semantics=("arbitrary",),
                                             vmem_limit_bytes=PROJ_VMEM_LIMIT),
        interpret=interpret, name="norm_proj",
    )(*args)
    return (outs[0], outs[1]) if w_gate is not None else (outs[0], None)


def _out_proj_kernel(h_ref, og_ref, x_ref, g_ref, hg_ref, w_ref, o_ref, *, head_dim):
    h = h_ref[...]
    if head_dim is not None:
        parts = []
        for j in range(D_MODEL // head_dim):
            hb = h[:, j * head_dim:(j + 1) * head_dim]
            parts.append(hb * lax.rsqrt(jnp.mean(hb * hb, axis=-1, keepdims=True) + EPS))
        h = jnp.concatenate(parts, axis=1) * hg_ref[...]
    a = jax.nn.sigmoid(og_ref[...]) * h
    y = jnp.dot(a.astype(BF16), w_ref[...], preferred_element_type=F32)
    o_ref[...] = x_ref[...] + g_ref[0] * y


def _out_proj(h2, p, og_block, x2, g, h_gain, w_out, seq_len, head_dim, *, token_tile=PROJ_TOKEN_TILE,
              interpret=False):
    n = x2.shape[0]
    tt = token_tile
    tiles_per_seq = seq_len // tt
    batch = n // seq_len
    tok = lambda i: (i, 0)
    row = pl.BlockSpec((tt, D_MODEL), tok)
    gain = jnp.ones((1, D_MODEL), F32) if h_gain is None else h_gain.reshape(1, D_MODEL)
    return pl.pallas_call(
        functools.partial(_out_proj_kernel, head_dim=head_dim),
        grid=(n // tt,),
        in_specs=[row, pl.BlockSpec((tt, D_MODEL), lambda i: (i, og_block)), row,
                  pl.BlockSpec((1, 1, D_MODEL), lambda i: (i // tiles_per_seq, 0, 0)),
                  pl.BlockSpec((1, D_MODEL), lambda i: (0, 0)),
                  pl.BlockSpec(memory_space=pltpu.VMEM)],
        out_specs=row,
        out_shape=jax.ShapeDtypeStruct((n, D_MODEL), F32),
        compiler_params=pltpu.CompilerParams(dimension_semantics=("arbitrary",),
                                             vmem_limit_bytes=PROJ_VMEM_LIMIT),
        interpret=interpret, name="out_proj",
    )(h2, p, x2, g.reshape(batch, 1, D_MODEL), gain, w_out.astype(BF16))


MLSTM_STATE_W = MLSTM_V_DIM + LANES


def _mlstm_kernel(q_ref, k_ref, v_ref, gc_ref, gr_ref, h_ref, cn_ref, m_ref):
    H, DQK, DV, L = MLSTM_HEADS, MLSTM_QK_DIM, MLSTM_V_DIM, MLSTM_CHUNK

    @pl.when(pl.program_id(1) == 0)
    def _():
        cn_ref[...] = jnp.zeros_like(cn_ref)
        m_ref[...] = jnp.zeros_like(m_ref)

    tril = lax.broadcasted_iota(jnp.int32, (L, L), 1) <= lax.broadcasted_iota(jnp.int32, (L, L), 0)
    gc = gc_ref[...]
    gr = gr_ref[0]
    ones = jnp.ones((L, LANES), F32)
    for h in range(H):
        ig_col, b_col = gc[:, h:h + 1], gc[:, H + h:H + h + 1]
        ig_row, b_row = gr[h:h + 1, :], gr[H + h:H + h + 1, :]
        m_prev = m_ref[h:h + 1, 0:1]
        d_log = jnp.where(tril, b_col - b_row + ig_row, -jnp.inf)
        inter = b_col + m_prev
        m_t = jnp.maximum(inter, jnp.max(d_log, axis=1, keepdims=True))
        w = jnp.exp(d_log - m_t)
        a_inter = jnp.exp(inter - m_t)
        qh = (q_ref[:, h * DQK:(h + 1) * DQK] * (DQK ** -0.5)).astype(BF16)
        kf = k_ref[:, h * DQK:(h + 1) * DQK]
        vh = v_ref[:, h * DV:(h + 1) * DV]
        s = lax.dot_general(qh, kf.astype(BF16), (((1,), (1,)), ((), ())), preferred_element_type=F32) * w
        cn = cn_ref[h]
        qc = jnp.dot(qh, cn.astype(BF16), preferred_element_type=F32)
        num = a_inter * qc[:, :DV] + jnp.dot(s.astype(BF16), vh.astype(BF16), preferred_element_type=F32)
        den = a_inter * qc[:, DV:DV + 1] + jnp.sum(s, axis=1, keepdims=True)
        h_ref[:, h * DV:(h + 1) * DV] = num / jnp.maximum(jnp.abs(den), jnp.exp(-m_t))
        b_last = b_col[L - 1:L, :]
        g_col = b_last - b_col + ig_col
        m_new = jnp.maximum(b_last + m_prev, jnp.max(g_col, axis=0, keepdims=True))
        w_s = jnp.exp(g_col - m_new)
        decay = jnp.exp(b_last + m_prev - m_new)
        kw = (kf * w_s).astype(BF16)
        vaug = jnp.concatenate([vh, ones], axis=1).astype(BF16)
        cn_ref[h] = decay * cn + lax.dot_general(kw, vaug, (((0,), (0,)), ((), ())), preferred_element_type=F32)
        m_ref[h:h + 1, :] = jnp.broadcast_to(m_new, (1, LANES))


def _mlstm_scan(p, gates, b_i, b_f, batch, seq_len, *, interpret=False):
    H, L = MLSTM_HEADS, MLSTM_CHUNK
    n = p.shape[0]
    nc = seq_len // L
    ig = _softcap(gates[:, :H] + b_i)
    lf = jax.nn.log_sigmoid(_softcap(gates[:, H:2 * H] + b_f))
    b = jnp.cumsum(lf.reshape(n // L, L, H), axis=1).reshape(n, H)
    gc = jnp.concatenate([ig, b], axis=1)
    gr = gc.reshape(n // L, L, 2 * H).transpose(0, 2, 1)
    chunk = lambda bi, c: bi * nc + c
    return pl.pallas_call(
        _mlstm_kernel,
        grid=(batch, nc),
        in_specs=[pl.BlockSpec((L, MLSTM_QK_W), lambda bi, c: (chunk(bi, c), 0)),
                  pl.BlockSpec((L, MLSTM_QK_W), lambda bi, c: (chunk(bi, c), 1)),
                  pl.BlockSpec((L, MLSTM_V_W), lambda bi, c: (chunk(bi, c), 1)),
                  pl.BlockSpec((L, 2 * H), lambda bi, c: (chunk(bi, c), 0)),
                  pl.BlockSpec((1, 2 * H, L), lambda bi, c: (chunk(bi, c), 0, 0))],
        out_specs=pl.BlockSpec((L, D_MODEL), lambda bi, c: (chunk(bi, c), 0)),
        out_shape=jax.ShapeDtypeStruct((n, D_MODEL), F32),
        scratch_shapes=[pltpu.VMEM((H, MLSTM_QK_DIM, MLSTM_STATE_W), F32),
                        pltpu.VMEM((SUBLANES, LANES), F32)],
        compiler_params=pltpu.CompilerParams(dimension_semantics=("arbitrary", "arbitrary")),
        interpret=interpret, name="mlstm_scan",
    )(p, p, p, gc, gr)


FOX_BLOCK = 1024
FOX_QUERY_PART = 256
FOX_SUM_ROWS = 16
FOX_VMEM_LIMIT = 48 * 1024 * 1024


def _bias_columns(f, query_side):
    hi = f.astype(BF16).astype(F32)
    r1 = f - hi
    lo = r1.astype(BF16).astype(F32)
    lo2 = r1 - lo
    lane = lax.broadcasted_iota(jnp.int32, (f.shape[0], LANES), 1)
    if query_side:
        vals = jnp.where(lane == 0, hi, jnp.where(lane == 1, lo, jnp.where(lane == 2, lo2,
                         jnp.where(lane < 6, 1.0, 0.0))))
    else:
        vals = jnp.where(lane < 3, 1.0, jnp.where(lane == 3, -hi, jnp.where(lane == 4, -lo,
                         jnp.where(lane == 5, -lo2, 0.0))))
    return vals.astype(BF16)


def _head_column(fc, h):
    lane = lax.broadcasted_iota(jnp.int32, fc.shape, 1)
    return jnp.sum(jnp.where(lane == h, fc, 0.0), axis=1, keepdims=True)


def _rms_rows(t, gain):
    return t * lax.rsqrt(jnp.mean(t * t, axis=-1, keepdims=True) + EPS) * gain


def _fox_attn_kernel(q_ref, k_ref, v_ref, fq_ref, fk_ref, qg_ref, kg_ref, o_ref,
                     ka_ref, vt_ref, qa_ref, st_cur_ref, st_next_ref, *state_refs):
    i = pl.program_id(1)
    h = pl.program_id(0) % FOX_HEADS
    tq = q_ref.shape[0]
    nk = k_ref.shape[0] // tq
    hd = FOX_HEAD_DIM
    part = FOX_QUERY_PART
    n_parts = tq // part
    m_refs, acc_refs = state_refs[:n_parts], state_refs[n_parts:]

    @pl.when(i == 0)
    def _():
        def prep(c, carry):
            r = pl.ds(pl.multiple_of(c * tq, tq), tq)
            ka_ref[r, :hd] = _rms_rows(k_ref[r, :], kg_ref[...]).astype(BF16)
            ka_ref[r, hd:] = _bias_columns(_head_column(fk_ref[r, :], h), False)
            vt_ref[c, :hd, :] = v_ref[r, :].T.astype(BF16)
            vt_ref[c, hd:, :] = jnp.ones((FOX_SUM_ROWS, tq), BF16)
            return carry
        lax.fori_loop(0, nk, prep, 0)

    qn = _rms_rows(q_ref[...], qg_ref[...]) * (hd ** -0.5)
    qa_ref[:, :hd] = qn.astype(BF16)
    qa_ref[:, hd:] = _bias_columns(_head_column(fq_ref[...], h), True)
    for m_ref, acc_ref in zip(m_refs, acc_refs):
        m_ref[...] = jnp.full_like(m_ref, -jnp.inf)
        acc_ref[...] = jnp.zeros_like(acc_ref)

    def scores(j, st_ref):
        kblk = ka_ref[pl.ds(pl.multiple_of(j * tq, tq), tq), :]
        for c in range(n_parts):
            st_ref[c] = lax.dot_general(kblk, qa_ref[c * part:(c + 1) * part, :],
                                        (((1,), (1,)), ((), ())), preferred_element_type=F32)

    def softmax_pv(j, masked):
        vtblk = vt_ref[j]
        pts, alphas = [], []
        for c, m_ref in enumerate(m_refs):
            st = st_cur_ref[c]
            if masked:
                key = lax.broadcasted_iota(jnp.int32, st.shape, 0)
                qry = lax.broadcasted_iota(jnp.int32, st.shape, 1) + c * part
                st = jnp.where(key <= qry, st, -jnp.inf)
            m_prev = m_ref[0:1, :]
            m_new = jnp.maximum(m_prev, jnp.max(st, axis=0, keepdims=True))
            alphas.append(jnp.exp(m_prev - m_new))
            pts.append(jnp.exp(st - m_new).astype(BF16))
            m_ref[0:1, :] = m_new
        for pt, alpha, acc_ref in zip(pts, alphas, acc_refs):
            acc_ref[...] = alpha * acc_ref[...] + jnp.dot(vtblk, pt, preferred_element_type=F32)

    def body(j, carry):
        scores(j + 1, st_next_ref)
        softmax_pv(j, False)
        st_cur_ref[...] = st_next_ref[...]
        return carry

    scores(0, st_cur_ref)
    lax.fori_loop(0, i, body, 0)
    softmax_pv(i, True)
    for c, acc_ref in enumerate(acc_refs):
        acc = acc_ref[...]
        o_ref[c * part:(c + 1) * part, :] = (acc[:hd] / acc[hd:hd + 1]).T


def _fox_attention(pq, kv, f_cum, q_gain, k_gain, batch, seq_len, *, block=FOX_BLOCK, interpret=False):
    n = pq.shape[0]
    hd, nh = FOX_HEAD_DIM, FOX_HEADS
    nq = seq_len // block
    return pl.pallas_call(
        _fox_attn_kernel,
        grid=(batch * nh, nq),
        in_specs=[pl.BlockSpec((block, hd), lambda bh, i: ((bh // nh) * nq + i, bh % nh)),
                  pl.BlockSpec((seq_len, hd), lambda bh, i: (bh // nh, bh % nh)),
                  pl.BlockSpec((seq_len, hd), lambda bh, i: (bh // nh, nh + bh % nh)),
                  pl.BlockSpec((block, nh), lambda bh, i: ((bh // nh) * nq + i, 0)),
                  pl.BlockSpec((seq_len, nh), lambda bh, i: (bh // nh, 0)),
                  pl.BlockSpec((1, hd), lambda bh, i: (0, 0)),
                  pl.BlockSpec((1, hd), lambda bh, i: (0, 0))],
        out_specs=pl.BlockSpec((block, hd), lambda bh, i: ((bh // nh) * nq + i, bh % nh)),
        out_shape=jax.ShapeDtypeStruct((n, D_MODEL), F32),
        scratch_shapes=[pltpu.VMEM((seq_len, 2 * hd), BF16),
                        pltpu.VMEM((nq, hd + FOX_SUM_ROWS, block), BF16),
                        pltpu.VMEM((block, 2 * hd), BF16),
                        *[pltpu.VMEM((block // FOX_QUERY_PART, block, FOX_QUERY_PART), F32)] * 2,
                        *[pltpu.VMEM((SUBLANES, FOX_QUERY_PART), F32)] * (block // FOX_QUERY_PART),
                        *[pltpu.VMEM((hd + FOX_SUM_ROWS, FOX_QUERY_PART), F32)] * (block // FOX_QUERY_PART)],
        compiler_params=pltpu.CompilerParams(dimension_semantics=("arbitrary", "arbitrary"),
                                             vmem_limit_bytes=FOX_VMEM_LIMIT),
        interpret=interpret, name="fox_attention",
    )(pq, kv, kv, f_cum, f_cum, q_gain.reshape(1, hd), k_gain.reshape(1, hd))


def kernel(x, c, ada_w, ada_b, mix_norm, ffn_norm, a_w_in, a_b_i, a_b_f, a_h_norm, a_w_out,
           kv_ada_w, kv_ada_b, kv_norm, kv_w, kv_b_f, kv_k_norm, b_w_qo, b_q_norm, b_w_out,
           peer_wq, peer_subkeys, peer_u, peer_v):
    B, S, D = x.shape
    n = B * S
    cs = jax.nn.silu(c)
    x2 = x.reshape(n, D)
    kv = f_cum = None
    for l in range(DEPTH):
        mod = cs @ ada_w[l] + ada_b[l]
        sh1, sc1, g1, sh2, sc2, g2 = jnp.split(mod, N_ADA, axis=-1)
        if l < N_A_LAYERS:
            split = 2 * MLSTM_QK_W + MLSTM_V_W + D_MODEL
            p, gates = _norm_proj(x2, sh1, sc1, mix_norm[l], a_w_in[l][:, :split], a_w_in[l][:, split:], S)
            h = _mlstm_scan(p, gates, a_b_i[l], a_b_f[l], B, S)
            x2 = _out_proj(h, p, 2, x2, g1, a_h_norm[l], a_w_out[l], S, MLSTM_V_DIM)
        else:
            j = l - N_A_LAYERS
            pq, _ = _norm_proj(x2, sh1, sc1, mix_norm[l], b_w_qo[j], None, S)
            att = _fox_attention(pq, kv, f_cum, b_q_norm[j], kv_k_norm, B, S)
            x2 = _out_proj(att, pq, 1, x2, g1, None, b_w_out[j], S, None)
        x2 = _peer(x2.reshape(B, S, D), ffn_norm[l], sh2, sc2, g2, peer_wq[l], peer_subkeys[l],
                   _pack_down_table(peer_u[l]), _pack_expert_table(peer_v[l])).reshape(n, D)
        if l == N_A_LAYERS - 1:
            sh, sc = jnp.split(cs @ kv_ada_w + kv_ada_b, 2, axis=-1)
            kv, fg = _norm_proj(x2, sh, sc, kv_norm, kv_w[:, :2 * D], kv_w[:, 2 * D:], S)
            log_f = jax.nn.log_sigmoid(fg[:, :FOX_HEADS] + kv_b_f)
            f_cum = jnp.cumsum(log_f.reshape(B, S, FOX_HEADS), axis=1).reshape(n, FOX_HEADS)
    return x2.reshape(B, S, D)
```

```python
import functools
import math

import jax
import jax.numpy as jnp
from jax import lax
from jax.experimental import pallas as pl
from jax.experimental.pallas import tpu as pltpu

F32 = jnp.float32
BF16 = jnp.bfloat16

D_MODEL = 1024
DEPTH = 2
N_A_LAYERS = DEPTH // 2
EPS = 1e-6
N_ADA = 6

MLSTM_HEADS = 4
MLSTM_QK_DIM = D_MODEL // (2 * MLSTM_HEADS)
MLSTM_V_DIM = D_MODEL // MLSTM_HEADS
MLSTM_CHUNK = 64
GATE_SOFTCAP = 15.0
MLSTM_QK_W = MLSTM_HEADS * MLSTM_QK_DIM
MLSTM_V_W = MLSTM_HEADS * MLSTM_V_DIM

FOX_HEADS = 8
FOX_HEAD_DIM = D_MODEL // FOX_HEADS

PEER_HEADS = 8
PEER_KEYS = 128
PEER_EXPERTS = PEER_KEYS * PEER_KEYS
PEER_QUERY_DIM = 256
PEER_HALF = PEER_QUERY_DIM // 2
PEER_TOPK = 16

SUBLANES = 8
LANES = 128
ROW_WORDS = SUBLANES * LANES
assert ROW_WORDS == D_MODEL
PEER_SLOTS = PEER_HEADS * PEER_TOPK
HALF_EXPERTS = PEER_EXPERTS // 2
SUBROWS = 2 * SUBLANES
GATHER_ROWS = PEER_SLOTS * SUBROWS
DOWN_SLAB_ROWS = SUBLANES // 2
DOWN_ROWS = PEER_SLOTS * SUBLANES
TOKEN_GROUP = 2 * SUBLANES
UP_TOKEN_GROUP = 4 * SUBLANES
PEER_TOKEN_TILE = 128
PEER_VMEM_LIMIT = 48 * 1024 * 1024


def _pack_expert_table(t):
    lo = lax.bitcast_convert_type(t[:HALF_EXPERTS].astype(BF16), jnp.uint16).astype(jnp.uint32)
    hi = lax.bitcast_convert_type(t[HALF_EXPERTS:].astype(BF16), jnp.uint16).astype(jnp.uint32)
    w = lo | (hi << 16)
    return lax.bitcast_convert_type(w, jnp.int32).reshape(HALF_EXPERTS, SUBLANES, LANES)


def _pack_down_table(t):
    bits = lax.bitcast_convert_type(t.astype(BF16), jnp.uint16).astype(jnp.uint32)
    bits = bits.reshape(t.shape[0], 2, DOWN_SLAB_ROWS, LANES)
    w = bits[:, 0] | (bits[:, 1] << 16)
    return lax.bitcast_convert_type(w, jnp.int32).reshape(t.shape[0] * DOWN_SLAB_ROWS, LANES)


def _slot_expand_matrix():
    slot = lax.broadcasted_iota(jnp.int32, (PEER_SLOTS, GATHER_ROWS), 0)
    sub = lax.broadcasted_iota(jnp.int32, (PEER_SLOTS, GATHER_ROWS), 1)
    return (sub // SUBROWS == slot).astype(BF16)


def _diag_mask():
    r = lax.broadcasted_iota(jnp.int32, (SUBLANES, GATHER_ROWS), 0)
    sub = lax.broadcasted_iota(jnp.int32, (SUBLANES, GATHER_ROWS), 1)
    return (sub % SUBROWS) // 2 == r


def _down_diag_mask():
    j = lax.broadcasted_iota(jnp.int32, (SUBLANES, DOWN_ROWS), 0)
    sub = lax.broadcasted_iota(jnp.int32, (SUBLANES, DOWN_ROWS), 1)
    return sub % SUBLANES == j


def _down_select_matrix():
    sub = lax.broadcasted_iota(jnp.int32, (DOWN_ROWS, PEER_SLOTS), 0)
    pos = lax.broadcasted_iota(jnp.int32, (DOWN_ROWS, PEER_SLOTS), 1)
    return (sub // SUBLANES == pos).astype(F32)


def _half_mask(hs, e):
    hsx = jnp.dot(hs.astype(BF16), e, preferred_element_type=F32)
    par = (lax.broadcasted_iota(jnp.int32, hsx.shape, 1) % 2).astype(F32)
    return hsx == par


def _gather_token(idx_ref, tab_ref, gb_ref, t):
    for s in range(PEER_SLOTS // 2):
        w = idx_ref[t, s]
        a = pl.multiple_of(w & 0xFFFF, SUBLANES)
        b = pl.multiple_of(lax.shift_right_logical(w, 16), SUBLANES)
        gb_ref[pl.ds(2 * s * SUBLANES, SUBLANES), :] = tab_ref[pl.ds(a, SUBLANES), :]
        gb_ref[pl.ds((2 * s + 1) * SUBLANES, SUBLANES), :] = tab_ref[pl.ds(b, SUBLANES), :]
    return pltpu.bitcast(gb_ref[...], BF16)


def _gather_token_down(idx_ref, tab_ref, gb_ref, t):
    for s in range(PEER_SLOTS // 2):
        w = idx_ref[t, s]
        a = pl.multiple_of(w & 0xFFFF, DOWN_SLAB_ROWS)
        b = pl.multiple_of(lax.shift_right_logical(w, 16), DOWN_SLAB_ROWS)
        gb_ref[pl.ds(s * SUBLANES, SUBLANES), :] = jnp.concatenate(
            [tab_ref[pl.ds(a, DOWN_SLAB_ROWS), :], tab_ref[pl.ds(b, DOWN_SLAB_ROWS), :]], axis=0)
    return pltpu.bitcast(gb_ref[...], BF16)


def _split_hi_lo(a):
    hi = a.astype(BF16)
    lo = (a - hi.astype(F32)).astype(BF16)
    return hi, lo


def _down_tokens(g, first, last, idx_ref, xs_ref, tab_ref, gb_ref):
    diag = _down_diag_mask()
    rows = []
    for i in range(first, last):
        t = g * TOKEN_GROUP + i
        gath = _gather_token_down(idx_ref, tab_ref, gb_ref, t)
        y = lax.dot_general(xs_ref[t], gath, (((1,), (1,)), ((), ())),
                            preferred_element_type=F32)
        y8 = y[:SUBLANES] + y[SUBLANES:]
        rows.append(jnp.sum(jnp.where(diag, y8, 0.0), axis=0, keepdims=True))
    return rows


def _store_z(g, rows, z_ref):
    z_ref[pl.ds(pl.multiple_of(g * TOKEN_GROUP, TOKEN_GROUP), TOKEN_GROUP), :] = jnp.concatenate(rows, axis=0)


def _down_epilogue(z_ref, sel_ref, gate_ref, w_ref):
    act = jnp.dot(z_ref[...], sel_ref[...], precision=lax.Precision.HIGHEST, preferred_element_type=F32)
    gelu = 0.5 * act * (1.0 + lax.erf(act * (1.0 / math.sqrt(2.0))))
    w_ref[...] = gate_ref[...] * gelu


def _peer_up_kernel(idx_ref, w_ref, hs_ref, x_ref, g_ref, tab_ref, e_ref, y_ref, gb_ref, ahi_ref, alo_ref):
    tt = w_ref.shape[0]
    diag = _diag_mask()
    e = e_ref[...]
    hm = _half_mask(hs_ref[...], e)
    w_hi, w_lo = _split_hi_lo(w_ref[...])
    ahi_ref[...] = jnp.where(hm, jnp.dot(w_hi, e, preferred_element_type=F32), 0.0)
    alo_ref[...] = jnp.where(hm, jnp.dot(w_lo, e, preferred_element_type=F32), 0.0)

    def group(g, carry):
        base = pl.multiple_of(g * UP_TOKEN_GROUP, UP_TOKEN_GROUP)
        a_hi = ahi_ref[pl.ds(base, UP_TOKEN_GROUP), :]
        a_lo = alo_ref[pl.ds(base, UP_TOKEN_GROUP), :]
        outs = []
        for i in range(UP_TOKEN_GROUP):
            t = g * UP_TOKEN_GROUP + i
            gath = _gather_token(idx_ref, tab_ref, gb_ref, t)
            lhs = jnp.concatenate(
                [jnp.where(diag, a_hi[i:i + 1, :], 0.0), jnp.where(diag, a_lo[i:i + 1, :], 0.0)],
                axis=0).astype(BF16)
            out = jnp.dot(lhs, gath, preferred_element_type=F32)
            outs.append(out[:SUBLANES] + out[SUBLANES:])
        rows = pl.ds(base, UP_TOKEN_GROUP)
        for r in range(SUBLANES):
            cols = slice(r * LANES, (r + 1) * LANES)
            chunk = jnp.concatenate([o[r:r + 1, :] for o in outs], axis=0)
            y_ref[rows, cols] = x_ref[rows, cols] + g_ref[0][:, cols] * chunk
        return carry

    lax.fori_loop(0, tt // UP_TOKEN_GROUP, group, 0)


def _peer_up(rows_up, w, hs, x2, g, tab_v, seq_len, *, token_tile=PEER_TOKEN_TILE, interpret=False):
    n = w.shape[0]
    tt = token_tile
    assert n % tt == 0 and tt % UP_TOKEN_GROUP == 0 and seq_len % tt == 0
    tiles_per_seq = seq_len // tt
    batch = n // seq_len
    tok = lambda i: (i, 0)
    slot_spec = pl.BlockSpec((tt, PEER_SLOTS), tok)
    row_spec = pl.BlockSpec((tt, D_MODEL), tok)
    return pl.pallas_call(
        _peer_up_kernel,
        grid=(n // tt,),
        in_specs=[pl.BlockSpec((tt, PEER_SLOTS // 2), tok, memory_space=pltpu.SMEM), slot_spec, slot_spec,
                  row_spec, pl.BlockSpec((1, 1, D_MODEL), lambda i: (i // tiles_per_seq, 0, 0)),
                  pl.BlockSpec(memory_space=pltpu.VMEM),
                  pl.BlockSpec((PEER_SLOTS, GATHER_ROWS), lambda i: (0, 0))],
        out_specs=row_spec,
        out_shape=jax.ShapeDtypeStruct((n, D_MODEL), F32),
        scratch_shapes=[pltpu.VMEM((PEER_SLOTS * SUBLANES, LANES), jnp.int32),
                        pltpu.VMEM((tt, GATHER_ROWS), F32),
                        pltpu.VMEM((tt, GATHER_ROWS), F32)],
        compiler_params=pltpu.CompilerParams(dimension_semantics=("arbitrary",),
                                             vmem_limit_bytes=PEER_VMEM_LIMIT),
        interpret=interpret,
        name="peer_up",
    )(rows_up, w, hs, x2, g.reshape(batch, 1, D_MODEL), tab_v.reshape(HALF_EXPERTS * SUBLANES, LANES),
      _slot_expand_matrix())


def _softcap(z):
    return GATE_SOFTCAP * jnp.tanh(z / GATE_SOFTCAP)


def _dot3(a_hi, a_lo, b_hi, b_lo, dims):
    dot = functools.partial(lax.dot_general, dimension_numbers=(dims, ((), ())), preferred_element_type=F32)
    return dot(a_hi, b_hi) + (dot(a_hi, b_lo) + dot(a_lo, b_hi))


def _top_rows(s, k):
    nrows = s.shape[0]
    row = lax.broadcasted_iota(jnp.int32, s.shape, 0)
    vals, rows = [], []
    for _ in range(k):
        m = jnp.max(s, axis=0, keepdims=True)
        r = jnp.min(jnp.where(s == m, row, nrows), axis=0, keepdims=True)
        vals.append(m)
        rows.append(r)
        s = jnp.where(row == r, -jnp.inf, s)
    return jnp.concatenate(vals, axis=0), jnp.concatenate(rows, axis=0)


def _tree(op, xs):
    while len(xs) > 1:
        xs = [op(xs[i], xs[i + 1]) if i + 1 < len(xs) else xs[i] for i in range(0, len(xs), 2)]
    return xs[0]


def _product_candidates(s0, s1, i0, i1):
    k = PEER_TOPK
    t = s0.shape[1]
    sub = lax.broadcasted_iota(jnp.int32, (SUBLANES, t), 0)
    blocks = []
    for half in range(k // SUBLANES):
        b = sub + half * SUBLANES
        lo = half * SUBLANES
        blocks.append((s0[0:1] + s1[lo:lo + SUBLANES], b, i0[0:1] * PEER_KEYS + i1[lo:lo + SUBLANES]))
    for a in range(1, SUBLANES):
        valid = (a + 1) * (sub + 1) <= k
        blocks.append((jnp.where(valid, s0[a:a + 1] + s1[0:SUBLANES], -jnp.inf), a * k + sub,
                       i0[a:a + 1] * PEER_KEYS + i1[0:SUBLANES]))
    a = sub + SUBLANES
    blocks.append((s0[SUBLANES:k] + s1[0:1], a * k, i0[SUBLANES:k] * PEER_KEYS + i1[0:1]))
    return blocks


def _route_prologue(x_ref, shift_ref, scale_ref, gain_ref, xs_ref, xh_ref, xl_ref):
    x = x_ref[...]
    xn = x * lax.rsqrt(jnp.mean(x * x, axis=-1, keepdims=True) + EPS) * gain_ref[...]
    xn = xn * (1.0 + scale_ref[0]) + shift_ref[0]
    x_hi, x_lo = _split_hi_lo(xn)
    for j in range(SUBLANES):
        c = (j % 2) * DOWN_SLAB_ROWS + j // 2
        xs_ref[:, j * LANES:(j + 1) * LANES] = x_hi[:, c * LANES:(c + 1) * LANES]
        xs_ref[:, D_MODEL + j * LANES:D_MODEL + (j + 1) * LANES] = x_lo[:, c * LANES:(c + 1) * LANES]
    xh_ref[...] = x_hi
    xl_ref[...] = x_lo


def _route_query(h, xh_ref, xl_ref, wqh_ref, wql_ref, q_ref):
    q_ref[...] = _dot3(xh_ref[...], xl_ref[...], wqh_ref[h], wql_ref[h], ((1,), (0,)))


def _route_sub_keys(p, skh_ref, skl_ref, q_ref, tv_ref, ti_ref):
    q_hi, q_lo = _split_hi_lo(q_ref[:, p * PEER_HALF:(p + 1) * PEER_HALF])
    s = _dot3(skh_ref[p], skl_ref[p], q_hi, q_lo, ((1,), (1,)))
    tv_ref[p], ti_ref[p] = _top_rows(s, PEER_TOPK)


def _route_head(h, tv_ref, ti_ref, bs_ref, be_ref):
    k = PEER_TOPK
    blocks = _product_candidates(tv_ref[0], tv_ref[1], ti_ref[0], ti_ref[1])
    sums = [b[0] for b in blocks]
    best_s, best_e = [], []
    for _ in range(k):
        m = jnp.max(_tree(jnp.maximum, sums), axis=0, keepdims=True)
        pos = jnp.min(_tree(jnp.minimum, [jnp.where(c == m, b[1], k * k) for c, b in zip(sums, blocks)]),
                      axis=0, keepdims=True)
        hit = [b[1] == pos for b in blocks]
        e = jnp.max(_tree(jnp.maximum, [jnp.where(hh, b[2], -1) for hh, b in zip(hit, blocks)]),
                    axis=0, keepdims=True)
        sums = [jnp.where(hh, -jnp.inf, c) for hh, c in zip(hit, sums)]
        best_s.append(m)
        best_e.append(e)
    bs = jnp.concatenate(best_s, axis=0)
    ex = jnp.exp(bs - bs[0:1])
    off = pl.multiple_of(h * k, k)
    bs_ref[pl.ds(off, k), :] = ex / jnp.sum(ex, axis=0, keepdims=True)
    be_ref[pl.ds(off, k), :] = jnp.concatenate(best_e, axis=0).astype(F32)


def _route_epilogue(bs_ref, be_ref, pos_ref, rows_down_ref, rows_up_ref, hs_ref, gate_ref):
    tt = gate_ref.shape[0]
    half = PEER_SLOTS // 2

    def gather_order(a):
        for c in range(tt // LANES):
            cols = slice(c * LANES, (c + 1) * LANES)
            tile_ref = pos_ref.at[c]
            tile_ref[pl.ds(0, half, stride=2), :] = a[:half, cols]
            tile_ref[pl.ds(1, half, stride=2), :] = a[half:, cols]
        return jnp.concatenate([pos_ref[c].T for c in range(tt // LANES)], axis=0)

    ids = be_ref[...]
    gate_ref[...] = gather_order(bs_ref[...])
    hs_ref[...] = (gather_order(ids).astype(jnp.int32) // HALF_EXPERTS).astype(F32)
    lo = ids.T[:, :half].astype(jnp.int32)
    hi = jnp.concatenate([ids[half:], ids[half:]], axis=0).T[:, :half].astype(jnp.int32)
    up_off = lambda e: (e % HALF_EXPERTS) * SUBLANES
    rows_up_ref[...] = up_off(lo) | (up_off(hi) << 16)
    rows_down_ref[...] = (lo * DOWN_SLAB_ROWS) | ((hi * DOWN_SLAB_ROWS) << 16)


N_ROUTE_IN, N_DOWN_IN, N_ROUTE_OUT, N_ROUTE_SCRATCH = 8, 5, 5, 8


def _route_down_kernel(*refs, do_route, do_down):
    refs = list(refs)
    take = lambda count: [refs.pop(0) for _ in range(count)]
    route_in = take(N_ROUTE_IN) if do_route else None
    down_in = take(N_DOWN_IN) if do_down else None
    route_out = take(N_ROUTE_OUT) if do_route else None
    w_ref = take(1)[0] if do_down else None
    route_scratch = take(N_ROUTE_SCRATCH) if do_route else None
    down_scratch = take(2) if do_down else None

    if do_route:
        x_ref, shift_ref, scale_ref, gain_ref, wqh_ref, wql_ref, skh_ref, skl_ref = route_in
        rows_down_ref, rows_up_ref, hs_ref, gate_ref, xs_ref = route_out
        xh_ref, xl_ref, q_ref, tv_ref, ti_ref, bs_ref, be_ref, pos_ref = route_scratch
        _route_prologue(x_ref, shift_ref, scale_ref, gain_ref, xs_ref, xh_ref, xl_ref)
    if do_down:
        idx_ref, xs_in_ref, gate_in_ref, tab_ref, sel_ref = down_in
        gb_ref, z_ref = down_scratch

    quarter = TOKEN_GROUP // 4

    def step(h, carry):
        rows = []
        if do_route:
            _route_query(h, xh_ref, xl_ref, wqh_ref, wql_ref, q_ref)
        for part in range(4):
            if do_down:
                rows += _down_tokens(h, part * quarter, (part + 1) * quarter, idx_ref, xs_in_ref, tab_ref, gb_ref)
            if do_route and part < 2:
                _route_sub_keys(part, skh_ref, skl_ref, q_ref, tv_ref, ti_ref)
            if do_route and part == 2:
                _route_head(h, tv_ref, ti_ref, bs_ref, be_ref)
        if do_down:
            _store_z(h, rows, z_ref)
        return carry

    lax.fori_loop(0, PEER_HEADS, step, 0)
    if do_route:
        _route_epilogue(bs_ref, be_ref, pos_ref, rows_down_ref, rows_up_ref, hs_ref, gate_ref)
    if do_down:
        _down_epilogue(z_ref, sel_ref, gate_in_ref, w_ref)


ROUTE_DOWN_VMEM_LIMIT = 56 * 1024 * 1024
PEER_CHUNKS_PER_SEQ = 2


def _route_down(route_args, down_args, seq_len, *, interpret=False):
    tt = PEER_TOKEN_TILE
    assert tt // TOKEN_GROUP == PEER_HEADS and seq_len % tt == 0
    tiles = seq_len // tt
    tok = lambda i: (i, 0)
    whole = pl.BlockSpec(memory_space=pltpu.VMEM)
    slot_spec = pl.BlockSpec((tt, PEER_SLOTS), tok)
    pair_spec = pl.BlockSpec((tt, PEER_SLOTS // 2), tok)
    slot_shape = jax.ShapeDtypeStruct((seq_len, PEER_SLOTS), F32)
    pair_shape = jax.ShapeDtypeStruct((seq_len, PEER_SLOTS // 2), jnp.int32)
    in_specs, args, out_specs, out_shape, scratch = [], [], [], [], []
    if route_args is not None:
        x2, chunk, seq, shift, scale, gain, wq_hi, wq_lo, sk_hi, sk_lo = route_args
        batch = shift.shape[0]
        per_seq = pl.BlockSpec((1, 1, D_MODEL), lambda i: (seq, 0, 0))
        in_specs += [pl.BlockSpec((tt, D_MODEL), lambda i: (chunk * tiles + i, 0)), per_seq, per_seq,
                     pl.BlockSpec((1, D_MODEL), lambda i: (0, 0)), whole, whole, whole, whole]
        args += [x2, shift.reshape(batch, 1, D_MODEL), scale.reshape(batch, 1, D_MODEL),
                 gain.reshape(1, D_MODEL), wq_hi, wq_lo, sk_hi, sk_lo]
    if down_args is not None:
        rows_down, xs, gate, tab_u = down_args
        in_specs += [pl.BlockSpec((tt, PEER_SLOTS // 2), tok, memory_space=pltpu.SMEM),
                     pl.BlockSpec((tt, SUBROWS, LANES), lambda i: (i, 0, 0)), slot_spec, whole,
                     pl.BlockSpec((DOWN_ROWS, PEER_SLOTS), lambda i: (0, 0))]
        args += [rows_down, xs.reshape(seq_len, SUBROWS, LANES), gate, tab_u, _down_select_matrix()]
    if route_args is not None:
        out_specs += [pair_spec, pair_spec, slot_spec, slot_spec, pl.BlockSpec((tt, 2 * D_MODEL), tok)]
        out_shape += [pair_shape, pair_shape, slot_shape, slot_shape,
                      jax.ShapeDtypeStruct((seq_len, 2 * D_MODEL), BF16)]
    if down_args is not None:
        out_specs.append(slot_spec)
        out_shape.append(slot_shape)
    if route_args is not None:
        scratch += [pltpu.VMEM((tt, D_MODEL), BF16),
                    pltpu.VMEM((tt, D_MODEL), BF16),
                    pltpu.VMEM((tt, PEER_QUERY_DIM), F32),
                    pltpu.VMEM((2, PEER_TOPK, tt), F32),
                    pltpu.VMEM((2, PEER_TOPK, tt), jnp.int32),
                    pltpu.VMEM((PEER_SLOTS, tt), F32),
                    pltpu.VMEM((PEER_SLOTS, tt), F32),
                    pltpu.VMEM((tt // LANES, PEER_SLOTS, LANES), F32)]
    if down_args is not None:
        scratch += [pltpu.VMEM((PEER_SLOTS // 2 * SUBLANES, LANES), jnp.int32),
                    pltpu.VMEM((tt, DOWN_ROWS), F32)]
    outs = pl.pallas_call(
        functools.partial(_route_down_kernel, do_route=route_args is not None, do_down=down_args is not None),
        grid=(tiles,), in_specs=in_specs, out_specs=out_specs, out_shape=out_shape, scratch_shapes=scratch,
        compiler_params=pltpu.CompilerParams(dimension_semantics=("arbitrary",),
                                             vmem_limit_bytes=ROUTE_DOWN_VMEM_LIMIT),
        interpret=interpret, name="peer_route_down",
    )(*args)
    routed = tuple(outs[:N_ROUTE_OUT]) if route_args is not None else None
    w = outs[-1] if down_args is not None else None
    return routed, w


def _peer(x, gain, shift, scale, res_gate, wq, subkeys, tab_u, tab_v):
    B, S, D = x.shape
    x2 = x.reshape(B * S, D)
    per_head = lambda w: w.reshape(D, PEER_HEADS, PEER_QUERY_DIM).transpose(1, 0, 2)
    wq_hi, wq_lo = _split_hi_lo(per_head(wq))
    sk_hi, sk_lo = _split_hi_lo(subkeys)
    routed, rows_up, hs, ws = None, [], [], []
    chunk_len = S // PEER_CHUNKS_PER_SEQ
    n_chunks = B * PEER_CHUNKS_PER_SEQ
    for c in range(n_chunks + 1):
        route_args = ((x2, c, c // PEER_CHUNKS_PER_SEQ, shift, scale, gain, wq_hi, wq_lo, sk_hi, sk_lo)
                      if c < n_chunks else None)
        down_args = (routed[0], routed[4], routed[3], tab_u) if routed is not None else None
        routed, w = _route_down(route_args, down_args, chunk_len)
        if w is not None:
            ws.append(w)
        if routed is not None:
            rows_up.append(routed[1])
            hs.append(routed[2])
    y = _peer_up(jnp.concatenate(rows_up), jnp.concatenate(ws), jnp.concatenate(hs), x2, res_gate, tab_v, S)
    return y.reshape(B, S, D)


PROJ_TOKEN_TILE = 512
PROJ_VMEM_LIMIT = 48 * 1024 * 1024


def _ada_norm_tile(x, gain, shift, scale):
    xn = x * lax.rsqrt(jnp.mean(x * x, axis=-1, keepdims=True) + EPS) * gain
    return xn * (1.0 + scale) + shift


def _norm_proj_kernel(x_ref, shift_ref, scale_ref, gain_ref, w_ref, *rest, with_gate):
    x_hi, x_lo = _split_hi_lo(_ada_norm_tile(x_ref[...], gain_ref[...], shift_ref[0], scale_ref[0]))
    if with_gate:
        wgh_ref, wgl_ref, main_ref, gate_ref = rest
        gate_ref[...] = _dot3(x_hi, x_lo, wgh_ref[...], wgl_ref[...], ((1,), (0,)))
    else:
        (main_ref,) = rest
    main_ref[...] = jnp.dot(x_hi, w_ref[...], preferred_element_type=F32)


def _norm_proj(x2, shift, scale, gain, w_main, w_gate, seq_len, *, token_tile=PROJ_TOKEN_TILE, interpret=False):
    n = x2.shape[0]
    tt = token_tile
    assert seq_len % tt == 0 and n % seq_len == 0
    tiles_per_seq = seq_len // tt
    batch = n // seq_len
    m = w_main.shape[1]
    tok = lambda i: (i, 0)
    per_seq = pl.BlockSpec((1, 1, D_MODEL), lambda i: (i // tiles_per_seq, 0, 0))
    whole = pl.BlockSpec(memory_space=pltpu.VMEM)
    in_specs = [pl.BlockSpec((tt, D_MODEL), tok), per_seq, per_seq,
                pl.BlockSpec((1, D_MODEL), lambda i: (0, 0)), whole]
    args = [x2, shift.reshape(batch, 1, D_MODEL), scale.reshape(batch, 1, D_MODEL), gain.reshape(1, D_MODEL),
            w_main.astype(BF16)]
    out_specs = [pl.BlockSpec((tt, m), tok)]
    out_shape = [jax.ShapeDtypeStruct((n, m), F32)]
    if w_gate is not None:
        wg_hi, wg_lo = _split_hi_lo(jnp.pad(w_gate, ((0, 0), (0, LANES - w_gate.shape[1]))))
        in_specs += [whole, whole]
        args += [wg_hi, wg_lo]
        out_specs.append(pl.BlockSpec((tt, LANES), tok))
        out_shape.append(jax.ShapeDtypeStruct((n, LANES), F32))
    outs = pl.pallas_call(
        functools.partial(_norm_proj_kernel, with_gate=w_gate is not None),
        grid=(n // tt,), in_specs=in_specs, out_specs=out_specs, out_shape=out_shape,
        compiler_params=pltpu.CompilerParams(dimension_semantics=("arbitrary",),
                                             vmem_limit_bytes=PROJ_VMEM_LIMIT),
        interpret=interpret, name="norm_proj",
    )(*args)
    return (outs[0], outs[1]) if w_gate is not None else (outs[0], None)


def _out_proj_kernel(h_ref, og_ref, x_ref, g_ref, hg_ref, w_ref, o_ref, *, head_dim):
    h = h_ref[...]
    if head_dim is not None:
        parts = []
        for j in range(D_MODEL // head_dim):
            hb = h[:, j * head_dim:(j + 1) * head_dim]
            parts.append(hb * lax.rsqrt(jnp.mean(hb * hb, axis=-1, keepdims=True) + EPS))
        h = jnp.concatenate(parts, axis=1) * hg_ref[...]
    a = jax.nn.sigmoid(og_ref[...]) * h
    y = jnp.dot(a.astype(BF16), w_ref[...], preferred_element_type=F32)
    o_ref[...] = x_ref[...] + g_ref[0] * y


def _out_proj(h2, p, og_block, x2, g, h_gain, w_out, seq_len, head_dim, *, token_tile=PROJ_TOKEN_TILE,
              interpret=False):
    n = x2.shape[0]
    tt = token_tile
    tiles_per_seq = seq_len // tt
    batch = n // seq_len
    tok = lambda i: (i, 0)
    row = pl.BlockSpec((tt, D_MODEL), tok)
    gain = jnp.ones((1, D_MODEL), F32) if h_gain is None else h_gain.reshape(1, D_MODEL)
    return pl.pallas_call(
        functools.partial(_out_proj_kernel, head_dim=head_dim),
        grid=(n // tt,),
        in_specs=[row, pl.BlockSpec((tt, D_MODEL), lambda i: (i, og_block)), row,
                  pl.BlockSpec((1, 1, D_MODEL), lambda i: (i // tiles_per_seq, 0, 0)),
                  pl.BlockSpec((1, D_MODEL), lambda i: (0, 0)),
                  pl.BlockSpec(memory_space=pltpu.VMEM)],
        out_specs=row,
        out_shape=jax.ShapeDtypeStruct((n, D_MODEL), F32),
        compiler_params=pltpu.CompilerParams(dimension_semantics=("arbitrary",),
                                             vmem_limit_bytes=PROJ_VMEM_LIMIT),
        interpret=interpret, name="out_proj",
    )(h2, p, x2, g.reshape(batch, 1, D_MODEL), gain, w_out.astype(BF16))


MLSTM_STATE_W = MLSTM_V_DIM + LANES


def _mlstm_kernel(q_ref, k_ref, v_ref, gc_ref, gr_ref, h_ref, cn_ref, m_ref):
    H, DQK, DV, L = MLSTM_HEADS, MLSTM_QK_DIM, MLSTM_V_DIM, MLSTM_CHUNK

    @pl.when(pl.program_id(1) == 0)
    def _():
        cn_ref[...] = jnp.zeros_like(cn_ref)
        m_ref[...] = jnp.zeros_like(m_ref)

    tril = lax.broadcasted_iota(jnp.int32, (L, L), 1) <= lax.broadcasted_iota(jnp.int32, (L, L), 0)
    gc = gc_ref[...]
    gr = gr_ref[0]
    ones = jnp.ones((L, LANES), F32)
    for h in range(H):
        ig_col, b_col = gc[:, h:h + 1], gc[:, H + h:H + h + 1]
        ig_row, b_row = gr[h:h + 1, :], gr[H + h:H + h + 1, :]
        m_prev = m_ref[h:h + 1, 0:1]
        d_log = jnp.where(tril, b_col - b_row + ig_row, -jnp.inf)
        inter = b_col + m_prev
        m_t = jnp.maximum(inter, jnp.max(d_log, axis=1, keepdims=True))
        w = jnp.exp(d_log - m_t)
        a_inter = jnp.exp(inter - m_t)
        qh = (q_ref[:, h * DQK:(h + 1) * DQK] * (DQK ** -0.5)).astype(BF16)
        kf = k_ref[:, h * DQK:(h + 1) * DQK]
        vh = v_ref[:, h * DV:(h + 1) * DV]
        s = lax.dot_general(qh, kf.astype(BF16), (((1,), (1,)), ((), ())), preferred_element_type=F32) * w
        cn = cn_ref[h]
        qc = jnp.dot(qh, cn.astype(BF16), preferred_element_type=F32)
        num = a_inter * qc[:, :DV] + jnp.dot(s.astype(BF16), vh.astype(BF16), preferred_element_type=F32)
        den = a_inter * qc[:, DV:DV + 1] + jnp.sum(s, axis=1, keepdims=True)
        h_ref[:, h * DV:(h + 1) * DV] = num / jnp.maximum(jnp.abs(den), jnp.exp(-m_t))
        b_last = b_col[L - 1:L, :]
        g_col = b_last - b_col + ig_col
        m_new = jnp.maximum(b_last + m_prev, jnp.max(g_col, axis=0, keepdims=True))
        w_s = jnp.exp(g_col - m_new)
        decay = jnp.exp(b_last + m_prev - m_new)
        kw = (kf * w_s).astype(BF16)
        vaug = jnp.concatenate([vh, ones], axis=1).astype(BF16)
        cn_ref[h] = decay * cn + lax.dot_general(kw, vaug, (((0,), (0,)), ((), ())), preferred_element_type=F32)
        m_ref[h:h + 1, :] = jnp.broadcast_to(m_new, (1, LANES))


def _mlstm_scan(p, gates, b_i, b_f, batch, seq_len, *, interpret=False):
    H, L = MLSTM_HEADS, MLSTM_CHUNK
    n = p.shape[0]
    nc = seq_len // L
    ig = _softcap(gates[:, :H] + b_i)
    lf = jax.nn.log_sigmoid(_softcap(gates[:, H:2 * H] + b_f))
    b = jnp.cumsum(lf.reshape(n // L, L, H), axis=1).reshape(n, H)
    gc = jnp.concatenate([ig, b], axis=1)
    gr = gc.reshape(n // L, L, 2 * H).transpose(0, 2, 1)
    chunk = lambda bi, c: bi * nc + c
    return pl.pallas_call(
        _mlstm_kernel,
        grid=(batch, nc),
        in_specs=[pl.BlockSpec((L, MLSTM_QK_W), lambda bi, c: (chunk(bi, c), 0)),
                  pl.BlockSpec((L, MLSTM_QK_W), lambda bi, c: (chunk(bi, c), 1)),
                  pl.BlockSpec((L, MLSTM_V_W), lambda bi, c: (chunk(bi, c), 1)),
                  pl.BlockSpec((L, 2 * H), lambda bi, c: (chunk(bi, c), 0)),
                  pl.BlockSpec((1, 2 * H, L), lambda bi, c: (chunk(bi, c), 0, 0))],
        out_specs=pl.BlockSpec((L, D_MODEL), lambda bi, c: (chunk(bi, c), 0)),
        out_shape=jax.ShapeDtypeStruct((n, D_MODEL), F32),
        scratch_shapes=[pltpu.VMEM((H, MLSTM_QK_DIM, MLSTM_STATE_W), F32),
                        pltpu.VMEM((SUBLANES, LANES), F32)],
        compiler_params=pltpu.CompilerParams(dimension_semantics=("arbitrary", "arbitrary")),
        interpret=interpret, name="mlstm_scan",
    )(p, p, p, gc, gr)


FOX_BLOCK = 1024
FOX_QUERY_PART = 256
FOX_SUM_ROWS = 16
FOX_VMEM_LIMIT = 48 * 1024 * 1024


def _bias_columns(f, query_side):
    hi = f.astype(BF16).astype(F32)
    r1 = f - hi
    lo = r1.astype(BF16).astype(F32)
    lo2 = r1 - lo
    lane = lax.broadcasted_iota(jnp.int32, (f.shape[0], LANES), 1)
    if query_side:
        vals = jnp.where(lane == 0, hi, jnp.where(lane == 1, lo, jnp.where(lane == 2, lo2,
                         jnp.where(lane < 6, 1.0, 0.0))))
    else:
        vals = jnp.where(lane < 3, 1.0, jnp.where(lane == 3, -hi, jnp.where(lane == 4, -lo,
                         jnp.where(lane == 5, -lo2, 0.0))))
    return vals.astype(BF16)


def _head_column(fc, h):
    lane = lax.broadcasted_iota(jnp.int32, fc.shape, 1)
    return jnp.sum(jnp.where(lane == h, fc, 0.0), axis=1, keepdims=True)


def _rms_rows(t, gain):
    return t * lax.rsqrt(jnp.mean(t * t, axis=-1, keepdims=True) + EPS) * gain


def _fox_attn_kernel(q_ref, k_ref, v_ref, fq_ref, fk_ref, qg_ref, kg_ref, o_ref,
                     ka_ref, vt_ref, qa_ref, st_cur_ref, st_next_ref, *state_refs):
    i = pl.program_id(1)
    h = pl.program_id(0) % FOX_HEADS
    tq = q_ref.shape[0]
    nk = k_ref.shape[0] // tq
    hd = FOX_HEAD_DIM
    part = FOX_QUERY_PART
    n_parts = tq // part
    m_refs, acc_refs = state_refs[:n_parts], state_refs[n_parts:]

    @pl.when(i == 0)
    def _():
        def prep(c, carry):
            r = pl.ds(pl.multiple_of(c * tq, tq), tq)
            ka_ref[r, :hd] = _rms_rows(k_ref[r, :], kg_ref[...]).astype(BF16)
            ka_ref[r, hd:] = _bias_columns(_head_column(fk_ref[r, :], h), False)
            vt_ref[c, :hd, :] = v_ref[r, :].T.astype(BF16)
            vt_ref[c, hd:, :] = jnp.ones((FOX_SUM_ROWS, tq), BF16)
            return carry
        lax.fori_loop(0, nk, prep, 0)

    qn = _rms_rows(q_ref[...], qg_ref[...]) * (hd ** -0.5)
    qa_ref[:, :hd] = qn.astype(BF16)
    qa_ref[:, hd:] = _bias_columns(_head_column(fq_ref[...], h), True)
    for m_ref, acc_ref in zip(m_refs, acc_refs):
        m_ref[...] = jnp.full_like(m_ref, -jnp.inf)
        acc_ref[...] = jnp.zeros_like(acc_ref)

    def scores(j, st_ref):
        kblk = ka_ref[pl.ds(pl.multiple_of(j * tq, tq), tq), :]
        for c in range(n_parts):
            st_ref[c] = lax.dot_general(kblk, qa_ref[c * part:(c + 1) * part, :],
                                        (((1,), (1,)), ((), ())), preferred_element_type=F32)

    def softmax_pv(j, masked):
        vtblk = vt_ref[j]
        pts, alphas = [], []
        for c, m_ref in enumerate(m_refs):
            st = st_cur_ref[c]
            if masked:
                key = lax.broadcasted_iota(jnp.int32, st.shape, 0)
                qry = lax.broadcasted_iota(jnp.int32, st.shape, 1) + c * part
                st = jnp.where(key <= qry, st, -jnp.inf)
            m_prev = m_ref[0:1, :]
            m_new = jnp.maximum(m_prev, jnp.max(st, axis=0, keepdims=True))
            alphas.append(jnp.exp(m_prev - m_new))
            pts.append(jnp.exp(st - m_new).astype(BF16))
            m_ref[0:1, :] = m_new
        for pt, alpha, acc_ref in zip(pts, alphas, acc_refs):
            acc_ref[...] = alpha * acc_ref[...] + jnp.dot(vtblk, pt, preferred_element_type=F32)

    def body(j, carry):
        scores(j + 1, st_next_ref)
        softmax_pv(j, False)
        st_cur_ref[...] = st_next_ref[...]
        return carry

    scores(0, st_cur_ref)
    lax.fori_loop(0, i, body, 0)
    softmax_pv(i, True)
    for c, acc_ref in enumerate(acc_refs):
        acc = acc_ref[...]
        o_ref[c * part:(c + 1) * part, :] = (acc[:hd] / acc[hd:hd + 1]).T


def _fox_attention(pq, kv, f_cum, q_gain, k_gain, batch, seq_len, *, block=FOX_BLOCK, interpret=False):
    n = pq.shape[0]
    hd, nh = FOX_HEAD_DIM, FOX_HEADS
    nq = seq_len // block
    return pl.pallas_call(
        _fox_attn_kernel,
        grid=(batch * nh, nq),
        in_specs=[pl.BlockSpec((block, hd), lambda bh, i: ((bh // nh) * nq + i, bh % nh)),
                  pl.BlockSpec((seq_len, hd), lambda bh, i: (bh // nh, bh % nh)),
                  pl.BlockSpec((seq_len, hd), lambda bh, i: (bh // nh, nh + bh % nh)),
                  pl.BlockSpec((block, nh), lambda bh, i: ((bh // nh) * nq + i, 0)),
                  pl.BlockSpec((seq_len, nh), lambda bh, i: (bh // nh, 0)),
                  pl.BlockSpec((1, hd), lambda bh, i: (0, 0)),
                  pl.BlockSpec((1, hd), lambda bh, i: (0, 0))],
        out_specs=pl.BlockSpec((block, hd), lambda bh, i: ((bh // nh) * nq + i, bh % nh)),
        out_shape=jax.ShapeDtypeStruct((n, D_MODEL), F32),
        scratch_shapes=[pltpu.VMEM((seq_len, 2 * hd), BF16),
                        pltpu.VMEM((nq, hd + FOX_SUM_ROWS, block), BF16),
                        pltpu.VMEM((block, 2 * hd), BF16),
                        *[pltpu.VMEM((block // FOX_QUERY_PART, block, FOX_QUERY_PART), F32)] * 2,
                        *[pltpu.VMEM((SUBLANES, FOX_QUERY_PART), F32)] * (block // FOX_QUERY_PART),
                        *[pltpu.VMEM((hd + FOX_SUM_ROWS, FOX_QUERY_PART), F32)] * (block // FOX_QUERY_PART)],
        compiler_params=pltpu.CompilerParams(dimension_semantics=("arbitrary", "arbitrary"),
                                             vmem_limit_bytes=FOX_VMEM_LIMIT),
        interpret=interpret, name="fox_attention",
    )(pq, kv, kv, f_cum, f_cum, q_gain.reshape(1, hd), k_gain.reshape(1, hd))


def kernel(x, c, ada_w, ada_b, mix_norm, ffn_norm, a_w_in, a_b_i, a_b_f, a_h_norm, a_w_out,
           kv_ada_w, kv_ada_b, kv_norm, kv_w, kv_b_f, kv_k_norm, b_w_qo, b_q_norm, b_w_out,
           peer_wq, peer_subkeys, peer_u, peer_v):
    B, S, D = x.shape
    n = B * S
    cs = jax.nn.silu(c)
    x2 = x.reshape(n, D)
    kv = f_cum = None
    for l in range(DEPTH):
        mod = cs @ ada_w[l] + ada_b[l]
        sh1, sc1, g1, sh2, sc2, g2 = jnp.split(mod, N_ADA, axis=-1)
        if l < N_A_LAYERS:
            split = 2 * MLSTM_QK_W + MLSTM_V_W + D_MODEL
            p, gates = _norm_proj(x2, sh1, sc1, mix_norm[l], a_w_in[l][:, :split], a_w_in[l][:, split:], S)
            h = _mlstm_scan(p, gates, a_b_i[l], a_b_f[l], B, S)
            x2 = _out_proj(h, p, 2, x2, g1, a_h_norm[l], a_w_out[l], S, MLSTM_V_DIM)
        else:
            j = l - N_A_LAYERS
            pq, _ = _norm_proj(x2, sh1, sc1, mix_norm[l], b_w_qo[j], None, S)
            att = _fox_attention(pq, kv, f_cum, b_q_norm[j], kv_k_norm, B, S)
            x2 = _out_proj(att, pq, 1, x2, g1, None, b_w_out[j], S, None)
        x2 = _peer(x2.reshape(B, S, D), ffn_norm[l], sh2, sc2, g2, peer_wq[l], peer_subkeys[l],
                   _pack_down_table(peer_u[l]), _pack_expert_table(peer_v[l])).reshape(n, D)
        if l == N_A_LAYERS - 1:
            sh, sc = jnp.split(cs @ kv_ada_w + kv_ada_b, 2, axis=-1)
            kv, fg = _norm_proj(x2, sh, sc, kv_norm, kv_w[:, :2 * D], kv_w[:, 2 * D:], S)
            log_f = jax.nn.log_sigmoid(fg[:, :FOX_HEADS] + kv_b_f)
            f_cum = jnp.cumsum(log_f.reshape(B, S, FOX_HEADS), axis=1).reshape(n, FOX_HEADS)
    return x2.reshape(B, S, D)
```

```python
import functools
import math

import jax
import jax.numpy as jnp
from jax import lax
from jax.experimental import pallas as pl
from jax.experimental.pallas import tpu as pltpu

F32 = jnp.float32
BF16 = jnp.bfloat16

D_MODEL = 1024
DEPTH = 2
N_A_LAYERS = DEPTH // 2
EPS = 1e-6
N_ADA = 6

MLSTM_HEADS = 4
MLSTM_QK_DIM = D_MODEL // (2 * MLSTM_HEADS)
MLSTM_V_DIM = D_MODEL // MLSTM_HEADS
MLSTM_CHUNK = 64
GATE_SOFTCAP = 15.0
MLSTM_QK_W = MLSTM_HEADS * MLSTM_QK_DIM
MLSTM_V_W = MLSTM_HEADS * MLSTM_V_DIM

FOX_HEADS = 8
FOX_HEAD_DIM = D_MODEL // FOX_HEADS

PEER_HEADS = 8
PEER_KEYS = 128
PEER_EXPERTS = PEER_KEYS * PEER_KEYS
PEER_QUERY_DIM = 256
PEER_HALF = PEER_QUERY_DIM // 2
PEER_TOPK = 16

SUBLANES = 8
LANES = 128
ROW_WORDS = SUBLANES * LANES
assert ROW_WORDS == D_MODEL
PEER_SLOTS = PEER_HEADS * PEER_TOPK
HALF_EXPERTS = PEER_EXPERTS // 2
SUBROWS = 2 * SUBLANES
GATHER_ROWS = PEER_SLOTS * SUBROWS
DOWN_SLAB_ROWS = SUBLANES // 2
DOWN_ROWS = PEER_SLOTS * SUBLANES
TOKEN_GROUP = 2 * SUBLANES
UP_TOKEN_GROUP = 4 * SUBLANES
PEER_TOKEN_TILE = 128
PEER_VMEM_LIMIT = 48 * 1024 * 1024


def _pack_expert_table(t):
    lo = lax.bitcast_convert_type(t[:HALF_EXPERTS].astype(BF16), jnp.uint16).astype(jnp.uint32)
    hi = lax.bitcast_convert_type(t[HALF_EXPERTS:].astype(BF16), jnp.uint16).astype(jnp.uint32)
    w = lo | (hi << 16)
    return lax.bitcast_convert_type(w, jnp.int32).reshape(HALF_EXPERTS, SUBLANES, LANES)


def _pack_down_table(t):
    bits = lax.bitcast_convert_type(t.astype(BF16), jnp.uint16).astype(jnp.uint32)
    bits = bits.reshape(t.shape[0], 2, DOWN_SLAB_ROWS, LANES)
    w = bits[:, 0] | (bits[:, 1] << 16)
    return lax.bitcast_convert_type(w, jnp.int32).reshape(t.shape[0] * DOWN_SLAB_ROWS, LANES)


def _slot_expand_matrix():
    slot = lax.broadcasted_iota(jnp.int32, (PEER_SLOTS, GATHER_ROWS), 0)
    sub = lax.broadcasted_iota(jnp.int32, (PEER_SLOTS, GATHER_ROWS), 1)
    return (sub // SUBROWS == slot).astype(BF16)


def _diag_mask():
    r = lax.broadcasted_iota(jnp.int32, (SUBLANES, GATHER_ROWS), 0)
    sub = lax.broadcasted_iota(jnp.int32, (SUBLANES, GATHER_ROWS), 1)
    return (sub % SUBROWS) // 2 == r


def _down_diag_mask():
    j = lax.broadcasted_iota(jnp.int32, (SUBLANES, DOWN_ROWS), 0)
    sub = lax.broadcasted_iota(jnp.int32, (SUBLANES, DOWN_ROWS), 1)
    return sub % SUBLANES == j


def _down_select_matrix():
    sub = lax.broadcasted_iota(jnp.int32, (DOWN_ROWS, PEER_SLOTS), 0)
    pos = lax.broadcasted_iota(jnp.int32, (DOWN_ROWS, PEER_SLOTS), 1)
    return (sub // SUBLANES == pos).astype(F32)


def _half_mask(hs, e):
    hsx = jnp.dot(hs.astype(BF16), e, preferred_element_type=F32)
    par = (lax.broadcasted_iota(jnp.int32, hsx.shape, 1) % 2).astype(F32)
    return hsx == par


def _gather_token(idx_ref, tab_ref, gb_ref, t):
    for s in range(PEER_SLOTS // 2):
        w = idx_ref[t, s]
        a = pl.multiple_of(w & 0xFFFF, SUBLANES)
        b = pl.multiple_of(lax.shift_right_logical(w, 16), SUBLANES)
        gb_ref[pl.ds(2 * s * SUBLANES, SUBLANES), :] = tab_ref[pl.ds(a, SUBLANES), :]
        gb_ref[pl.ds((2 * s + 1) * SUBLANES, SUBLANES), :] = tab_ref[pl.ds(b, SUBLANES), :]
    return pltpu.bitcast(gb_ref[...], BF16)


def _gather_token_down(idx_ref, tab_ref, gb_ref, t):
    for s in range(PEER_SLOTS // 2):
        w = idx_ref[t, s]
        a = pl.multiple_of(w & 0xFFFF, DOWN_SLAB_ROWS)
        b = pl.multiple_of(lax.shift_right_logical(w, 16), DOWN_SLAB_ROWS)
        gb_ref[pl.ds(s * SUBLANES, SUBLANES), :] = jnp.concatenate(
            [tab_ref[pl.ds(a, DOWN_SLAB_ROWS), :], tab_ref[pl.ds(b, DOWN_SLAB_ROWS), :]], axis=0)
    return pltpu.bitcast(gb_ref[...], BF16)


def _split_hi_lo(a):
    hi = a.astype(BF16)
    lo = (a - hi.astype(F32)).astype(BF16)
    return hi, lo


def _down_tokens(g, first, last, idx_ref, xs_ref, tab_ref, gb_ref):
    diag = _down_diag_mask()
    rows = []
    for i in range(first, last):
        t = g * TOKEN_GROUP + i
        gath = _gather_token_down(idx_ref, tab_ref, gb_ref, t)
        y = lax.dot_general(xs_ref[t], gath, (((1,), (1,)), ((), ())),
                            preferred_element_type=F32)
        y8 = y[:SUBLANES] + y[SUBLANES:]
        rows.append(jnp.sum(jnp.where(diag, y8, 0.0), axis=0, keepdims=True))
    return rows


def _store_z(g, rows, z_ref):
    z_ref[pl.ds(pl.multiple_of(g * TOKEN_GROUP, TOKEN_GROUP), TOKEN_GROUP), :] = jnp.concatenate(rows, axis=0)


def _down_epilogue(z_ref, sel_ref, gate_ref, w_ref):
    act = jnp.dot(z_ref[...], sel_ref[...], precision=lax.Precision.HIGHEST, preferred_element_type=F32)
    gelu = 0.5 * act * (1.0 + lax.erf(act * (1.0 / math.sqrt(2.0))))
    w_ref[...] = gate_ref[...] * gelu


def _peer_up_kernel(idx_ref, w_ref, hs_ref, x_ref, g_ref, tab_ref, e_ref, y_ref, gb_ref, ahi_ref, alo_ref):
    tt = w_ref.shape[0]
    diag = _diag_mask()
    e = e_ref[...]
    hm = _half_mask(hs_ref[...], e)
    w_hi, w_lo = _split_hi_lo(w_ref[...])
    ahi_ref[...] = jnp.where(hm, jnp.dot(w_hi, e, preferred_element_type=F32), 0.0)
    alo_ref[...] = jnp.where(hm, jnp.dot(w_lo, e, preferred_element_type=F32), 0.0)

    def group(g, carry):
        base = pl.multiple_of(g * UP_TOKEN_GROUP, UP_TOKEN_GROUP)
        a_hi = ahi_ref[pl.ds(base, UP_TOKEN_GROUP), :]
        a_lo = alo_ref[pl.ds(base, UP_TOKEN_GROUP), :]
        outs = []
        for i in range(UP_TOKEN_GROUP):
            t = g * UP_TOKEN_GROUP + i
            gath = _gather_token(idx_ref, tab_ref, gb_ref, t)
            lhs = jnp.concatenate(
                [jnp.where(diag, a_hi[i:i + 1, :], 0.0), jnp.where(diag, a_lo[i:i + 1, :], 0.0)],
                axis=0).astype(BF16)
            out = jnp.dot(lhs, gath, preferred_element_type=F32)
            outs.append(out[:SUBLANES] + out[SUBLANES:])
        rows = pl.ds(base, UP_TOKEN_GROUP)
        for r in range(SUBLANES):
            cols = slice(r * LANES, (r + 1) * LANES)
            chunk = jnp.concatenate([o[r:r + 1, :] for o in outs], axis=0)
            y_ref[rows, cols] = x_ref[rows, cols] + g_ref[0][:, cols] * chunk
        return carry

    lax.fori_loop(0, tt // UP_TOKEN_GROUP, group, 0)


def _peer_up(rows_up, w, hs, x2, g, tab_v, seq_len, *, token_tile=PEER_TOKEN_TILE, interpret=False):
    n = w.shape[0]
    tt = token_tile
    assert n % tt == 0 and tt % UP_TOKEN_GROUP == 0 and seq_len % tt == 0
    tiles_per_seq = seq_len // tt
    batch = n // seq_len
    tok = lambda i: (i, 0)
    slot_spec = pl.BlockSpec((tt, PEER_SLOTS), tok)
    row_spec = pl.BlockSpec((tt, D_MODEL), tok)
    return pl.pallas_call(
        _peer_up_kernel,
        grid=(n // tt,),
        in_specs=[pl.BlockSpec((tt, PEER_SLOTS // 2), tok, memory_space=pltpu.SMEM), slot_spec, slot_spec,
                  row_spec, pl.BlockSpec((1, 1, D_MODEL), lambda i: (i // tiles_per_seq, 0, 0)),
                  pl.BlockSpec(memory_space=pltpu.VMEM),
                  pl.BlockSpec((PEER_SLOTS, GATHER_ROWS), lambda i: (0, 0))],
        out_specs=row_spec,
        out_shape=jax.ShapeDtypeStruct((n, D_MODEL), F32),
        scratch_shapes=[pltpu.VMEM((PEER_SLOTS * SUBLANES, LANES), jnp.int32),
                        pltpu.VMEM((tt, GATHER_ROWS), F32),
                        pltpu.VMEM((tt, GATHER_ROWS), F32)],
        compiler_params=pltpu.CompilerParams(dimension_semantics=("arbitrary",),
                                             vmem_limit_bytes=PEER_VMEM_LIMIT),
        interpret=interpret,
        name="peer_up",
    )(rows_up, w, hs, x2, g.reshape(batch, 1, D_MODEL), tab_v.reshape(HALF_EXPERTS * SUBLANES, LANES),
      _slot_expand_matrix())


def _softcap(z):
    return GATE_SOFTCAP * jnp.tanh(z / GATE_SOFTCAP)


def _dot3(a_hi, a_lo, b_hi, b_lo, dims):
    dot = functools.partial(lax.dot_general, dimension_numbers=(dims, ((), ())), preferred_element_type=F32)
    return dot(a_hi, b_hi) + (dot(a_hi, b_lo) + dot(a_lo, b_hi))


def _top_rows(s, k):
    nrows = s.shape[0]
    row = lax.broadcasted_iota(jnp.int32, s.shape, 0)
    vals, rows = [], []
    for _ in range(k):
        m = jnp.max(s, axis=0, keepdims=True)
        r = jnp.min(jnp.where(s == m, row, nrows), axis=0, keepdims=True)
        vals.append(m)
        rows.append(r)
        s = jnp.where(row == r, -jnp.inf, s)
    return jnp.concatenate(vals, axis=0), jnp.concatenate(rows, axis=0)


def _tree(op, xs):
    while len(xs) > 1:
        xs = [op(xs[i], xs[i + 1]) if i + 1 < len(xs) else xs[i] for i in range(0, len(xs), 2)]
    return xs[0]


def _product_candidates(s0, s1, i0, i1):
    k = PEER_TOPK
    t = s0.shape[1]
    sub = lax.broadcasted_iota(jnp.int32, (SUBLANES, t), 0)
    blocks = []
    for half in range(k // SUBLANES):
        b = sub + half * SUBLANES
        lo = half * SUBLANES
        blocks.append((s0[0:1] + s1[lo:lo + SUBLANES], b, i0[0:1] * PEER_KEYS + i1[lo:lo + SUBLANES]))
    for a in range(1, SUBLANES):
        valid = (a + 1) * (sub + 1) <= k
        blocks.append((jnp.where(valid, s0[a:a + 1] + s1[0:SUBLANES], -jnp.inf), a * k + sub,
                       i0[a:a + 1] * PEER_KEYS + i1[0:SUBLANES]))
    a = sub + SUBLANES
    blocks.append((s0[SUBLANES:k] + s1[0:1], a * k, i0[SUBLANES:k] * PEER_KEYS + i1[0:1]))
    return blocks


def _route_prologue(x_ref, shift_ref, scale_ref, gain_ref, xs_ref, xh_ref, xl_ref):
    x = x_ref[...]
    xn = x * lax.rsqrt(jnp.mean(x * x, axis=-1, keepdims=True) + EPS) * gain_ref[...]
    xn = xn * (1.0 + scale_ref[0]) + shift_ref[0]
    x_hi, x_lo = _split_hi_lo(xn)
    for j in range(SUBLANES):
        c = (j % 2) * DOWN_SLAB_ROWS + j // 2
        xs_ref[:, j * LANES:(j + 1) * LANES] = x_hi[:, c * LANES:(c + 1) * LANES]
        xs_ref[:, D_MODEL + j * LANES:D_MODEL + (j + 1) * LANES] = x_lo[:, c * LANES:(c + 1) * LANES]
    xh_ref[...] = x_hi
    xl_ref[...] = x_lo


def _route_query(h, xh_ref, xl_ref, wqh_ref, wql_ref, q_ref):
    q_ref[...] = _dot3(xh_ref[...], xl_ref[...], wqh_ref[h], wql_ref[h], ((1,), (0,)))


def _route_sub_keys(p, skh_ref, skl_ref, q_ref, tv_ref, ti_ref):
    q_hi, q_lo = _split_hi_lo(q_ref[:, p * PEER_HALF:(p + 1) * PEER_HALF])
    s = _dot3(skh_ref[p], skl_ref[p], q_hi, q_lo, ((1,), (1,)))
    tv_ref[p], ti_ref[p] = _top_rows(s, PEER_TOPK)


def _route_head(h, tv_ref, ti_ref, bs_ref, be_ref):
    k = PEER_TOPK
    blocks = _product_candidates(tv_ref[0], tv_ref[1], ti_ref[0], ti_ref[1])
    sums = [b[0] for b in blocks]
    best_s, best_e = [], []
    for _ in range(k):
        m = jnp.max(_tree(jnp.maximum, sums), axis=0, keepdims=True)
        pos = jnp.min(_tree(jnp.minimum, [jnp.where(c == m, b[1], k * k) for c, b in zip(sums, blocks)]),
                      axis=0, keepdims=True)
        hit = [b[1] == pos for b in blocks]
        e = jnp.max(_tree(jnp.maximum, [jnp.where(hh, b[2], -1) for hh, b in zip(hit, blocks)]),
                    axis=0, keepdims=True)
        sums = [jnp.where(hh, -jnp.inf, c) for hh, c in zip(hit, sums)]
        best_s.append(m)
        best_e.append(e)
    bs = jnp.concatenate(best_s, axis=0)
    ex = jnp.exp(bs - bs[0:1])
    off = pl.multiple_of(h * k, k)
    bs_ref[pl.ds(off, k), :] = ex / jnp.sum(ex, axis=0, keepdims=True)
    be_ref[pl.ds(off, k), :] = jnp.concatenate(best_e, axis=0).astype(F32)


def _route_epilogue(bs_ref, be_ref, pos_ref, rows_down_ref, rows_up_ref, hs_ref, gate_ref):
    tt = gate_ref.shape[0]
    half = PEER_SLOTS // 2

    def gather_order(a):
        for c in range(tt // LANES):
            cols = slice(c * LANES, (c + 1) * LANES)
            tile_ref = pos_ref.at[c]
            tile_ref[pl.ds(0, half, stride=2), :] = a[:half, cols]
            tile_ref[pl.ds(1, half, stride=2), :] = a[half:, cols]
        return jnp.concatenate([pos_ref[c].T for c in range(tt // LANES)], axis=0)

    ids = be_ref[...]
    gate_ref[...] = gather_order(bs_ref[...])
    hs_ref[...] = (gather_order(ids).astype(jnp.int32) // HALF_EXPERTS).astype(F32)
    lo = ids.T[:, :half].astype(jnp.int32)
    hi = jnp.concatenate([ids[half:], ids[half:]], axis=0).T[:, :half].astype(jnp.int32)
    up_off = lambda e: (e % HALF_EXPERTS) * SUBLANES
    rows_up_ref[...] = up_off(lo) | (up_off(hi) << 16)
    rows_down_ref[...] = (lo * DOWN_SLAB_ROWS) | ((hi * DOWN_SLAB_ROWS) << 16)


N_ROUTE_IN, N_DOWN_IN, N_ROUTE_OUT, N_ROUTE_SCRATCH = 8, 5, 5, 8


def _route_down_kernel(*refs, do_route, do_down):
    refs = list(refs)
    take = lambda count: [refs.pop(0) for _ in range(count)]
    route_in = take(N_ROUTE_IN) if do_route else None
    down_in = take(N_DOWN_IN) if do_down else None
    route_out = take(N_ROUTE_OUT) if do_route else None
    w_ref = take(1)[0] if do_down else None
    route_scratch = take(N_ROUTE_SCRATCH) if do_route else None
    down_scratch = take(2) if do_down else None

    if do_route:
        x_ref, shift_ref, scale_ref, gain_ref, wqh_ref, wql_ref, skh_ref, skl_ref = route_in
        rows_down_ref, rows_up_ref, hs_ref, gate_ref, xs_ref = route_out
        xh_ref, xl_ref, q_ref, tv_ref, ti_ref, bs_ref, be_ref, pos_ref = route_scratch
        _route_prologue(x_ref, shift_ref, scale_ref, gain_ref, xs_ref, xh_ref, xl_ref)
    if do_down:
        idx_ref, xs_in_ref, gate_in_ref, tab_ref, sel_ref = down_in
        gb_ref, z_ref = down_scratch

    quarter = TOKEN_GROUP // 4

    def step(h, carry):
        rows = []
        if do_route:
            _route_query(h, xh_ref, xl_ref, wqh_ref, wql_ref, q_ref)
        for part in range(4):
            if do_down:
                rows += _down_tokens(h, part * quarter, (part + 1) * quarter, idx_ref, xs_in_ref, tab_ref, gb_ref)
            if do_route and part < 2:
                _route_sub_keys(part, skh_ref, skl_ref, q_ref, tv_ref, ti_ref)
            if do_route and part == 2:
                _route_head(h, tv_ref, ti_ref, bs_ref, be_ref)
        if do_down:
            _store_z(h, rows, z_ref)
        return carry

    lax.fori_loop(0, PEER_HEADS, step, 0)
    if do_route:
        _route_epilogue(bs_ref, be_ref, pos_ref, rows_down_ref, rows_up_ref, hs_ref, gate_ref)
    if do_down:
        _down_epilogue(z_ref, sel_ref, gate_in_ref, w_ref)


ROUTE_DOWN_VMEM_LIMIT = 56 * 1024 * 1024
PEER_CHUNKS_PER_SEQ = 4


def _route_down(route_args, down_args, seq_len, *, interpret=False):
    tt = PEER_TOKEN_TILE
    assert tt // TOKEN_GROUP == PEER_HEADS and seq_len % tt == 0
    tiles = seq_len // tt
    tok = lambda i: (i, 0)
    whole = pl.BlockSpec(memory_space=pltpu.VMEM)
    slot_spec = pl.BlockSpec((tt, PEER_SLOTS), tok)
    pair_spec = pl.BlockSpec((tt, PEER_SLOTS // 2), tok)
    slot_shape = jax.ShapeDtypeStruct((seq_len, PEER_SLOTS), F32)
    pair_shape = jax.ShapeDtypeStruct((seq_len, PEER_SLOTS // 2), jnp.int32)
    in_specs, args, out_specs, out_shape, scratch = [], [], [], [], []
    if route_args is not None:
        x2, chunk, seq, shift, scale, gain, wq_hi, wq_lo, sk_hi, sk_lo = route_args
        batch = shift.shape[0]
        per_seq = pl.BlockSpec((1, 1, D_MODEL), lambda i: (seq, 0, 0))
        in_specs += [pl.BlockSpec((tt, D_MODEL), lambda i: (chunk * tiles + i, 0)), per_seq, per_seq,
                     pl.BlockSpec((1, D_MODEL), lambda i: (0, 0)), whole, whole, whole, whole]
        args += [x2, shift.reshape(batch, 1, D_MODEL), scale.reshape(batch, 1, D_MODEL),
                 gain.reshape(1, D_MODEL), wq_hi, wq_lo, sk_hi, sk_lo]
    if down_args is not None:
        rows_down, xs, gate, tab_u = down_args
        in_specs += [pl.BlockSpec((tt, PEER_SLOTS // 2), tok, memory_space=pltpu.SMEM),
                     pl.BlockSpec((tt, SUBROWS, LANES), lambda i: (i, 0, 0)), slot_spec, whole,
                     pl.BlockSpec((DOWN_ROWS, PEER_SLOTS), lambda i: (0, 0))]
        args += [rows_down, xs.reshape(seq_len, SUBROWS, LANES), gate, tab_u, _down_select_matrix()]
    if route_args is not None:
        out_specs += [pair_spec, pair_spec, slot_spec, slot_spec, pl.BlockSpec((tt, 2 * D_MODEL), tok)]
        out_shape += [pair_shape, pair_shape, slot_shape, slot_shape,
                      jax.ShapeDtypeStruct((seq_len, 2 * D_MODEL), BF16)]
    if down_args is not None:
        out_specs.append(slot_spec)
        out_shape.append(slot_shape)
    if route_args is not None:
        scratch += [pltpu.VMEM((tt, D_MODEL), BF16),
                    pltpu.VMEM((tt, D_MODEL), BF16),
                    pltpu.VMEM((tt, PEER_QUERY_DIM), F32),
                    pltpu.VMEM((2, PEER_TOPK, tt), F32),
                    pltpu.VMEM((2, PEER_TOPK, tt), jnp.int32),
                    pltpu.VMEM((PEER_SLOTS, tt), F32),
                    pltpu.VMEM((PEER_SLOTS, tt), F32),
                    pltpu.VMEM((tt // LANES, PEER_SLOTS, LANES), F32)]
    if down_args is not None:
        scratch += [pltpu.VMEM((PEER_SLOTS // 2 * SUBLANES, LANES), jnp.int32),
                    pltpu.VMEM((tt, DOWN_ROWS), F32)]
    outs = pl.pallas_call(
        functools.partial(_route_down_kernel, do_route=route_args is not None, do_down=down_args is not None),
        grid=(tiles,), in_specs=in_specs, out_specs=out_specs, out_shape=out_shape, scratch_shapes=scratch,
        compiler_params=pltpu.CompilerParams(dimension_semantics=("arbitrary",),
                                             vmem_limit_bytes=ROUTE_DOWN_VMEM_LIMIT),
        interpret=interpret, name="peer_route_down",
    )(*args)
    routed = tuple(outs[:N_ROUTE_OUT]) if route_args is not None else None
    w = outs[-1] if down_args is not None else None
    return routed, w


def _peer(x, gain, shift, scale, res_gate, wq, subkeys, tab_u, tab_v):
    B, S, D = x.shape
    x2 = x.reshape(B * S, D)
    per_head = lambda w: w.reshape(D, PEER_HEADS, PEER_QUERY_DIM).transpose(1, 0, 2)
    wq_hi, wq_lo = _split_hi_lo(per_head(wq))
    sk_hi, sk_lo = _split_hi_lo(subkeys)
    routed, rows_up, hs, ws = None, [], [], []
    chunk_len = S // PEER_CHUNKS_PER_SEQ
    n_chunks = B * PEER_CHUNKS_PER_SEQ
    for c in range(n_chunks + 1):
        route_args = ((x2, c, c // PEER_CHUNKS_PER_SEQ, shift, scale, gain, wq_hi, wq_lo, sk_hi, sk_lo)
                      if c < n_chunks else None)
        down_args = (routed[0], routed[4], routed[3], tab_u) if routed is not None else None
        routed, w = _route_down(route_args, down_args, chunk_len)
        if w is not None:
            ws.append(w)
        if routed is not None:
            rows_up.append(routed[1])
            hs.append(routed[2])
    y = _peer_up(jnp.concatenate(rows_up), jnp.concatenate(ws), jnp.concatenate(hs), x2, res_gate, tab_v, S)
    return y.reshape(B, S, D)


PROJ_TOKEN_TILE = 512
PROJ_VMEM_LIMIT = 48 * 1024 * 1024


def _ada_norm_tile(x, gain, shift, scale):
    xn = x * lax.rsqrt(jnp.mean(x * x, axis=-1, keepdims=True) + EPS) * gain
    return xn * (1.0 + scale) + shift


def _norm_proj_kernel(x_ref, shift_ref, scale_ref, gain_ref, w_ref, *rest, with_gate):
    x_hi, x_lo = _split_hi_lo(_ada_norm_tile(x_ref[...], gain_ref[...], shift_ref[0], scale_ref[0]))
    if with_gate:
        wgh_ref, wgl_ref, main_ref, gate_ref = rest
        gate_ref[...] = _dot3(x_hi, x_lo, wgh_ref[...], wgl_ref[...], ((1,), (0,)))
    else:
        (main_ref,) = rest
    main_ref[...] = jnp.dot(x_hi, w_ref[...], preferred_element_type=F32)


def _norm_proj(x2, shift, scale, gain, w_main, w_gate, seq_len, *, token_tile=PROJ_TOKEN_TILE, interpret=False):
    n = x2.shape[0]
    tt = token_tile
    assert seq_len % tt == 0 and n % seq_len == 0
    tiles_per_seq = seq_len // tt
    batch = n // seq_len
    m = w_main.shape[1]
    tok = lambda i: (i, 0)
    per_seq = pl.BlockSpec((1, 1, D_MODEL), lambda i: (i // tiles_per_seq, 0, 0))
    whole = pl.BlockSpec(memory_space=pltpu.VMEM)
    in_specs = [pl.BlockSpec((tt, D_MODEL), tok), per_seq, per_seq,
                pl.BlockSpec((1, D_MODEL), lambda i: (0, 0)), whole]
    args = [x2, shift.reshape(batch, 1, D_MODEL), scale.reshape(batch, 1, D_MODEL), gain.reshape(1, D_MODEL),
            w_main.astype(BF16)]
    out_specs = [pl.BlockSpec((tt, m), tok)]
    out_shape = [jax.ShapeDtypeStruct((n, m), F32)]
    if w_gate is not None:
        wg_hi, wg_lo = _split_hi_lo(jnp.pad(w_gate, ((0, 0), (0, LANES - w_gate.shape[1]))))
        in_specs += [whole, whole]
        args += [wg_hi, wg_lo]
        out_specs.append(pl.BlockSpec((tt, LANES), tok))
        out_shape.append(jax.ShapeDtypeStruct((n, LANES), F32))
    outs = pl.pallas_call(
        functools.partial(_norm_proj_kernel, with_gate=w_gate is not None),
        grid=(n // tt,), in_specs=in_specs, out_specs=out_specs, out_shape=out_shape,
        compiler_params=pltpu.CompilerParams(dimension_semantics=("arbitrary",),
                                             vmem_limit_bytes=PROJ_VMEM_LIMIT),
        interpret=interpret, name="norm_proj",
    )(*args)
    return (outs[0], outs[1]) if w_gate is not None else (outs[0], None)


def _out_proj_kernel(h_ref, og_ref, x_ref, g_ref, hg_ref, w_ref, o_ref, *, head_dim):
    h = h_ref[...]
    if head_dim is not None:
        parts = []
        for j in range(D_MODEL // head_dim):
            hb = h[:, j * head_dim:(j + 1) * head_dim]
            parts.append(hb * lax.rsqrt(jnp.mean(hb * hb, axis=-1, keepdims=True) + EPS))
        h = jnp.concatenate(parts, axis=1) * hg_ref[...]
    a = jax.nn.sigmoid(og_ref[...]) * h
    y = jnp.dot(a.astype(BF16), w_ref[...], preferred_element_type=F32)
    o_ref[...] = x_ref[...] + g_ref[0] * y


def _out_proj(h2, p, og_block, x2, g, h_gain, w_out, seq_len, head_dim, *, token_tile=PROJ_TOKEN_TILE,
              interpret=False):
    n = x2.shape[0]
    tt = token_tile
    tiles_per_seq = seq_len // tt
    batch = n // seq_len
    tok = lambda i: (i, 0)
    row = pl.BlockSpec((tt, D_MODEL), tok)
    gain = jnp.ones((1, D_MODEL), F32) if h_gain is None else h_gain.reshape(1, D_MODEL)
    return pl.pallas_call(
        functools.partial(_out_proj_kernel, head_dim=head_dim),
        grid=(n // tt,),
        in_specs=[row, pl.BlockSpec((tt, D_MODEL), lambda i: (i, og_block)), row,
                  pl.BlockSpec((1, 1, D_MODEL), lambda i: (i // tiles_per_seq, 0, 0)),
                  pl.BlockSpec((1, D_MODEL), lambda i: (0, 0)),
                  pl.BlockSpec(memory_space=pltpu.VMEM)],
        out_specs=row,
        out_shape=jax.ShapeDtypeStruct((n, D_MODEL), F32),
        compiler_params=pltpu.CompilerParams(dimension_semantics=("arbitrary",),
                                             vmem_limit_bytes=PROJ_VMEM_LIMIT),
        interpret=interpret, name="out_proj",
    )(h2, p, x2, g.reshape(batch, 1, D_MODEL), gain, w_out.astype(BF16))


MLSTM_STATE_W = MLSTM_V_DIM + LANES


def _mlstm_kernel(q_ref, k_ref, v_ref, gc_ref, gr_ref, h_ref, cn_ref, m_ref):
    H, DQK, DV, L = MLSTM_HEADS, MLSTM_QK_DIM, MLSTM_V_DIM, MLSTM_CHUNK

    @pl.when(pl.program_id(1) == 0)
    def _():
        cn_ref[...] = jnp.zeros_like(cn_ref)
        m_ref[...] = jnp.zeros_like(m_ref)

    tril = lax.broadcasted_iota(jnp.int32, (L, L), 1) <= lax.broadcasted_iota(jnp.int32, (L, L), 0)
    gc = gc_ref[...]
    gr = gr_ref[0]
    ones = jnp.ones((L, LANES), F32)
    for h in range(H):
        ig_col, b_col = gc[:, h:h + 1], gc[:, H + h:H + h + 1]
        ig_row, b_row = gr[h:h + 1, :], gr[H + h:H + h + 1, :]
        m_prev = m_ref[h:h + 1, 0:1]
        d_log = jnp.where(tril, b_col - b_row + ig_row, -jnp.inf)
        inter = b_col + m_prev
        m_t = jnp.maximum(inter, jnp.max(d_log, axis=1, keepdims=True))
        w = jnp.exp(d_log - m_t)
        a_inter = jnp.exp(inter - m_t)
        qh = (q_ref[:, h * DQK:(h + 1) * DQK] * (DQK ** -0.5)).astype(BF16)
        kf = k_ref[:, h * DQK:(h + 1) * DQK]
        vh = v_ref[:, h * DV:(h + 1) * DV]
        s = lax.dot_general(qh, kf.astype(BF16), (((1,), (1,)), ((), ())), preferred_element_type=F32) * w
        cn = cn_ref[h]
        qc = jnp.dot(qh, cn.astype(BF16), preferred_element_type=F32)
        num = a_inter * qc[:, :DV] + jnp.dot(s.astype(BF16), vh.astype(BF16), preferred_element_type=F32)
        den = a_inter * qc[:, DV:DV + 1] + jnp.sum(s, axis=1, keepdims=True)
        h_ref[:, h * DV:(h + 1) * DV] = num / jnp.maximum(jnp.abs(den), jnp.exp(-m_t))
        b_last = b_col[L - 1:L, :]
        g_col = b_last - b_col + ig_col
        m_new = jnp.maximum(b_last + m_prev, jnp.max(g_col, axis=0, keepdims=True))
        w_s = jnp.exp(g_col - m_new)
        decay = jnp.exp(b_last + m_prev - m_new)
        kw = (kf * w_s).astype(BF16)
        vaug = jnp.concatenate([vh, ones], axis=1).astype(BF16)
        cn_ref[h] = decay * cn + lax.dot_general(kw, vaug, (((0,), (0,)), ((), ())), preferred_element_type=F32)
        m_ref[h:h + 1, :] = jnp.broadcast_to(m_new, (1, LANES))


def _mlstm_scan(p, gates, b_i, b_f, batch, seq_len, *, interpret=False):
    H, L = MLSTM_HEADS, MLSTM_CHUNK
    n = p.shape[0]
    nc = seq_len // L
    ig = _softcap(gates[:, :H] + b_i)
    lf = jax.nn.log_sigmoid(_softcap(gates[:, H:2 * H] + b_f))
    b = jnp.cumsum(lf.reshape(n // L, L, H), axis=1).reshape(n, H)
    gc = jnp.concatenate([ig, b], axis=1)
    gr = gc.reshape(n // L, L, 2 * H).transpose(0, 2, 1)
    chunk = lambda bi, c: bi * nc + c
    return pl.pallas_call(
        _mlstm_kernel,
        grid=(batch, nc),
        in_specs=[pl.BlockSpec((L, MLSTM_QK_W), lambda bi, c: (chunk(bi, c), 0)),
                  pl.BlockSpec((L, MLSTM_QK_W), lambda bi, c: (chunk(bi, c), 1)),
                  pl.BlockSpec((L, MLSTM_V_W), lambda bi, c: (chunk(bi, c), 1)),
                  pl.BlockSpec((L, 2 * H), lambda bi, c: (chunk(bi, c), 0)),
                  pl.BlockSpec((1, 2 * H, L), lambda bi, c: (chunk(bi, c), 0, 0))],
        out_specs=pl.BlockSpec((L, D_MODEL), lambda bi, c: (chunk(bi, c), 0)),
        out_shape=jax.ShapeDtypeStruct((n, D_MODEL), F32),
        scratch_shapes=[pltpu.VMEM((H, MLSTM_QK_DIM, MLSTM_STATE_W), F32),
                        pltpu.VMEM((SUBLANES, LANES), F32)],
        compiler_params=pltpu.CompilerParams(dimension_semantics=("arbitrary", "arbitrary")),
        interpret=interpret, name="mlstm_scan",
    )(p, p, p, gc, gr)


FOX_BLOCK = 1024
FOX_QUERY_PART = 256
FOX_SUM_ROWS = 16
FOX_VMEM_LIMIT = 48 * 1024 * 1024


def _bias_columns(f, query_side):
    hi = f.astype(BF16).astype(F32)
    r1 = f - hi
    lo = r1.astype(BF16).astype(F32)
    lo2 = r1 - lo
    lane = lax.broadcasted_iota(jnp.int32, (f.shape[0], LANES), 1)
    if query_side:
        vals = jnp.where(lane == 0, hi, jnp.where(lane == 1, lo, jnp.where(lane == 2, lo2,
                         jnp.where(lane < 6, 1.0, 0.0))))
    else:
        vals = jnp.where(lane < 3, 1.0, jnp.where(lane == 3, -hi, jnp.where(lane == 4, -lo,
                         jnp.where(lane == 5, -lo2, 0.0))))
    return vals.astype(BF16)


def _head_column(fc, h):
    lane = lax.broadcasted_iota(jnp.int32, fc.shape, 1)
    return jnp.sum(jnp.where(lane == h, fc, 0.0), axis=1, keepdims=True)


def _rms_rows(t, gain):
    return t * lax.rsqrt(jnp.mean(t * t, axis=-1, keepdims=True) + EPS) * gain


def _fox_attn_kernel(q_ref, k_ref, v_ref, fq_ref, fk_ref, qg_ref, kg_ref, o_ref,
                     ka_ref, vt_ref, qa_ref, st_cur_ref, st_next_ref, *state_refs):
    i = pl.program_id(1)
    h = pl.program_id(0) % FOX_HEADS
    tq = q_ref.shape[0]
    nk = k_ref.shape[0] // tq
    hd = FOX_HEAD_DIM
    part = FOX_QUERY_PART
    n_parts = tq // part
    m_refs, acc_refs = state_refs[:n_parts], state_refs[n_parts:]

    @pl.when(i == 0)
    def _():
        def prep(c, carry):
            r = pl.ds(pl.multiple_of(c * tq, tq), tq)
            ka_ref[r, :hd] = _rms_rows(k_ref[r, :], kg_ref[...]).astype(BF16)
            ka_ref[r, hd:] = _bias_columns(_head_column(fk_ref[r, :], h), False)
            vt_ref[c, :hd, :] = v_ref[r, :].T.astype(BF16)
            vt_ref[c, hd:, :] = jnp.ones((FOX_SUM_ROWS, tq), BF16)
            return carry
        lax.fori_loop(0, nk, prep, 0)

    qn = _rms_rows(q_ref[...], qg_ref[...]) * (hd ** -0.5)
    qa_ref[:, :hd] = qn.astype(BF16)
    qa_ref[:, hd:] = _bias_columns(_head_column(fq_ref[...], h), True)
    for m_ref, acc_ref in zip(m_refs, acc_refs):
        m_ref[...] = jnp.full_like(m_ref, -jnp.inf)
        acc_ref[...] = jnp.zeros_like(acc_ref)

    def scores(j, st_ref):
        kblk = ka_ref[pl.ds(pl.multiple_of(j * tq, tq), tq), :]
        for c in range(n_parts):
            st_ref[c] = lax.dot_general(kblk, qa_ref[c * part:(c + 1) * part, :],
                                        (((1,), (1,)), ((), ())), preferred_element_type=F32)

    def softmax_pv(j, masked):
        vtblk = vt_ref[j]
        pts, alphas = [], []
        for c, m_ref in enumerate(m_refs):
            st = st_cur_ref[c]
            if masked:
                key = lax.broadcasted_iota(jnp.int32, st.shape, 0)
                qry = lax.broadcasted_iota(jnp.int32, st.shape, 1) + c * part
                st = jnp.where(key <= qry, st, -jnp.inf)
            m_prev = m_ref[0:1, :]
            m_new = jnp.maximum(m_prev, jnp.max(st, axis=0, keepdims=True))
            alphas.append(jnp.exp(m_prev - m_new))
            pts.append(jnp.exp(st - m_new).astype(BF16))
            m_ref[0:1, :] = m_new
        for pt, alpha, acc_ref in zip(pts, alphas, acc_refs):
            acc_ref[...] = alpha * acc_ref[...] + jnp.dot(vtblk, pt, preferred_element_type=F32)

    def body(j, carry):
        scores(j + 1, st_next_ref)
        softmax_pv(j, False)
        st_cur_ref[...] = st_next_ref[...]
        return carry

    scores(0, st_cur_ref)
    lax.fori_loop(0, i, body, 0)
    softmax_pv(i, True)
    for c, acc_ref in enumerate(acc_refs):
        acc = acc_ref[...]
        o_ref[c * part:(c + 1) * part, :] = (acc[:hd] / acc[hd:hd + 1]).T


def _fox_attention(pq, kv, f_cum, q_gain, k_gain, batch, seq_len, *, block=FOX_BLOCK, interpret=False):
    n = pq.shape[0]
    hd, nh = FOX_HEAD_DIM, FOX_HEADS
    nq = seq_len // block
    return pl.pallas_call(
        _fox_attn_kernel,
        grid=(batch * nh, nq),
        in_specs=[pl.BlockSpec((block, hd), lambda bh, i: ((bh // nh) * nq + i, bh % nh)),
                  pl.BlockSpec((seq_len, hd), lambda bh, i: (bh // nh, bh % nh)),
                  pl.BlockSpec((seq_len, hd), lambda bh, i: (bh // nh, nh + bh % nh)),
                  pl.BlockSpec((block, nh), lambda bh, i: ((bh // nh) * nq + i, 0)),
                  pl.BlockSpec((seq_len, nh), lambda bh, i: (bh // nh, 0)),
                  pl.BlockSpec((1, hd), lambda bh, i: (0, 0)),
                  pl.BlockSpec((1, hd), lambda bh, i: (0, 0))],
        out_specs=pl.BlockSpec((block, hd), lambda bh, i: ((bh // nh) * nq + i, bh % nh)),
        out_shape=jax.ShapeDtypeStruct((n, D_MODEL), F32),
        scratch_shapes=[pltpu.VMEM((seq_len, 2 * hd), BF16),
                        pltpu.VMEM((nq, hd + FOX_SUM_ROWS, block), BF16),
                        pltpu.VMEM((block, 2 * hd), BF16),
                        *[pltpu.VMEM((block // FOX_QUERY_PART, block, FOX_QUERY_PART), F32)] * 2,
                        *[pltpu.VMEM((SUBLANES, FOX_QUERY_PART), F32)] * (block // FOX_QUERY_PART),
                        *[pltpu.VMEM((hd + FOX_SUM_ROWS, FOX_QUERY_PART), F32)] * (block // FOX_QUERY_PART)],
        compiler_params=pltpu.CompilerParams(dimension_semantics=("arbitrary", "arbitrary"),
                                             vmem_limit_bytes=FOX_VMEM_LIMIT),
        interpret=interpret, name="fox_attention",
    )(pq, kv, kv, f_cum, f_cum, q_gain.reshape(1, hd), k_gain.reshape(1, hd))


def kernel(x, c, ada_w, ada_b, mix_norm, ffn_norm, a_w_in, a_b_i, a_b_f, a_h_norm, a_w_out,
           kv_ada_w, kv_ada_b, kv_norm, kv_w, kv_b_f, kv_k_norm, b_w_qo, b_q_norm, b_w_out,
           peer_wq, peer_subkeys, peer_u, peer_v):
    B, S, D = x.shape
    n = B * S
    cs = jax.nn.silu(c)
    x2 = x.reshape(n, D)
    kv = f_cum = None
    for l in range(DEPTH):
        mod = cs @ ada_w[l] + ada_b[l]
        sh1, sc1, g1, sh2, sc2, g2 = jnp.split(mod, N_ADA, axis=-1)
        if l < N_A_LAYERS:
            split = 2 * MLSTM_QK_W + MLSTM_V_W + D_MODEL
            p, gates = _norm_proj(x2, sh1, sc1, mix_norm[l], a_w_in[l][:, :split], a_w_in[l][:, split:], S)
            h = _mlstm_scan(p, gates, a_b_i[l], a_b_f[l], B, S)
            x2 = _out_proj(h, p, 2, x2, g1, a_h_norm[l], a_w_out[l], S, MLSTM_V_DIM)
        else:
            j = l - N_A_LAYERS
            pq, _ = _norm_proj(x2, sh1, sc1, mix_norm[l], b_w_qo[j], None, S)
            att = _fox_attention(pq, kv, f_cum, b_q_norm[j], kv_k_norm, B, S)
            x2 = _out_proj(att, pq, 1, x2, g1, None, b_w_out[j], S, None)
        x2 = _peer(x2.reshape(B, S, D), ffn_norm[l], sh2, sc2, g2, peer_wq[l], peer_subkeys[l],
                   _pack_down_table(peer_u[l]), _pack_expert_table(peer_v[l])).reshape(n, D)
        if l == N_A_LAYERS - 1:
            sh, sc = jnp.split(cs @ kv_ada_w + kv_ada_b, 2, axis=-1)
            kv, fg = _norm_proj(x2, sh, sc, kv_norm, kv_w[:, :2 * D], kv_w[:, 2 * D:], S)
            log_f = jax.nn.log_sigmoid(fg[:, :FOX_HEADS] + kv_b_f)
            f_cum = jnp.cumsum(log_f.reshape(B, S, FOX_HEADS), axis=1).reshape(n, FOX_HEADS)
    return x2.reshape(B, S, D)
```
